```python
import math
import jax, jax.numpy as jnp
from jax import lax
import numpy as np

D_MODEL = 1024
BATCH = 32
SEQ = 2048
DEPTH = 4

HEAD_DIM = 64
NSA_HEADS = 16
NSA_GROUPS = 2
NSA_HPG = NSA_HEADS // NSA_GROUPS
NSA_WIDTH = NSA_HEADS * HEAD_DIM
CMP_BLOCK = 32
CMP_STRIDE = 16
CMP_HIDDEN = 2 * HEAD_DIM
SEL_BLOCK = 64
SEL_TOPK = 8
WINDOW = 512
SEL_Q_BLOCK = 64
FOX_HEADS = 16
FOX_WIDTH = FOX_HEADS * HEAD_DIM
Q_BLOCK = 128
ROPE_THETA = 10000.0
RMS_EPS = 1e-6
NEG_INF = -1e30
FORCED_SCORE = 1e4

SPLITS = (
    NSA_WIDTH,
    3 * 2 * NSA_GROUPS * HEAD_DIM,
    3 * NSA_HEADS,
    NSA_WIDTH,
    FOX_WIDTH,
    FOX_WIDTH,
    FOX_WIDTH,
    FOX_HEADS,
    FOX_WIDTH,
    2 * D_MODEL,
)
IN_COLS = sum(SPLITS)

kernel_name = 'nsa_fox_gated_hybrid'


def rms_norm(x, g):
    xf = x.astype(jnp.float32)
    y = xf * lax.rsqrt(jnp.mean(xf * xf, axis=-1, keepdims=True) + RMS_EPS)
    return (y * g.astype(jnp.float32)).astype(x.dtype)


def rope_tables(T, dtype):
    inv = ROPE_THETA ** (-jnp.arange(0, HEAD_DIM, 2, dtype=jnp.float32) / HEAD_DIM)
    ang = jnp.arange(T, dtype=jnp.float32)[:, None] * inv[None, :]
    return jnp.cos(ang).astype(dtype), jnp.sin(ang).astype(dtype)


def rope(x, cos, sin):
    d2 = HEAD_DIM // 2
    shape = (1, x.shape[1]) + (1,) * (x.ndim - 3) + (d2,)
    c, s = cos.reshape(shape), sin.reshape(shape)
    x1, x2 = x[..., :d2], x[..., d2:]
    return jnp.concatenate([x1 * c - x2 * s, x2 * c + x1 * s], axis=-1)


def masked_softmax(s, mask):
    p = jax.nn.softmax(jnp.where(mask, s, NEG_INF), axis=-1)
    return jnp.where(mask, p, 0.0)


def compress(k, pe, w1, w2):
    T = k.shape[1]
    n_cmp = (T - CMP_BLOCK) // CMP_STRIDE + 1
    idx = np.arange(n_cmp)[:, None] * CMP_STRIDE + np.arange(CMP_BLOCK)[None, :]
    blk = k[:, idx] + pe[None, None, :, None, :]
    hid = jax.nn.silu(jnp.einsum('bnlgd,ldh->bngh', blk, w1.reshape(CMP_BLOCK, HEAD_DIM, CMP_HIDDEN)))
    return jnp.einsum('bngh,hd->bngd', hid, w2)


def nsa_mixer(q, kv, gate_logits, pe_k, w1_k, w2_k, pe_v, w1_v, w2_v, cos, sin):
    B, T = q.shape[0], q.shape[1]
    G, HPG, D = NSA_GROUPS, NSA_HPG, HEAD_DIM
    scale = D ** -0.5
    q = q.reshape(B, T, NSA_HEADS, D)
    kv = kv.reshape(B, T, 6, G, D)
    k_c, v_c, k_s, v_s, k_w, v_w = [kv[:, :, i] for i in range(6)]
    t = jnp.arange(T)

    k_cmp = compress(k_c, pe_k, w1_k, w2_k)
    v_cmp = compress(v_c, pe_v, w1_v, w2_v)
    n_cmp = k_cmp.shape[1]
    qg = q.reshape(B, T, G, HPG, D)
    s_cmp = jnp.einsum('btghd,bngd->bghtn', qg, k_cmp).astype(jnp.float32) * scale
    blk_end = jnp.arange(n_cmp) * CMP_STRIDE + CMP_BLOCK - 1
    cmp_mask = blk_end[None, :] <= t[:, None]
    p_cmp = masked_softmax(s_cmp, cmp_mask)
    o_cmp = jnp.einsum('bghtn,bngd->btghd', p_cmp.astype(v_cmp.dtype), v_cmp)

    n_sel = T // SEL_BLOCK
    ci = np.arange(n_cmp)[:, None] * CMP_STRIDE
    sj = np.arange(n_sel)[None, :] * SEL_BLOCK
    overlap = jnp.asarray(((ci < sj + SEL_BLOCK) & (ci + CMP_BLOCK > sj)).astype(np.float32))
    imp = jnp.einsum('bghtn,nj->bgtj', p_cmp, overlap)
    sel_ids = jnp.arange(n_sel)
    cur = t // SEL_BLOCK
    forced = (sel_ids[None, :] == 0) | (sel_ids[None, :] == cur[:, None]) | (sel_ids[None, :] == cur[:, None] - 1)
    valid = sel_ids[None, :] * SEL_BLOCK <= t[:, None]
    score = jnp.where(forced, FORCED_SCORE, jnp.where(valid, imp, -1.0))
    k_top = min(SEL_TOPK, n_sel)
    _, sel_idx = lax.top_k(score, k_top)

    qr = rope(q, cos, sin).reshape(B, T, G, HPG, D)
    kb = rope(k_s, cos, sin).reshape(B, n_sel, SEL_BLOCK, G, D).transpose(0, 3, 1, 2, 4)
    vb = v_s.reshape(B, n_sel, SEL_BLOCK, G, D).transpose(0, 3, 1, 2, 4)
    n_qs = T // SEL_Q_BLOCK
    q_chunks = jnp.moveaxis(qr.reshape(B, n_qs, SEL_Q_BLOCK, G, HPG, D), 1, 0)
    idx_chunks = jnp.moveaxis(sel_idx.reshape(B, G, n_qs, SEL_Q_BLOCK, k_top), 2, 0)
    t_chunks = t.reshape(n_qs, SEL_Q_BLOCK)
    bi = jnp.arange(B)[:, None, None, None]
    gi = jnp.arange(G)[None, :, None, None]

    def sel_step(args):
        qc, ic, tc = args
        kg = kb[bi, gi, ic]
        vg = vb[bi, gi, ic]
        s = jnp.einsum('bqghd,bgqnld->bghqnl', qc, kg).astype(jnp.float32) * scale
        tok = ic[..., None] * SEL_BLOCK + jnp.arange(SEL_BLOCK)
        mask = (tok <= tc[None, None, :, None, None])[:, :, None]
        p = masked_softmax(s.reshape(B, G, HPG, SEL_Q_BLOCK, k_top * SEL_BLOCK),
                           mask.reshape(B, G, 1, SEL_Q_BLOCK, k_top * SEL_BLOCK))
        p = p.reshape(B, G, HPG, SEL_Q_BLOCK, k_top, SEL_BLOCK)
        return jnp.einsum('bghqnl,bgqnld->bqghd', p.astype(vg.dtype), vg)

    o_slc = lax.map(sel_step, (q_chunks, idx_chunks, t_chunks))
    o_slc = jnp.moveaxis(o_slc, 0, 1).reshape(B, T, G, HPG, D)

    kwp = jnp.pad(rope(k_w, cos, sin), ((0, 0), (WINDOW, 0), (0, 0), (0, 0)))
    vwp = jnp.pad(v_w, ((0, 0), (WINDOW, 0), (0, 0), (0, 0)))
    n_qw = T // Q_BLOCK
    qw_chunks = jnp.moveaxis(qr.reshape(B, n_qw, Q_BLOCK, G, HPG, D), 1, 0)
    starts = jnp.arange(n_qw, dtype=jnp.int32) * Q_BLOCK

    def win_step(args):
        qc, s0 = args
        kc = lax.dynamic_slice_in_dim(kwp, s0, WINDOW + Q_BLOCK, axis=1)
        vc = lax.dynamic_slice_in_dim(vwp, s0, WINDOW + Q_BLOCK, axis=1)
        tq = s0 + jnp.arange(Q_BLOCK)
        sk = s0 - WINDOW + jnp.arange(WINDOW + Q_BLOCK)
        mask = (sk[None, :] <= tq[:, None]) & (sk[None, :] > tq[:, None] - WINDOW) & (sk[None, :] >= 0)
        s = jnp.einsum('bqghd,bkgd->bghqk', qc, kc).astype(jnp.float32) * scale
        p = masked_softmax(s, mask)
        return jnp.einsum('bghqk,bkgd->bqghd', p.astype(vc.dtype), vc)

    o_win = lax.map(win_step, (qw_chunks, starts))
    o_win = jnp.moveaxis(o_win, 0, 1).reshape(B, T, G, HPG, D)

    g = jax.nn.sigmoid(gate_logits).reshape(B, T, 3, G, HPG)[..., None]
    o = g[:, :, 0] * o_cmp + g[:, :, 1] * o_slc + g[:, :, 2] * o_win
    return o.reshape(B, T, NSA_WIDTH)


def fox_mixer(q, k, v, f_logits, b_f):
    B, T = q.shape[0], q.shape[1]
    H, D = FOX_HEADS, HEAD_DIM
    scale = D ** -0.5
    q = q.reshape(B, T, H, D)
    k = k.reshape(B, T, H, D)
    v = v.reshape(B, T, H, D)
    log_f = jax.nn.log_sigmoid(f_logits.astype(jnp.float32) + b_f.astype(jnp.float32))
    F = jnp.cumsum(log_f, axis=1).transpose(0, 2, 1)
    n_q = T // Q_BLOCK
    q_chunks = jnp.moveaxis(q.reshape(B, n_q, Q_BLOCK, H, D), 1, 0)
    F_chunks = jnp.moveaxis(F.reshape(B, H, n_q, Q_BLOCK), 2, 0)
    starts = jnp.arange(n_q, dtype=jnp.int32) * Q_BLOCK
    key_pos = jnp.arange(T)

    def fox_step(args):
        qc, Fq, s0 = args
        s = jnp.einsum('bqhd,bkhd->bhqk', qc, k).astype(jnp.float32) * scale
        s = s + Fq[..., None] - F[:, :, None, :]
        mask = key_pos[None, :] <= (s0 + jnp.arange(Q_BLOCK))[:, None]
        p = masked_softmax(s, mask)
        return jnp.einsum('bhqk,bkhd->bqhd', p.astype(v.dtype), v)

    o = lax.map(fox_step, (q_chunks, F_chunks, starts))
    return jnp.moveaxis(o, 0, 1).reshape(B, T, FOX_WIDTH)


def setup_inputs(seed: int = 0) -> dict:
    key = jax.random.key(seed)
    ks = jax.random.split(key, 14)
    f32 = jnp.float32

    def nrm(k, shape, scale):
        return jax.random.normal(k, shape, f32) * scale

    return {
        'x': nrm(ks[0], (BATCH, SEQ, D_MODEL), 1.0),
        'norm_g': 1.0 + nrm(ks[1], (DEPTH, D_MODEL), 0.01),
        'w_in': nrm(ks[2], (DEPTH, D_MODEL, IN_COLS), D_MODEL ** -0.5),
        'b_forget': jax.random.uniform(ks[3], (DEPTH, FOX_HEADS), f32, 1.0, 5.0),
        'cmp_pe_k': nrm(ks[4], (DEPTH, CMP_BLOCK, HEAD_DIM), 0.1),
        'cmp_w1_k': nrm(ks[5], (DEPTH, CMP_BLOCK * HEAD_DIM, CMP_HIDDEN), (CMP_BLOCK * HEAD_DIM) ** -0.5),
        'cmp_w2_k': nrm(ks[6], (DEPTH, CMP_HIDDEN, HEAD_DIM), CMP_HIDDEN ** -0.5),
        'cmp_pe_v': nrm(ks[7], (DEPTH, CMP_BLOCK, HEAD_DIM), 0.1),
        'cmp_w1_v': nrm(ks[8], (DEPTH, CMP_BLOCK * HEAD_DIM, CMP_HIDDEN), (CMP_BLOCK * HEAD_DIM) ** -0.5),
        'cmp_w2_v': nrm(ks[9], (DEPTH, CMP_HIDDEN, HEAD_DIM), CMP_HIDDEN ** -0.5),
        'w_proj_nsa': nrm(ks[10], (DEPTH, NSA_WIDTH, D_MODEL), NSA_WIDTH ** -0.5),
        'w_proj_fox': nrm(ks[11], (DEPTH, FOX_WIDTH, D_MODEL), FOX_WIDTH ** -0.5),
        'w_out': nrm(ks[12], (DEPTH, D_MODEL, D_MODEL), D_MODEL ** -0.5),
        'final_g': 1.0 + nrm(ks[13], (D_MODEL,), 0.01),
    }


def reference(x, norm_g, w_in, b_forget, cmp_pe_k, cmp_w1_k, cmp_w2_k, cmp_pe_v, cmp_w1_v, cmp_w2_v,
              w_proj_nsa, w_proj_fox, w_out, final_g):
    T = x.shape[1]
    cos, sin = rope_tables(T, x.dtype)
    split_at = np.cumsum(SPLITS)[:-1].tolist()
    for l in range(DEPTH):
        h = rms_norm(x, norm_g[l])
        proj = jnp.einsum('btd,dc->btc', h, w_in[l])
        q_a, kv_a, g_a, z_a, q_b, k_b, v_b, f_b, z_b, merge = jnp.split(proj, split_at, axis=-1)
        y_a = nsa_mixer(q_a, kv_a, g_a, cmp_pe_k[l], cmp_w1_k[l], cmp_w2_k[l],
                        cmp_pe_v[l], cmp_w1_v[l], cmp_w2_v[l], cos, sin) * jax.nn.silu(z_a)
        y_b = fox_mixer(q_b, k_b, v_b, f_b, b_forget[l]) * jax.nn.silu(z_b)
        r_a, r_b = jnp.split(merge, 2, axis=-1)
        merged = (jax.nn.sigmoid(r_a) * jnp.einsum('btc,cd->btd', y_a, w_proj_nsa[l])
                  + jax.nn.sigmoid(r_b) * jnp.einsum('btc,cd->btd', y_b, w_proj_fox[l]))
        x = x + jnp.einsum('btd,de->bte', merged, w_out[l])
    return rms_norm(x, final_g)
```

```python
import functools
import math

import numpy as np
import jax
import jax.numpy as jnp
from jax import lax
from jax.experimental import pallas as pl
from jax.experimental.pallas import tpu as pltpu

F32 = jnp.float32
BF16 = jnp.bfloat16

D_MODEL = 1024
HEAD_DIM = 64
LANES = 128
NSA_HEADS = 16
NSA_GROUPS = 2
NSA_HPG = NSA_HEADS // NSA_GROUPS
CMP_BLOCK = 32
CMP_STRIDE = 16
CMP_HIDDEN = 2 * HEAD_DIM
SEL_BLOCK = 64
SEL_TOPK = 8
WINDOW = 512
FOX_HEADS = 16
ROPE_THETA = 10000.0
RMS_EPS = 1e-6
NEG_INF = -1e30
FORCED_SCORE = 1e4
SCALE = HEAD_DIM ** -0.5

COL_QA, COL_ZA, COL_QB, COL_KB, COL_VB, COL_ZB, COL_RA, COL_RB, COL_KV = (
    0, 1024, 2048, 3072, 4096, 5120, 6144, 7168, 8192)
MAIN_COLS = 8960
SMALL_COLS = 256

VMEM_LIMIT = 56 * 1024 * 1024


def _cparams(sem):
    return pltpu.CompilerParams(dimension_semantics=sem, vmem_limit_bytes=VMEM_LIMIT)


def _lane_iota(shape):
    return lax.broadcasted_iota(jnp.int32, shape, len(shape) - 1)


def _dot_nt(a, b):
    return lax.dot_general(a, b, (((1,), (1,)), ((), ())), preferred_element_type=F32)


def _dot(a, b):
    return jnp.dot(a, b, preferred_element_type=F32)


def _split3_dot(a, m):
    hi = a.astype(BF16)
    r1 = a - hi.astype(F32)
    mid = r1.astype(BF16)
    lo = (r1 - mid.astype(F32)).astype(BF16)
    return _dot(hi, m) + _dot(mid, m) + _dot(lo, m)


def _sigmoid(x):
    return 1.0 / (1.0 + jnp.exp(-x))


def _column(x, idx):
    return jnp.sum(jnp.where(_lane_iota(x.shape) == idx, x, 0.0), axis=-1, keepdims=True)


def _rope_slab(x, cos, sin):
    lane = _lane_iota(x.shape)
    swapped = jnp.where((lane % HEAD_DIM) < HEAD_DIM // 2,
                        pltpu.roll(x, LANES - HEAD_DIM // 2, 1), pltpu.roll(x, HEAD_DIM // 2, 1))
    return x * cos + swapped * sin


def _stack_heads(slabs):
    lane = _lane_iota(slabs[0].shape)
    parts = []
    for s in slabs:
        parts.append(jnp.where(lane < HEAD_DIM, s, jnp.zeros_like(s)))
        parts.append(jnp.where(lane >= HEAD_DIM, s, jnp.zeros_like(s)))
    return jnp.concatenate(parts, axis=0)


def _unstack_heads(o, nslab, rows):
    lane = _lane_iota((rows, LANES))
    out = []
    for j in range(nslab):
        a = o[(2 * j) * rows:(2 * j + 1) * rows]
        b = o[(2 * j + 1) * rows:(2 * j + 2) * rows]
        out.append(jnp.where(lane < HEAD_DIM, a, b))
    return out


def _inproj_kernel(x_ref, g_ref, w_ref, ws_ref, main_ref, small_ref, xn_ref):
    @pl.when(pl.program_id(1) == 0)
    def _():
        x = x_ref[...]
        y = x * lax.rsqrt(jnp.mean(x * x, axis=-1, keepdims=True) + RMS_EPS) * g_ref[...]
        xn = y.astype(BF16)
        xn_ref[...] = xn
        small_ref[...] = _dot(xn, ws_ref[...])

    main_ref[...] = _dot(xn_ref[...], w_ref[...]).astype(BF16)


def _inproj(x2, g, w_main, w_small, tm=1024, tn=1280):
    n = x2.shape[0]
    tm = min(tm, n)
    return pl.pallas_call(
        _inproj_kernel,
        grid=(n // tm, MAIN_COLS // tn),
        in_specs=[
            pl.BlockSpec((tm, D_MODEL), lambda i, j: (i, 0)),
            pl.BlockSpec((1, D_MODEL), lambda i, j: (0, 0)),
            pl.BlockSpec((D_MODEL, tn), lambda i, j: (0, j)),
            pl.BlockSpec((D_MODEL, SMALL_COLS), lambda i, j: (0, 0)),
        ],
        out_specs=[
            pl.BlockSpec((tm, tn), lambda i, j: (i, j)),
            pl.BlockSpec((tm, SMALL_COLS), lambda i, j: (i, 0)),
        ],
        out_shape=[jax.ShapeDtypeStruct((n, MAIN_COLS), BF16),
                   jax.ShapeDtypeStruct((n, SMALL_COLS), F32)],
        scratch_shapes=[pltpu.VMEM((tm, D_MODEL), BF16)],
        compiler_params=_cparams(("parallel", "arbitrary")),
        name="inproj",
    )(x2, g, w_main, w_small)


def _compress_kernel(xk_ref, xv_ref, pek_ref, pev_ref, w1k_ref, w1v_ref,
                     wak_ref, wbk_ref, wav_ref, wbv_ref, w2k_ref, w2v_ref, kc_ref, vc_ref):
    def one(x_ref, pe_ref, w1_ref, wa_ref, wb_ref, w2_ref, out_ref):
        x = x_ref[...]
        a = _dot(x, wa_ref[...])
        b = _dot(x, wb_ref[...])
        nc = a.shape[0]
        b_up = pltpu.roll(b, nc - 1, 0)
        pe_c = _dot(pe_ref[...].astype(BF16), w1_ref[...])[0:1]
        pe_c = jnp.concatenate([pe_c, pe_c], axis=1)
        hid = a + b_up + pe_c
        hid = (hid * _sigmoid(hid)).astype(BF16)
        for g in range(NSA_GROUPS):
            out_ref[g] = _dot(hid, w2_ref[g]).astype(BF16)

    one(xk_ref, pek_ref, w1k_ref, wak_ref, wbk_ref, w2k_ref, kc_ref)
    one(xv_ref, pev_ref, w1v_ref, wav_ref, wbv_ref, w2v_ref, vc_ref)


def _compress(xk, xv, pek, pev, w1k, w1v, wak, wbk, wav, wbv, w2k, w2v):
    b, nc, kw = xk.shape
    full = lambda a: pl.BlockSpec(a.shape, lambda i: (0,) * a.ndim)
    xs = pl.BlockSpec((None, nc, kw), lambda i: (i, 0, 0))
    os_ = pl.BlockSpec((None, NSA_GROUPS, nc, LANES), lambda i: (i, 0, 0, 0))
    return pl.pallas_call(
        _compress_kernel,
        grid=(b,),
        in_specs=[xs, xs] + [full(a) for a in (pek, pev, w1k, w1v, wak, wbk, wav, wbv, w2k, w2v)],
        out_specs=[os_, os_],
        out_shape=[jax.ShapeDtypeStruct((b, NSA_GROUPS, nc, LANES), BF16)] * 2,
        compiler_params=_cparams(("parallel",)),
        name="compress",
    )(xk, xv, pek, pev, w1k, w1v, wak, wbk, wav, wbv, w2k, w2v)


def _kvprep_kernel(s_ref, w_ref, cos_ref, sin_ref, ks_ref, vs_ref, kw_ref, vw_ref):
    cos = cos_ref[...]
    sin = sin_ref[...]

    def dup(x, out_ref):
        lane = _lane_iota(x.shape)
        r = pltpu.roll(x, HEAD_DIM, 1)
        out_ref[0] = jnp.where(lane < HEAD_DIM, x, r).astype(BF16)
        out_ref[1] = jnp.where(lane < HEAD_DIM, r, x).astype(BF16)

    dup(_rope_slab(s_ref[:, 0:LANES].astype(F32), cos, sin), ks_ref)
    dup(s_ref[:, LANES:2 * LANES].astype(F32), vs_ref)
    dup(_rope_slab(w_ref[:, 0:LANES].astype(F32), cos, sin), kw_ref)
    dup(w_ref[:, LANES:2 * LANES].astype(F32), vw_ref)


def _kvprep(main3, cos, sin, tt=512):
    b, t, _ = main3.shape
    tt = min(tt, t)
    blk = COL_KV // (2 * LANES)
    o = pl.BlockSpec((None, NSA_GROUPS, tt, LANES), lambda i, j: (i, 0, j, 0))
    return pl.pallas_call(
        _kvprep_kernel,
        grid=(b, t // tt),
        in_specs=[
            pl.BlockSpec((None, tt, 2 * LANES), lambda i, j: (i, j, blk + 1)),
            pl.BlockSpec((None, tt, 2 * LANES), lambda i, j: (i, j, blk + 2)),
            pl.BlockSpec((tt, LANES), lambda i, j: (j, 0)),
            pl.BlockSpec((tt, LANES), lambda i, j: (j, 0)),
        ],
        out_specs=[o, o, o, o],
        out_shape=[jax.ShapeDtypeStruct((b, NSA_GROUPS, t, LANES), BF16)] * 4,
        compiler_params=_cparams(("parallel", "parallel")),
        name="kvprep",
    )(main3, main3, cos, sin)


def _cmp_kernel(q_ref, kc_ref, vc_ref, ovl_ref, sm_ref, o_ref, sel_ref, *, tq):
    q0 = pl.program_id(1) * tq
    nc = kc_ref.shape[1]
    lane = _lane_iota((tq, LANES))
    t = q0 + lax.broadcasted_iota(jnp.int32, (tq, LANES), 0)
    gates = _sigmoid(sm_ref[:, 0:LANES])
    cmp_valid = (lane * CMP_STRIDE + CMP_BLOCK - 1 <= t) & (lane < nc)
    cur = t // SEL_BLOCK
    forced = (lane == 0) | (lane == cur) | (lane == cur - 1)
    blk_valid = lane * SEL_BLOCK <= t
    nsel = 32
    row32 = lax.broadcasted_iota(jnp.int32, (nsel, tq), 0)
    for g in range(NSA_GROUPS):
        slabs = [(q_ref[:, (g * 4 + j) * LANES:(g * 4 + j + 1) * LANES].astype(F32) * SCALE).astype(BF16)
                 for j in range(4)]
        qs = _stack_heads(slabs)
        s3 = _dot_nt(qs, kc_ref[g]).reshape(NSA_HPG, tq, nc)
        s3 = jnp.where(cmp_valid[None], s3, NEG_INF)
        m = jnp.max(s3, axis=-1, keepdims=True)
        e = jnp.where(cmp_valid[None], jnp.exp(s3 - m), 0.0)
        l = jnp.sum(e, axis=-1, keepdims=True)
        p = (e / jnp.where(l > 0.0, l, 1.0)).astype(BF16)
        o3 = _dot(p.reshape(NSA_HPG * tq, nc), vc_ref[g]).reshape(NSA_HPG, tq, LANES)
        o3 = jnp.stack([o3[h] * _column(gates, g * NSA_HPG + h) for h in range(NSA_HPG)])
        for j, slab in enumerate(_unstack_heads(o3.reshape(NSA_HPG * tq, LANES), 4, tq)):
            o_ref[:, (g * 4 + j) * LANES:(g * 4 + j + 1) * LANES] = slab
        imp = _split3_dot(jnp.sum(p.astype(F32), axis=0), ovl_ref[...])
        score = jnp.where(forced, FORCED_SCORE, jnp.where(blk_valid, imp, -1.0))
        score = jnp.where(lane < nsel, score, -2.0)
        st = score.T[0:nsel, :]
        cnt = jnp.zeros((nsel, tq), F32)
        for j in range(nsel):
            rj = st[j:j + 1, :]
            beats = (rj > st) | ((rj == st) & (row32 > j))
            cnt = cnt + jnp.where(beats, 1.0, 0.0)
        sel_t = jnp.where(cnt < float(SEL_TOPK), 1.0, 0.0)
        sel_t = jnp.concatenate([sel_t, jnp.zeros((LANES - nsel, tq), F32)], axis=0)
        sel_ref[g] = sel_t.T.astype(BF16)


def _cmp_attn(main3, small3, kc, vc, ovl, tq=128):
    b, t, _ = main3.shape
    nc = kc.shape[2]
    return pl.pallas_call(
        functools.partial(_cmp_kernel, tq=tq),
        grid=(b, t // tq),
        in_specs=[
            pl.BlockSpec((None, tq, 1024), lambda i, j: (i, j, COL_QA // 1024)),
            pl.BlockSpec((None, NSA_GROUPS, nc, LANES), lambda i, j: (i, 0, 0, 0)),
            pl.BlockSpec((None, NSA_GROUPS, nc, LANES), lambda i, j: (i, 0, 0, 0)),
            pl.BlockSpec(ovl.shape, lambda i, j: (0, 0)),
            pl.BlockSpec((None, tq, SMALL_COLS), lambda i, j: (i, j, 0)),
        ],
        out_specs=[
            pl.BlockSpec((None, tq, 1024), lambda i, j: (i, j, 0)),
            pl.BlockSpec((None, NSA_GROUPS, tq, LANES), lambda i, j: (i, 0, j, 0)),
        ],
        out_shape=[jax.ShapeDtypeStruct((b, t, 1024), F32),
                   jax.ShapeDtypeStruct((b, NSA_GROUPS, t, LANES), BF16)],
        compiler_params=_cparams(("parallel", "parallel")),
        name="cmp_attn",
    )(main3, kc, vc, ovl, small3)


def _flash_sweep(qs, nh, tq, tk, k_ref, v_ref, c_lo, c_hi, adjust, m_ref, l_ref, acc_ref):
    m_ref[...] = jnp.full(m_ref.shape, NEG_INF, F32)
    l_ref[...] = jnp.zeros(l_ref.shape, F32)
    acc_ref[...] = jnp.zeros(acc_ref.shape, F32)

    def body(c, carry):
        k0 = pl.multiple_of(c * tk, tk)
        k = k_ref[pl.ds(k0, tk), :]
        v = v_ref[pl.ds(k0, tk), :]
        s3 = adjust(c, _dot_nt(qs, k).reshape(nh, tq, tk))
        m_prev = m_ref[...].reshape(nh, tq, 1)
        m_new = jnp.maximum(m_prev, jnp.max(s3, axis=-1, keepdims=True))
        alpha = jnp.exp(m_prev - m_new)
        p = jnp.exp(s3 - m_new)
        l_ref[...] = alpha.reshape(nh * tq, 1) * l_ref[...] + jnp.sum(p, axis=-1, keepdims=True).reshape(nh * tq, 1)
        pv = _dot(p.reshape(nh * tq, tk).astype(BF16), v)
        acc_ref[...] = alpha.reshape(nh * tq, 1) * acc_ref[...] + pv
        m_ref[...] = m_new.reshape(nh * tq, 1)
        return carry

    lax.fori_loop(c_lo, c_hi, body, 0)
    return acc_ref[...] / l_ref[...]


def _flash_scratch(rows):
    return [pltpu.VMEM((rows, 1), F32), pltpu.VMEM((rows, 1), F32), pltpu.VMEM((rows, LANES), F32)]


def _roped_group_queries(q_ref, cos_ref, sin_ref, q0, tq):
    cos = cos_ref[pl.ds(q0, tq), :]
    sin = sin_ref[pl.ds(q0, tq), :]
    slabs = []
    for j in range(4):
        x = q_ref[pl.ds(q0, tq), j * LANES:(j + 1) * LANES].astype(F32)
        slabs.append((_rope_slab(x, cos, sin) * SCALE).astype(BF16))
    return _stack_heads(slabs)


def _slc_kernel(q_ref, cos_ref, sin_ref, k_ref, v_ref, sel_ref, sm_ref, prev_ref, o_ref,
                m_ref, l_ref, acc_ref, *, tq, tk):
    g = pl.program_id(1)
    t_total = q_ref.shape[0]

    def qtile(i, carry):
        q0 = pl.multiple_of(i * tq, tq)
        qs = _roped_group_queries(q_ref, cos_ref, sin_ref, q0, tq)
        sel = sel_ref[pl.ds(q0, tq), :]
        t = q0 + lax.broadcasted_iota(jnp.int32, (tq, tk), 0)

        def adjust(c, s3):
            key = c * tk + lax.broadcasted_iota(jnp.int32, (tq, tk), 1)
            blk_of_key = c * tk // SEL_BLOCK + lax.broadcasted_iota(jnp.int32, (LANES, tk), 1) // SEL_BLOCK
            expand = jnp.where(lax.broadcasted_iota(jnp.int32, (LANES, tk), 0) == blk_of_key, 1.0, 0.0).astype(BF16)
            chosen = _dot(sel, expand)
            ok = (chosen > 0.5) & (key <= t)
            return jnp.where(ok[None], s3, NEG_INF)

        c_hi = (q0 + tq + tk - 1) // tk
        o = _flash_sweep(qs, NSA_HPG, tq, tk, k_ref, v_ref, 0, c_hi, adjust, m_ref, l_ref, acc_ref)
        gates = _sigmoid(sm_ref[pl.ds(q0, tq), 0:LANES])
        o3 = o.reshape(NSA_HPG, tq, LANES)
        o3 = jnp.stack([o3[h] * _column(gates, NSA_HEADS + g * NSA_HPG + h) for h in range(NSA_HPG)])
        for j, slab in enumerate(_unstack_heads(o3.reshape(NSA_HPG * tq, LANES), 4, tq)):
            o_ref[pl.ds(q0, tq), j * LANES:(j + 1) * LANES] = prev_ref[pl.ds(q0, tq), j * LANES:(j + 1) * LANES] + slab
        return carry

    lax.fori_loop(0, t_total // tq, qtile, 0)


def _slc_attn(main3, small3, cos, sin, ks, vs, sel, prev, tq=128, tk=256):
    b, t, _ = main3.shape
    tk = min(tk, t)
    gw = NSA_HPG * HEAD_DIM
    kv = pl.BlockSpec((None, None, t, LANES), lambda i, g: (i, g, 0, 0))
    tab = pl.BlockSpec((t, LANES), lambda i, g: (0, 0))
    return pl.pallas_call(
        functools.partial(_slc_kernel, tq=tq, tk=tk),
        grid=(b, NSA_GROUPS),
        in_specs=[
            pl.BlockSpec((None, t, gw), lambda i, g: (i, 0, g)),
            tab, tab, kv, kv, kv,
            pl.BlockSpec((None, t, SMALL_COLS), lambda i, g: (i, 0, 0)),
            pl.BlockSpec((None, t, gw), lambda i, g: (i, 0, g)),
        ],
        out_specs=pl.BlockSpec((None, t, gw), lambda i, g: (i, 0, g)),
        out_shape=jax.ShapeDtypeStruct((b, t, 1024), F32),
        scratch_shapes=_flash_scratch(NSA_HPG * tq),
        compiler_params=_cparams(("parallel", "parallel")),
        name="slc_attn",
    )(main3, cos, sin, ks, vs, sel, small3, prev)


def _win_kernel(q_ref, cos_ref, sin_ref, k_ref, v_ref, sm_ref, prev_ref, z_ref, y_ref,
                m_ref, l_ref, acc_ref, *, tq, tk):
    g = pl.program_id(1)
    t_total = q_ref.shape[0]

    def qtile(i, carry):
        q0 = pl.multiple_of(i * tq, tq)
        qs = _roped_group_queries(q_ref, cos_ref, sin_ref, q0, tq)
        t = q0 + lax.broadcasted_iota(jnp.int32, (tq, tk), 0)

        def adjust(c, s3):
            key = c * tk + lax.broadcasted_iota(jnp.int32, (tq, tk), 1)
            ok = (key <= t) & (key > t - WINDOW)
            return jnp.where(ok[None], s3, NEG_INF)

        c_lo = jnp.maximum(q0 - WINDOW + 1, 0) // tk
        c_hi = (q0 + tq + tk - 1) // tk
        o = _flash_sweep(qs, NSA_HPG, tq, tk, k_ref, v_ref, c_lo, c_hi, adjust, m_ref, l_ref, acc_ref)
        gates = _sigmoid(sm_ref[pl.ds(q0, tq), 0:LANES])
        o3 = o.reshape(NSA_HPG, tq, LANES)
        o3 = jnp.stack([o3[h] * _column(gates, 2 * NSA_HEADS + g * NSA_HPG + h) for h in range(NSA_HPG)])
        for j, slab in enumerate(_unstack_heads(o3.reshape(NSA_HPG * tq, LANES), 4, tq)):
            cols = slice(j * LANES, (j + 1) * LANES)
            z = z_ref[pl.ds(q0, tq), cols].astype(F32)
            y = (prev_ref[pl.ds(q0, tq), cols] + slab) * (z * _sigmoid(z))
            y_ref[pl.ds(q0, tq), cols] = y.astype(BF16)
        return carry

    lax.fori_loop(0, t_total // tq, qtile, 0)


def _win_attn(main3, small3, cos, sin, kw, vw, prev, tq=128, tk=128):
    b, t, _ = main3.shape
    gw = NSA_HPG * HEAD_DIM
    kv = pl.BlockSpec((None, None, t, LANES), lambda i, g: (i, g, 0, 0))
    tab = pl.BlockSpec((t, LANES), lambda i, g: (0, 0))
    return pl.pallas_call(
        functools.partial(_win_kernel, tq=tq, tk=tk),
        grid=(b, NSA_GROUPS),
        in_specs=[
            pl.BlockSpec((None, t, gw), lambda i, g: (i, 0, g)),
            tab, tab, kv, kv,
            pl.BlockSpec((None, t, SMALL_COLS), lambda i, g: (i, 0, 0)),
            pl.BlockSpec((None, t, gw), lambda i, g: (i, 0, g)),
            pl.BlockSpec((None, t, gw), lambda i, g: (i, 0, COL_ZA // gw + g)),
        ],
        out_specs=pl.BlockSpec((None, t, gw), lambda i, g: (i, 0, g)),
        out_shape=jax.ShapeDtypeStruct((b, t, 1024), BF16),
        scratch_shapes=_flash_scratch(NSA_HPG * tq),
        compiler_params=_cparams(("parallel", "parallel")),
        name="win_attn",
    )(main3, cos, sin, kw, vw, small3, prev, main3)


def _decay_kernel(sm_ref, b_ref, ft_ref, *, tk):
    t_total = sm_ref.shape[0]
    x = sm_ref[:, LANES:2 * LANES] + b_ref[...]
    f = jnp.minimum(x, 0.0) - jnp.log1p(jnp.exp(-jnp.abs(x)))
    row = lax.broadcasted_iota(jnp.int32, f.shape, 0)
    sh = 1
    while sh < t_total:
        f = f + jnp.where(row >= sh, pltpu.roll(f, sh, 0), 0.0)
        sh *= 2
    for c in range(t_total // tk):
        blk = f[c * tk:(c + 1) * tk, :].T
        for p in range(FOX_HEADS // 2):
            ft_ref[p, c] = blk[2 * p:2 * p + 2, :]


def _decay(small3, b_pad, tk):
    b, t, _ = small3.shape
    return pl.pallas_call(
        functools.partial(_decay_kernel, tk=tk),
        grid=(b,),
        in_specs=[pl.BlockSpec((None, t, SMALL_COLS), lambda i: (i, 0, 0)),
                  pl.BlockSpec((1, LANES), lambda i: (0, 0))],
        out_specs=pl.BlockSpec((None, FOX_HEADS // 2, t // tk, 2, tk), lambda i: (i, 0, 0, 0, 0)),
        out_shape=jax.ShapeDtypeStruct((b, FOX_HEADS // 2, t // tk, 2, tk), F32),
        compiler_params=_cparams(("parallel",)),
        name="fox_decay",
    )(small3, b_pad)


def _fox_kernel(q_ref, k_ref, v_ref, ft_ref, z_ref, y_ref, m_ref, l_ref, acc_ref, *, tq, tk):
    t_total = q_ref.shape[0]

    def qtile(i, carry):
        q0 = pl.multiple_of(i * tq, tq)
        qs = _stack_heads([(q_ref[pl.ds(q0, tq), :].astype(F32) * SCALE).astype(BF16)])
        t = q0 + lax.broadcasted_iota(jnp.int32, (tq, tk), 0)

        def adjust(c, s3):
            key = c * tk + lax.broadcasted_iota(jnp.int32, (tq, tk), 1)
            s3 = s3 - ft_ref[c][:, None, :]
            return jnp.where((key <= t)[None], s3, NEG_INF)

        c_hi = (q0 + tq + tk - 1) // tk
        o = _flash_sweep(qs, 2, tq, tk, k_ref, v_ref, 0, c_hi, adjust, m_ref, l_ref, acc_ref)
        slab = _unstack_heads(o, 1, tq)[0]
        z = z_ref[pl.ds(q0, tq), :].astype(F32)
        y_ref[pl.ds(q0, tq), :] = (slab * (z * _sigmoid(z))).astype(BF16)
        return carry

    lax.fori_loop(0, t_total // tq, qtile, 0)


def _fox_attn(main3, ft, tq=256, tk=256):
    b, t, _ = main3.shape
    tq = min(tq, t)
    nslab = FOX_HEADS // 2

    def col(base):
        return pl.BlockSpec((None, t, LANES), lambda i, p: (i, 0, base // LANES + p))

    return pl.pallas_call(
        functools.partial(_fox_kernel, tq=tq, tk=tk),
        grid=(b, nslab),
        in_specs=[col(COL_QB), col(COL_KB), col(COL_VB),
                  pl.BlockSpec((None, None, t // tk, 2, tk), lambda i, p: (i, p, 0, 0, 0)),
                  col(COL_ZB)],
        out_specs=pl.BlockSpec((None, t, LANES), lambda i, p: (i, 0, p)),
        out_shape=jax.ShapeDtypeStruct((b, t, 1024), BF16),
        scratch_shapes=_flash_scratch(2 * tq),
        compiler_params=_cparams(("parallel", "parallel")),
        name="fox_attn",
    )(main3, main3, main3, ft, main3)


def _out_kernel(x_ref, ya_ref, yb_ref, ra_ref, rb_ref, wn_ref, wf_ref, wo_ref, fg_ref, o_ref, *, final):
    ta = _dot(ya_ref[...], wn_ref[...])
    tb = _dot(yb_ref[...], wf_ref[...])
    merged = _sigmoid(ra_ref[...].astype(F32)) * ta + _sigmoid(rb_ref[...].astype(F32)) * tb
    out = x_ref[...] + _dot(merged.astype(BF16), wo_ref[...])
    if final:
        out = out * lax.rsqrt(jnp.mean(out * out, axis=-1, keepdims=True) + RMS_EPS) * fg_ref[...]
    o_ref[...] = out


def _out_proj(x2, ya2, yb2, main2, wn, wf, wo, fg, final, tm=512):
    n = x2.shape[0]
    tm = min(tm, n)
    row = lambda c: pl.BlockSpec((tm, 1024), lambda i: (i, c))
    w = pl.BlockSpec((1024, 1024), lambda i: (0, 0))
    return pl.pallas_call(
        functools.partial(_out_kernel, final=final),
        grid=(n // tm,),
        in_specs=[row(0), row(0), row(0), row(COL_RA // 1024), row(COL_RB // 1024), w, w, w,
                  pl.BlockSpec((1, 1024), lambda i: (0, 0))],
        out_specs=row(0),
        out_shape=jax.ShapeDtypeStruct((n, 1024), F32),
        compiler_params=_cparams(("parallel",)),
        name="out_proj",
    )(x2, ya2, yb2, main2, main2, wn, wf, wo, fg)


def _rope_tables(t):
    inv = ROPE_THETA ** (-jnp.arange(0, HEAD_DIM, 2, dtype=F32) / HEAD_DIM)
    ang = jnp.arange(t, dtype=F32)[:, None] * inv[None, :]
    c, s = jnp.cos(ang), jnp.sin(ang)
    return jnp.concatenate([c, c, c, c], axis=1), jnp.concatenate([-s, s, -s, s], axis=1)


def _overlap_matrix():
    n = np.arange(LANES)[:, None] * CMP_STRIDE
    j = np.arange(LANES)[None, :] * SEL_BLOCK
    m = (n < j + SEL_BLOCK) & (n + CMP_BLOCK > j) & (np.arange(LANES)[None, :] < 32)
    return jnp.asarray(m.astype(np.float32), dtype=BF16)


def _reorder_w_in(w):
    qa, kv, ga, za, qb, kb, vb, fb, zb, mg = jnp.split(
        w, np.cumsum([1024, 768, 48, 1024, 1024, 1024, 1024, 16, 1024])[:9].tolist(), axis=1)
    main = jnp.concatenate([qa, za, qb, kb, vb, zb, mg, kv], axis=1).astype(BF16)
    d = w.shape[0]
    small = jnp.concatenate([ga, jnp.zeros((d, LANES - 48), F32), fb, jnp.zeros((d, LANES - 16), F32)],
                            axis=1).astype(BF16)
    return main, small


def _compress_weights(w1, w2):
    eye = jnp.eye(NSA_GROUPS, dtype=F32)
    w1r = w1.reshape(CMP_BLOCK, HEAD_DIM, CMP_HIDDEN)
    half = CMP_BLOCK // 2

    def blockdiag(wl):
        return jnp.einsum('ldh,gk->lgdkh', wl, eye).reshape(half * NSA_GROUPS * HEAD_DIM,
                                                            NSA_GROUPS * CMP_HIDDEN).astype(BF16)

    wa, wb = blockdiag(w1r[:half]), blockdiag(w1r[half:])
    w2d = jnp.einsum('hd,gk,c->gkhcd', w2, eye, jnp.ones((2,), F32)).reshape(
        NSA_GROUPS, NSA_GROUPS * CMP_HIDDEN, LANES).astype(BF16)
    return wa, wb, w2d


def kernel(x, norm_g, w_in, b_forget, cmp_pe_k, cmp_w1_k, cmp_w2_k, cmp_pe_v, cmp_w1_v, cmp_w2_v,
           w_proj_nsa, w_proj_fox, w_out, final_g):
    b, t, d = x.shape
    depth = norm_g.shape[0]
    n = b * t
    nc = t // CMP_STRIDE
    fox_tk = 256
    cos, sin = _rope_tables(t)
    ovl = _overlap_matrix()
    fg = final_g.reshape(1, d)
    x2 = x.reshape(n, d)
    for l in range(depth):
        w_main, w_small = _reorder_w_in(w_in[l])
        main2, small2 = _inproj(x2, norm_g[l].reshape(1, d), w_main, w_small)
        main3 = main2.reshape(b, t, MAIN_COLS)
        small3 = small2.reshape(b, t, SMALL_COLS)
        xk = main3[:, :, COL_KV:COL_KV + LANES].reshape(b, nc, CMP_STRIDE * LANES)
        xv = main3[:, :, COL_KV + LANES:COL_KV + 2 * LANES].reshape(b, nc, CMP_STRIDE * LANES)
        wak, wbk, w2k = _compress_weights(cmp_w1_k[l], cmp_w2_k[l])
        wav, wbv, w2v = _compress_weights(cmp_w1_v[l], cmp_w2_v[l])
        pek = jnp.broadcast_to(cmp_pe_k[l].reshape(1, -1), (8, CMP_BLOCK * HEAD_DIM))
        pev = jnp.broadcast_to(cmp_pe_v[l].reshape(1, -1), (8, CMP_BLOCK * HEAD_DIM))
        kc, vc = _compress(xk, xv, pek, pev, cmp_w1_k[l].astype(BF16), cmp_w1_v[l].astype(BF16),
                           wak, wbk, wav, wbv, w2k, w2v)
        ks, vs, kw, vw = _kvprep(main3, cos, sin)
        o_cmp, sel = _cmp_attn(main3, small3, kc, vc, ovl)
        o_cs = _slc_attn(main3, small3, cos, sin, ks, vs, sel, o_cmp)
        y_a = _win_attn(main3, small3, cos, sin, kw, vw, o_cs)
        b_pad = jnp.concatenate([b_forget[l], jnp.zeros((LANES - FOX_HEADS,), F32)]).reshape(1, LANES)
        ft = _decay(small3, b_pad, fox_tk)
        y_b = _fox_attn(main3, ft, tk=fox_tk)
        x2 = _out_proj(x2, y_a.reshape(n, 1024), y_b.reshape(n, 1024), main2,
                       w_proj_nsa[l].astype(BF16), w_proj_fox[l].astype(BF16), w_out[l].astype(BF16),
                       fg, final=(l == depth - 1))
    return x2.reshape(b, t, d)
```

```python
import functools

import numpy as np
import jax
import jax.numpy as jnp
from jax import lax
from jax.experimental import pallas as pl
from jax.experimental.pallas import tpu as pltpu

F32 = jnp.float32
BF16 = jnp.bfloat16

D_MODEL = 1024
HEAD_DIM = 64
LANES = 128
SUBLANES = 8
NSA_HEADS = 16
NSA_GROUPS = 2
NSA_HPG = NSA_HEADS // NSA_GROUPS
CMP_BLOCK = 32
CMP_STRIDE = 16
CMP_HIDDEN = 2 * HEAD_DIM
SEL_BLOCK = 64
SEL_TOPK = 8
N_SEL = 32
WINDOW = 512
FOX_HEADS = 16
ROPE_THETA = 10000.0
RMS_EPS = 1e-6
NEG_INF = -1e30
FORCED_SCORE = 1e4
SCALE = HEAD_DIM ** -0.5
LOG2E = 1.4426950408889634
EXT = HEAD_DIM

COL_QA, COL_ZA, COL_QB, COL_KB, COL_VB, COL_ZB, COL_RA, COL_RB, COL_KV = (
    0, 1024, 2048, 3072, 4096, 5120, 6144, 7168, 8192)
MAIN_COLS = 8960
SMALL_COLS = 256

TQ_NSA = 128
TK_SLC = 256
TK_WIN = 128
TQ_FOX = 256
VMEM_LIMIT = 56 * 1024 * 1024


def _cparams(sem):
    return pltpu.CompilerParams(dimension_semantics=sem, vmem_limit_bytes=VMEM_LIMIT)


def _lane_iota(shape):
    return lax.broadcasted_iota(jnp.int32, shape, len(shape) - 1)


def _dot_nt(a, b):
    return lax.dot_general(a, b, (((1,), (1,)), ((), ())), preferred_element_type=F32)


def _dot(a, b):
    return jnp.dot(a, b, preferred_element_type=F32)


def _split3(x):
    hi = x.astype(BF16).astype(F32)
    r = x - hi
    mid = r.astype(BF16).astype(F32)
    lo = (r - mid).astype(BF16).astype(F32)
    return hi, mid, lo


def _split3_dot(a, m):
    hi, mid, lo = _split3(a)
    return _dot(hi.astype(BF16), m) + _dot(mid.astype(BF16), m) + _dot(lo.astype(BF16), m)


def _sigmoid(x):
    return 1.0 / (1.0 + jnp.exp(-x))


def _column(x, idx):
    return jnp.sum(jnp.where(_lane_iota(x.shape) == idx, x, 0.0), axis=-1, keepdims=True)


def _rope_slab(x, cos, sin):
    lane = _lane_iota(x.shape)
    swapped = jnp.where((lane % HEAD_DIM) < HEAD_DIM // 2,
                        pltpu.roll(x, LANES - HEAD_DIM // 2, 1), pltpu.roll(x, HEAD_DIM // 2, 1))
    return x * cos + swapped * sin


def _stack_heads(slabs):
    lane = _lane_iota(slabs[0].shape)
    parts = []
    for s in slabs:
        parts.append(jnp.where(lane < HEAD_DIM, s, jnp.zeros_like(s)))
        parts.append(jnp.where(lane >= HEAD_DIM, s, jnp.zeros_like(s)))
    return jnp.concatenate(parts, axis=0)


def _unstack_heads(o, nslab, rows):
    lane = _lane_iota((rows, LANES))
    out = []
    for j in range(nslab):
        a = o[(2 * j) * rows:(2 * j + 1) * rows]
        b = o[(2 * j + 1) * rows:(2 * j + 2) * rows]
        out.append(jnp.where(lane < HEAD_DIM, a, b))
    return out


def _fold_rows(x):
    return x.reshape(x.shape[0] // SUBLANES, SUBLANES, x.shape[1])


def _inproj_kernel(x_ref, g_ref, w_ref, ws_ref, main_ref, small_ref, xn_ref):
    @pl.when(pl.program_id(1) == 0)
    def _():
        x = x_ref[...]
        y = x * lax.rsqrt(jnp.mean(x * x, axis=-1, keepdims=True) + RMS_EPS) * g_ref[...]
        xn = y.astype(BF16)
        xn_ref[...] = xn
        small_ref[...] = _dot(xn, ws_ref[...])

    main_ref[...] = _dot(xn_ref[...], w_ref[...]).astype(BF16)


def _inproj(x2, g, w_main, w_small, tm=1024, tn=1280):
    n = x2.shape[0]
    tm = min(tm, n)
    return pl.pallas_call(
        _inproj_kernel,
        grid=(n // tm, MAIN_COLS // tn),
        in_specs=[
            pl.BlockSpec((tm, D_MODEL), lambda i, j: (i, 0)),
            pl.BlockSpec((1, D_MODEL), lambda i, j: (0, 0)),
            pl.BlockSpec((D_MODEL, tn), lambda i, j: (0, j)),
            pl.BlockSpec((D_MODEL, SMALL_COLS), lambda i, j: (0, 0)),
        ],
        out_specs=[
            pl.BlockSpec((tm, tn), lambda i, j: (i, j)),
            pl.BlockSpec((tm, SMALL_COLS), lambda i, j: (i, 0)),
        ],
        out_shape=[jax.ShapeDtypeStruct((n, MAIN_COLS), BF16),
                   jax.ShapeDtypeStruct((n, SMALL_COLS), F32)],
        scratch_shapes=[pltpu.VMEM((tm, D_MODEL), BF16)],
        compiler_params=_cparams(("parallel", "arbitrary")),
        name="inproj",
    )(x2, g, w_main, w_small)


def _compress_kernel(xk_ref, xv_ref, pek_ref, pev_ref, w1k_ref, w1v_ref,
                     wak_ref, wbk_ref, wav_ref, wbv_ref, w2k_ref, w2v_ref, kc_ref, vc_ref):
    def one(x_ref, pe_ref, w1_ref, wa_ref, wb_ref, w2_ref, out_ref):
        x = x_ref[...]
        a = _dot(x, wa_ref[...])
        b = _dot(x, wb_ref[...])
        nc = a.shape[0]
        b_up = pltpu.roll(b, nc - 1, 0)
        pe_c = _dot(pe_ref[...].astype(BF16), w1_ref[...])[0:1]
        pe_c = jnp.concatenate([pe_c, pe_c], axis=1)
        hid = a + b_up + pe_c
        hid = (hid * _sigmoid(hid)).astype(BF16)
        for g in range(NSA_GROUPS):
            out_ref[g] = _dot(hid, w2_ref[g]).astype(BF16)

    one(xk_ref, pek_ref, w1k_ref, wak_ref, wbk_ref, w2k_ref, kc_ref)
    one(xv_ref, pev_ref, w1v_ref, wav_ref, wbv_ref, w2v_ref, vc_ref)


def _compress(xk, xv, pek, pev, w1k, w1v, wak, wbk, wav, wbv, w2k, w2v):
    b, nc, kw = xk.shape
    full = lambda a: pl.BlockSpec(a.shape, lambda i: (0,) * a.ndim)
    xs = pl.BlockSpec((None, nc, kw), lambda i: (i, 0, 0))
    os_ = pl.BlockSpec((None, NSA_GROUPS, nc, LANES), lambda i: (i, 0, 0, 0))
    return pl.pallas_call(
        _compress_kernel,
        grid=(b,),
        in_specs=[xs, xs] + [full(a) for a in (pek, pev, w1k, w1v, wak, wbk, wav, wbv, w2k, w2v)],
        out_specs=[os_, os_],
        out_shape=[jax.ShapeDtypeStruct((b, NSA_GROUPS, nc, LANES), BF16)] * 2,
        compiler_params=_cparams(("parallel",)),
        name="compress",
    )(xk, xv, pek, pev, w1k, w1v, wak, wbk, wav, wbv, w2k, w2v)


def _nsaprep_kernel(q_ref, s_ref, w_ref, sm_ref, cos_ref, sin_ref,
                    qh_ref, ks_ref, kw_ref, vst_ref, vwt_ref, gt_ref, *, tt):
    cos = cos_ref[...]
    sin = sin_ref[...]
    lane = _lane_iota((tt, LANES))
    t = pl.program_id(1) * tt + lax.broadcasted_iota(jnp.int32, (tt, LANES), 0)
    low = lane < HEAD_DIM
    for j in range(NSA_HEADS // 2):
        x = _rope_slab(q_ref[:, j * LANES:(j + 1) * LANES].astype(F32), cos, sin) * (SCALE * LOG2E)
        qh_ref[2 * j] = jnp.where(low, x, 0.0).astype(BF16)
        qh_ref[2 * j + 1] = jnp.where(low, pltpu.roll(x, HEAD_DIM, 1), 0.0).astype(BF16)

    onehot = jnp.where((lane >= EXT) & (lane - EXT == t // SEL_BLOCK) & (lane < EXT + N_SEL), 1.0, 0.0)
    ks = _rope_slab(s_ref[:, 0:LANES].astype(F32), cos, sin)
    kw = _rope_slab(w_ref[:, 0:LANES].astype(F32), cos, sin)
    for g in range(NSA_GROUPS):
        ks_ref[g] = jnp.where(low, ks if g == 0 else pltpu.roll(ks, HEAD_DIM, 1), onehot).astype(BF16)
        kw_ref[g] = jnp.where(low, kw if g == 0 else pltpu.roll(kw, HEAD_DIM, 1), 0.0).astype(BF16)

    vs_t = s_ref[:, LANES:2 * LANES].astype(F32).T
    vw_t = w_ref[:, LANES:2 * LANES].astype(F32).T
    g_t = _sigmoid(sm_ref[:, 0:LANES]).T
    for g in range(NSA_GROUPS):
        rows = slice(g * HEAD_DIM, (g + 1) * HEAD_DIM)
        for c in range(tt // TK_SLC):
            vst_ref[g, c] = vs_t[rows, c * TK_SLC:(c + 1) * TK_SLC].astype(BF16)
        for c in range(tt // TK_WIN):
            vwt_ref[g, c] = vw_t[rows, c * TK_WIN:(c + 1) * TK_WIN].astype(BF16)
        for br in range(3):
            r0 = br * NSA_HEADS + g * NSA_HPG
            for c in range(tt // TQ_NSA):
                gt_ref[br, g, c] = g_t[r0:r0 + NSA_HPG, c * TQ_NSA:(c + 1) * TQ_NSA]


def _nsaprep(main3, small3, cos, sin, tt=256):
    b, t, _ = main3.shape
    blk = COL_KV // (2 * LANES)
    kspec = pl.BlockSpec((None, NSA_GROUPS, tt, LANES), lambda i, j: (i, 0, j, 0))
    return pl.pallas_call(
        functools.partial(_nsaprep_kernel, tt=tt),
        grid=(b, t // tt),
        in_specs=[
            pl.BlockSpec((None, tt, 1024), lambda i, j: (i, j, COL_QA // 1024)),
            pl.BlockSpec((None, tt, 2 * LANES), lambda i, j: (i, j, blk + 1)),
            pl.BlockSpec((None, tt, 2 * LANES), lambda i, j: (i, j, blk + 2)),
            pl.BlockSpec((None, tt, SMALL_COLS), lambda i, j: (i, j, 0)),
            pl.BlockSpec((tt, LANES), lambda i, j: (j, 0)),
            pl.BlockSpec((tt, LANES), lambda i, j: (j, 0)),
        ],
        out_specs=[
            pl.BlockSpec((None, NSA_HEADS, tt, LANES), lambda i, j: (i, 0, j, 0)),
            kspec, kspec,
            pl.BlockSpec((None, NSA_GROUPS, tt // TK_SLC, HEAD_DIM, TK_SLC), lambda i, j: (i, 0, j, 0, 0)),
            pl.BlockSpec((None, NSA_GROUPS, tt // TK_WIN, HEAD_DIM, TK_WIN), lambda i, j: (i, 0, j, 0, 0)),
            pl.BlockSpec((None, 3, NSA_GROUPS, tt // TQ_NSA, NSA_HPG, TQ_NSA), lambda i, j: (i, 0, 0, j, 0, 0)),
        ],
        out_shape=[
            jax.ShapeDtypeStruct((b, NSA_HEADS, t, LANES), BF16),
            jax.ShapeDtypeStruct((b, NSA_GROUPS, t, LANES), BF16),
            jax.ShapeDtypeStruct((b, NSA_GROUPS, t, LANES), BF16),
            jax.ShapeDtypeStruct((b, NSA_GROUPS, t // TK_SLC, HEAD_DIM, TK_SLC), BF16),
            jax.ShapeDtypeStruct((b, NSA_GROUPS, t // TK_WIN, HEAD_DIM, TK_WIN), BF16),
            jax.ShapeDtypeStruct((b, 3, NSA_GROUPS, t // TQ_NSA, NSA_HPG, TQ_NSA), F32),
        ],
        compiler_params=_cparams(("parallel", "parallel")),
        name="nsa_prep",
    )(main3, main3, main3, small3, cos, sin)


def _cmp_kernel(q_ref, kc_ref, vc_ref, ovl_ref, sm_ref, o_ref, nsel_ref, *, tq):
    q0 = pl.program_id(1) * tq
    nc = kc_ref.shape[1]
    lane = _lane_iota((tq, LANES))
    t = q0 + lax.broadcasted_iota(jnp.int32, (tq, LANES), 0)
    gates = _sigmoid(sm_ref[:, 0:LANES])
    cmp_valid = (lane * CMP_STRIDE + CMP_BLOCK - 1 <= t) & (lane < nc)
    cur = t // SEL_BLOCK
    forced = (lane == 0) | (lane == cur) | (lane == cur - 1)
    blk_valid = lane * SEL_BLOCK <= t
    row_sel = lax.broadcasted_iota(jnp.int32, (N_SEL, tq), 0)
    for g in range(NSA_GROUPS):
        slabs = [(q_ref[:, (g * 4 + j) * LANES:(g * 4 + j + 1) * LANES].astype(F32) * SCALE).astype(BF16)
                 for j in range(4)]
        qs = _stack_heads(slabs)
        s3 = _dot_nt(qs, kc_ref[g]).reshape(NSA_HPG, tq, nc)
        s3 = jnp.where(cmp_valid[None], s3, NEG_INF)
        m = jnp.max(s3, axis=-1, keepdims=True)
        e = jnp.where(cmp_valid[None], jnp.exp(s3 - m), 0.0)
        l = jnp.sum(e, axis=-1, keepdims=True)
        p = (e / jnp.where(l > 0.0, l, 1.0)).astype(BF16)
        o3 = _dot(p.reshape(NSA_HPG * tq, nc), vc_ref[g]).reshape(NSA_HPG, tq, LANES)
        o3 = jnp.stack([o3[h] * _column(gates, g * NSA_HPG + h) for h in range(NSA_HPG)])
        for j, slab in enumerate(_unstack_heads(o3.reshape(NSA_HPG * tq, LANES), 4, tq)):
            o_ref[:, (g * 4 + j) * LANES:(g * 4 + j + 1) * LANES] = slab
        imp = _split3_dot(jnp.sum(p.astype(F32), axis=0), ovl_ref[...])
        score = jnp.where(forced, FORCED_SCORE, jnp.where(blk_valid, imp, -1.0))
        score = jnp.where(lane < N_SEL, score, -2.0)
        st = score.T[0:N_SEL, :]
        cnt = jnp.zeros((N_SEL, tq), F32)
        for j in range(N_SEL):
            rj = st[j:j + 1, :]
            beats = (rj > st) | ((rj == st) & (row_sel > j))
            cnt = cnt + jnp.where(beats, 1.0, 0.0)
        nsel_t = jnp.where(cnt < float(SEL_TOPK), 0.0, NEG_INF)
        nsel_t = jnp.concatenate([jnp.zeros((EXT, tq), F32), nsel_t,
                                  jnp.zeros((LANES - EXT - N_SEL, tq), F32)], axis=0)
        nsel_ref[g] = nsel_t.T.astype(BF16)


def _cmp_attn(main3, small3, kc, vc, ovl, tq=128):
    b, t, _ = main3.shape
    nc = kc.shape[2]
    return pl.pallas_call(
        functools.partial(_cmp_kernel, tq=tq),
        grid=(b, t // tq),
        in_specs=[
            pl.BlockSpec((None, tq, 1024), lambda i, j: (i, j, COL_QA // 1024)),
            pl.BlockSpec((None, NSA_GROUPS, nc, LANES), lambda i, j: (i, 0, 0, 0)),
            pl.BlockSpec((None, NSA_GROUPS, nc, LANES), lambda i, j: (i, 0, 0, 0)),
            pl.BlockSpec(ovl.shape, lambda i, j: (0, 0)),
            pl.BlockSpec((None, tq, SMALL_COLS), lambda i, j: (i, j, 0)),
        ],
        out_specs=[
            pl.BlockSpec((None, tq, 1024), lambda i, j: (i, j, 0)),
            pl.BlockSpec((None, NSA_GROUPS, tq, LANES), lambda i, j: (i, 0, j, 0)),
        ],
        out_shape=[jax.ShapeDtypeStruct((b, t, 1024), F32),
                   jax.ShapeDtypeStruct((b, NSA_GROUPS, t, LANES), BF16)],
        compiler_params=_cparams(("parallel", "parallel")),
        name="cmp_attn",
    )(main3, kc, vc, ovl, small3)


def _tile_lanes(x, n):
    return jnp.concatenate([x] * n, axis=1)


def _group_slabs(o_t, gates, tq):
    slabs = []
    for j in range(NSA_HPG // 2):
        a = o_t[:, (2 * j) * tq:(2 * j + 1) * tq] * gates[2 * j:2 * j + 1, :]
        b = o_t[:, (2 * j + 1) * tq:(2 * j + 2) * tq] * gates[2 * j + 1:2 * j + 2, :]
        slabs.append(jnp.concatenate([a, b], axis=0).T)
    return slabs


def _slc_kernel(qh_ref, k_ref, vt_ref, nsel_ref, gt_ref, prev_ref, o_ref, s_ref, acc_ref):
    tq, tk = TQ_NSA, TK_SLC
    n_lanes = NSA_HPG * tq
    t_total = k_ref.shape[0]
    key_l = lax.broadcasted_iota(jnp.int32, (tk, tq), 0)
    qry_l = lax.broadcasted_iota(jnp.int32, (tk, tq), 1)

    def qtile(i, carry):
        q0 = pl.multiple_of(i * tq, tq)
        q3 = qh_ref[:, pl.ds(q0, tq), :] + nsel_ref[pl.ds(q0, tq), :][None]
        qs = q3.reshape(n_lanes, LANES)
        c_last = q0 // tk

        def scores(c):
            k0 = pl.multiple_of(c * tk, tk)
            return _dot_nt(k_ref[pl.ds(k0, tk), :], qs)

        def pass1(c, mrun):
            s = scores(c)
            s_ref[c] = s
            return jnp.maximum(mrun, jnp.max(_fold_rows(s), axis=0))

        mrun = lax.fori_loop(0, c_last, pass1, jnp.full((SUBLANES, n_lanes), NEG_INF, F32))
        causal = jnp.where((c_last * tk + key_l) <= (q0 + qry_l), 0.0, NEG_INF)
        s = scores(c_last) + _tile_lanes(causal, NSA_HPG)
        s_ref[c_last] = s
        mrun = jnp.maximum(mrun, jnp.max(_fold_rows(s), axis=0))
        m = jnp.broadcast_to(jnp.max(mrun, axis=0, keepdims=True), (SUBLANES, n_lanes))

        acc_ref[...] = jnp.zeros(acc_ref.shape, F32)

        def pass2(c, lrun):
            p = jnp.exp2(_fold_rows(s_ref[c]) - m[None])
            acc_ref[...] += _dot(vt_ref[c], p.reshape(tk, n_lanes).astype(BF16))
            return lrun + jnp.sum(p, axis=0)

        lrun = lax.fori_loop(0, c_last + 1, pass2, jnp.zeros((SUBLANES, n_lanes), F32))
        o_t = acc_ref[...] / jnp.sum(lrun, axis=0, keepdims=True)
        for j, slab in enumerate(_group_slabs(o_t, gt_ref[i], tq)):
            cols = slice(j * LANES, (j + 1) * LANES)
            o_ref[pl.ds(q0, tq), cols] = prev_ref[pl.ds(q0, tq), cols] + slab
        return carry

    lax.fori_loop(0, t_total // tq, qtile, 0)


def _slc_attn(qh, ks, vst, nsel, gt, prev):
    b, _, t, _ = qh.shape
    gw = NSA_HPG * HEAD_DIM
    return pl.pallas_call(
        _slc_kernel,
        grid=(b, NSA_GROUPS),
        in_specs=[
            pl.BlockSpec((None, NSA_HPG, t, LANES), lambda i, g: (i, g, 0, 0)),
            pl.BlockSpec((None, None, t, LANES), lambda i, g: (i, g, 0, 0)),
            pl.BlockSpec((None, None, t // TK_SLC, HEAD_DIM, TK_SLC), lambda i, g: (i, g, 0, 0, 0)),
            pl.BlockSpec((None, None, t, LANES), lambda i, g: (i, g, 0, 0)),
            pl.BlockSpec((None, None, None, t // TQ_NSA, NSA_HPG, TQ_NSA), lambda i, g: (i, 1, g, 0, 0, 0)),
            pl.BlockSpec((None, t, gw), lambda i, g: (i, 0, g)),
        ],
        out_specs=pl.BlockSpec((None, t, gw), lambda i, g: (i, 0, g)),
        out_shape=jax.ShapeDtypeStruct((b, t, 1024), F32),
        scratch_shapes=[pltpu.VMEM((t // TK_SLC, TK_SLC, NSA_HPG * TQ_NSA), F32),
                        pltpu.VMEM((HEAD_DIM, NSA_HPG * TQ_NSA), F32)],
        compiler_params=_cparams(("parallel", "parallel")),
        name="slc_attn",
    )(qh, ks, vst, nsel, gt, prev)


def _win_kernel(qh_ref, k_ref, vt_ref, gt_ref, prev_ref, z_ref, y_ref, s_ref):
    tq, tk = TQ_NSA, TK_WIN
    n_lanes = NSA_HPG * tq
    n_chunks = WINDOW // tk + 1
    t_total = k_ref.shape[0]
    key_l = lax.broadcasted_iota(jnp.int32, (tk, tq), 0)
    qry_l = lax.broadcasted_iota(jnp.int32, (tk, tq), 1)

    def qtile(i, carry):
        q0 = pl.multiple_of(i * tq, tq)
        qs = qh_ref[:, pl.ds(q0, tq), :].reshape(n_lanes, LANES)
        start = pl.multiple_of(jnp.maximum(q0 - WINDOW, 0), tk)
        c0 = start // tk
        mrun = jnp.full((SUBLANES, n_lanes), NEG_INF, F32)
        for j in range(n_chunks):
            k0 = pl.multiple_of(start + j * tk, tk)
            dist = (k0 - q0) + key_l - qry_l
            band = jnp.where((dist <= 0) & (dist > -WINDOW), 0.0, NEG_INF)
            s = _dot_nt(k_ref[pl.ds(k0, tk), :], qs) + _tile_lanes(band, NSA_HPG)
            s_ref[j] = s
            mrun = jnp.maximum(mrun, jnp.max(_fold_rows(s), axis=0))
        m = jnp.broadcast_to(jnp.max(mrun, axis=0, keepdims=True), (SUBLANES, n_lanes))
        lrun = jnp.zeros((SUBLANES, n_lanes), F32)
        acc = jnp.zeros((HEAD_DIM, n_lanes), F32)
        for j in range(n_chunks):
            p = jnp.exp2(_fold_rows(s_ref[j]) - m[None])
            acc = acc + _dot(vt_ref[c0 + j], p.reshape(tk, n_lanes).astype(BF16))
            lrun = lrun + jnp.sum(p, axis=0)
        o_t = acc / jnp.sum(lrun, axis=0, keepdims=True)
        for j, slab in enumerate(_group_slabs(o_t, gt_ref[i], tq)):
            cols = slice(j * LANES, (j + 1) * LANES)
            z = z_ref[pl.ds(q0, tq), cols].astype(F32)
            y = (prev_ref[pl.ds(q0, tq), cols] + slab) * (z * _sigmoid(z))
            y_ref[pl.ds(q0, tq), cols] = y.astype(BF16)
        return carry

    lax.fori_loop(0, t_total // tq, qtile, 0)


def _win_attn(qh, kw, vwt, gt, prev, main3):
    b, _, t, _ = qh.shape
    gw = NSA_HPG * HEAD_DIM
    assert t >= WINDOW + TQ_NSA
    return pl.pallas_call(
        _win_kernel,
        grid=(b, NSA_GROUPS),
        in_specs=[
            pl.BlockSpec((None, NSA_HPG, t, LANES), lambda i, g: (i, g, 0, 0)),
            pl.BlockSpec((None, None, t, LANES), lambda i, g: (i, g, 0, 0)),
            pl.BlockSpec((None, None, t // TK_WIN, HEAD_DIM, TK_WIN), lambda i, g: (i, g, 0, 0, 0)),
            pl.BlockSpec((None, None, None, t // TQ_NSA, NSA_HPG, TQ_NSA), lambda i, g: (i, 2, g, 0, 0, 0)),
            pl.BlockSpec((None, t, gw), lambda i, g: (i, 0, g)),
            pl.BlockSpec((None, t, gw), lambda i, g: (i, 0, COL_ZA // gw + g)),
        ],
        out_specs=pl.BlockSpec((None, t, gw), lambda i, g: (i, 0, g)),
        out_shape=jax.ShapeDtypeStruct((b, t, 1024), BF16),
        scratch_shapes=[pltpu.VMEM((WINDOW // TK_WIN + 1, TK_WIN, NSA_HPG * TQ_NSA), F32)],
        compiler_params=_cparams(("parallel", "parallel")),
        name="win_attn",
    )(qh, kw, vwt, gt, prev, main3)


def _decay_kernel(sm_ref, b_ref, f_ref):
    t_total = sm_ref.shape[0]
    x = sm_ref[:, LANES:2 * LANES] + b_ref[...]
    f = jnp.minimum(x, 0.0) - jnp.log1p(jnp.exp(-jnp.abs(x)))
    row = lax.broadcasted_iota(jnp.int32, f.shape, 0)
    sh = 1
    while sh < t_total:
        f = f + jnp.where(row >= sh, pltpu.roll(f, sh, 0), 0.0)
        sh *= 2
    f_ref[...] = f


def _decay(small3, b_pad):
    b, t, _ = small3.shape
    return pl.pallas_call(
        _decay_kernel,
        grid=(b,),
        in_specs=[pl.BlockSpec((None, t, SMALL_COLS), lambda i: (i, 0, 0)),
                  pl.BlockSpec((1, LANES), lambda i: (0, 0))],
        out_specs=pl.BlockSpec((None, t, LANES), lambda i: (i, 0, 0)),
        out_shape=jax.ShapeDtypeStruct((b, t, LANES), F32),
        compiler_params=_cparams(("parallel",)),
        name="fox_decay",
    )(small3, b_pad)


def _foxprep_kernel(q_ref, k_ref, v_ref, f_ref, qf_ref, kf_ref, vt_ref, *, tt):
    p = pl.program_id(1)
    lane = _lane_iota((tt, LANES))
    low = lane < HEAD_DIM
    q = q_ref[...].astype(F32) * (SCALE * LOG2E)
    k = k_ref[...].astype(F32)
    ones_ext = jnp.where((lane >= EXT) & (lane < EXT + 3), 1.0, 0.0)
    v_t = v_ref[...].astype(F32).T
    f_all = f_ref[...]
    for e in range(2):
        qe = q if e == 0 else pltpu.roll(q, HEAD_DIM, 1)
        ke = k if e == 0 else pltpu.roll(k, HEAD_DIM, 1)
        qf_ref[e] = jnp.where(low, qe, ones_ext).astype(BF16)
        hi, mid, lo = _split3(_column(f_all, 2 * p + e) * (-LOG2E))
        ext = jnp.where(lane == EXT, hi, jnp.where(lane == EXT + 1, mid, jnp.where(lane == EXT + 2, lo, 0.0)))
        kf_ref[e] = jnp.where(low, ke, ext).astype(BF16)
        for c in range(tt // TQ_FOX):
            vt_ref[e, c] = v_t[e * HEAD_DIM:(e + 1) * HEAD_DIM, c * TQ_FOX:(c + 1) * TQ_FOX].astype(BF16)


def _foxprep(main3, f, tt=512):
    b, t, _ = main3.shape
    nslab = FOX_HEADS // 2

    def col(base):
        return pl.BlockSpec((None, tt, LANES), lambda i, p, j: (i, j, base // LANES + p))

    hspec = pl.BlockSpec((None, 2, tt, LANES), lambda i, p, j: (i, p, j, 0))
    return pl.pallas_call(
        functools.partial(_foxprep_kernel, tt=tt),
        grid=(b, nslab, t // tt),
        in_specs=[col(COL_QB), col(COL_KB), col(COL_VB),
                  pl.BlockSpec((None, tt, LANES), lambda i, p, j: (i, j, 0))],
        out_specs=[hspec, hspec,
                   pl.BlockSpec((None, 2, tt // TQ_FOX, HEAD_DIM, TQ_FOX), lambda i, p, j: (i, p, j, 0, 0))],
        out_shape=[jax.ShapeDtypeStruct((b, FOX_HEADS, t, LANES), BF16),
                   jax.ShapeDtypeStruct((b, FOX_HEADS, t, LANES), BF16),
                   jax.ShapeDtypeStruct((b, FOX_HEADS, t // TQ_FOX, HEAD_DIM, TQ_FOX), BF16)],
        compiler_params=_cparams(("parallel", "parallel", "parallel")),
        name="fox_prep",
    )(main3, main3, main3, f)


def _fox_kernel(qf_ref, kf_ref, vt_ref, z_ref, y_ref, s_ref, acc_ref):
    tq = TQ_FOX
    t_total = z_ref.shape[0]
    causal = (lax.broadcasted_iota(jnp.int32, (tq, tq), 0) <= lax.broadcasted_iota(jnp.int32, (tq, tq), 1))

    def qtile(i, carry):
        q0 = pl.multiple_of(i * tq, tq)
        qs = [qf_ref[h, pl.ds(q0, tq), :] for h in range(2)]

        def scores(h, c):
            k0 = pl.multiple_of(c * tq, tq)
            return _dot_nt(kf_ref[h, pl.ds(k0, tq), :], qs[h])

        def pass1(c, mruns):
            out = []
            for h in range(2):
                s = scores(h, c)
                s_ref[h, c] = s
                out.append(jnp.maximum(mruns[h], jnp.max(_fold_rows(s), axis=0)))
            return tuple(out)

        init = jnp.full((SUBLANES, tq), NEG_INF, F32)
        mruns = lax.fori_loop(0, i, pass1, (init, init))
        ms = []
        for h in range(2):
            s = jnp.where(causal, scores(h, i), NEG_INF)
            s_ref[h, i] = s
            mrun = jnp.maximum(mruns[h], jnp.max(_fold_rows(s), axis=0))
            ms.append(jnp.broadcast_to(jnp.max(mrun, axis=0, keepdims=True), (SUBLANES, tq)))

        acc_ref[...] = jnp.zeros(acc_ref.shape, F32)

        def pass2(c, lruns):
            out = []
            for h in range(2):
                p = jnp.exp2(_fold_rows(s_ref[h, c]) - ms[h][None])
                acc_ref[h] += _dot(vt_ref[h, c], p.reshape(tq, tq).astype(BF16))
                out.append(lruns[h] + jnp.sum(p, axis=0))
            return tuple(out)

        zero = jnp.zeros((SUBLANES, tq), F32)
        lruns = lax.fori_loop(0, i + 1, pass2, (zero, zero))
        o_t = jnp.concatenate([acc_ref[h] / jnp.sum(lruns[h], axis=0, keepdims=True) for h in range(2)], axis=0)
        z = z_ref[pl.ds(q0, tq), :].astype(F32)
        y_ref[pl.ds(q0, tq), :] = (o_t.T * (z * _sigmoid(z))).astype(BF16)
        return carry

    lax.fori_loop(0, t_total // tq, qtile, 0)


def _fox_attn(qf, kf, vt, main3):
    b, _, t, _ = qf.shape
    nslab = FOX_HEADS // 2
    hspec = pl.BlockSpec((None, 2, t, LANES), lambda i, p: (i, p, 0, 0))
    return pl.pallas_call(
        _fox_kernel,
        grid=(b, nslab),
        in_specs=[hspec, hspec,
                  pl.BlockSpec((None, 2, t // TQ_FOX, HEAD_DIM, TQ_FOX), lambda i, p: (i, p, 0, 0, 0)),
                  pl.BlockSpec((None, t, LANES), lambda i, p: (i, 0, COL_ZB // LANES + p))],
        out_specs=pl.BlockSpec((None, t, LANES), lambda i, p: (i, 0, p)),
        out_shape=jax.ShapeDtypeStruct((b, t, 1024), BF16),
        scratch_shapes=[pltpu.VMEM((2, t // TQ_FOX, TQ_FOX, TQ_FOX), F32),
                        pltpu.VMEM((2, HEAD_DIM, TQ_FOX), F32)],
        compiler_params=_cparams(("parallel", "parallel")),
        name="fox_attn",
    )(qf, kf, vt, main3)


def _out_kernel(x_ref, ya_ref, yb_ref, ra_ref, rb_ref, wn_ref, wf_ref, wo_ref, fg_ref, o_ref, *, final):
    ta = _dot(ya_ref[...], wn_ref[...])
    tb = _dot(yb_ref[...], wf_ref[...])
    merged = _sigmoid(ra_ref[...].astype(F32)) * ta + _sigmoid(rb_ref[...].astype(F32)) * tb
    out = x_ref[...] + _dot(merged.astype(BF16), wo_ref[...])
    if final:
        out = out * lax.rsqrt(jnp.mean(out * out, axis=-1, keepdims=True) + RMS_EPS) * fg_ref[...]
    o_ref[...] = out


def _out_proj(x2, ya2, yb2, main2, wn, wf, wo, fg, final, tm=512):
    n = x2.shape[0]
    tm = min(tm, n)
    row = lambda c: pl.BlockSpec((tm, 1024), lambda i: (i, c))
    w = pl.BlockSpec((1024, 1024), lambda i: (0, 0))
    return pl.pallas_call(
        functools.partial(_out_kernel, final=final),
        grid=(n // tm,),
        in_specs=[row(0), row(0), row(0), row(COL_RA // 1024), row(COL_RB // 1024), w, w, w,
                  pl.BlockSpec((1, 1024), lambda i: (0, 0))],
        out_specs=row(0),
        out_shape=jax.ShapeDtypeStruct((n, 1024), F32),
        compiler_params=_cparams(("parallel",)),
        name="out_proj",
    )(x2, ya2, yb2, main2, main2, wn, wf, wo, fg)


def _rope_tables(t):
    inv = ROPE_THETA ** (-jnp.arange(0, HEAD_DIM, 2, dtype=F32) / HEAD_DIM)
    ang = jnp.arange(t, dtype=F32)[:, None] * inv[None, :]
    c, s = jnp.cos(ang), jnp.sin(ang)
    return jnp.concatenate([c, c, c, c], axis=1), jnp.concatenate([-s, s, -s, s], axis=1)


def _overlap_matrix():
    n = np.arange(LANES)[:, None] * CMP_STRIDE
    j = np.arange(LANES)[None, :] * SEL_BLOCK
    m = (n < j + SEL_BLOCK) & (n + CMP_BLOCK > j) & (np.arange(LANES)[None, :] < N_SEL)
    return jnp.asarray(m.astype(np.float32), dtype=BF16)


def _reorder_w_in(w):
    qa, kv, ga, za, qb, kb, vb, fb, zb, mg = jnp.split(
        w, np.cumsum([1024, 768, 48, 1024, 1024, 1024, 1024, 16, 1024])[:9].tolist(), axis=1)
    main = jnp.concatenate([qa, za, qb, kb, vb, zb, mg, kv], axis=1).astype(BF16)
    d = w.shape[0]
    small = jnp.concatenate([ga, jnp.zeros((d, LANES - 48), F32), fb, jnp.zeros((d, LANES - 16), F32)],
                            axis=1).astype(BF16)
    return main, small


def _compress_weights(w1, w2):
    eye = jnp.eye(NSA_GROUPS, dtype=F32)
    w1r = w1.reshape(CMP_BLOCK, HEAD_DIM, CMP_HIDDEN)
    half = CMP_BLOCK // 2

    def blockdiag(wl):
        return jnp.einsum('ldh,gk->lgdkh', wl, eye).reshape(half * NSA_GROUPS * HEAD_DIM,
                                                            NSA_GROUPS * CMP_HIDDEN).astype(BF16)

    wa, wb = blockdiag(w1r[:half]), blockdiag(w1r[half:])
    w2d = jnp.einsum('hd,gk,c->gkhcd', w2, eye, jnp.ones((2,), F32)).reshape(
        NSA_GROUPS, NSA_GROUPS * CMP_HIDDEN, LANES).astype(BF16)
    return wa, wb, w2d


def kernel(x, norm_g, w_in, b_forget, cmp_pe_k, cmp_w1_k, cmp_w2_k, cmp_pe_v, cmp_w1_v, cmp_w2_v,
           w_proj_nsa, w_proj_fox, w_out, final_g):
    b, t, d = x.shape
    depth = norm_g.shape[0]
    n = b * t
    nc = t // CMP_STRIDE
    assert d == D_MODEL and nc == LANES and t // SEL_BLOCK == N_SEL
    cos, sin = _rope_tables(t)
    ovl = _overlap_matrix()
    fg = final_g.reshape(1, d)
    x2 = x.reshape(n, d)
    for l in range(depth):
        w_main, w_small = _reorder_w_in(w_in[l])
        main2, small2 = _inproj(x2, norm_g[l].reshape(1, d), w_main, w_small)
        main3 = main2.reshape(b, t, MAIN_COLS)
        small3 = small2.reshape(b, t, SMALL_COLS)
        xk = main3[:, :, COL_KV:COL_KV + LANES].reshape(b, nc, CMP_STRIDE * LANES)
        xv = main3[:, :, COL_KV + LANES:COL_KV + 2 * LANES].reshape(b, nc, CMP_STRIDE * LANES)
        wak, wbk, w2k = _compress_weights(cmp_w1_k[l], cmp_w2_k[l])
        wav, wbv, w2v = _compress_weights(cmp_w1_v[l], cmp_w2_v[l])
        pek = jnp.broadcast_to(cmp_pe_k[l].reshape(1, -1), (8, CMP_BLOCK * HEAD_DIM))
        pev = jnp.broadcast_to(cmp_pe_v[l].reshape(1, -1), (8, CMP_BLOCK * HEAD_DIM))
        kc, vc = _compress(xk, xv, pek, pev, cmp_w1_k[l].astype(BF16), cmp_w1_v[l].astype(BF16),
                           wak, wbk, wav, wbv, w2k, w2v)
        qh, ks, kw, vst, vwt, gt = _nsaprep(main3, small3, cos, sin)
        o_cmp, nsel = _cmp_attn(main3, small3, kc, vc, ovl)
        o_cs = _slc_attn(qh, ks, vst, nsel, gt, o_cmp)
        y_a = _win_attn(qh, kw, vwt, gt, o_cs, main3)
        b_pad = jnp.concatenate([b_forget[l], jnp.zeros((LANES - FOX_HEADS,), F32)]).reshape(1, LANES)
        qf, kf, vt = _foxprep(main3, _decay(small3, b_pad))
        y_b = _fox_attn(qf, kf, vt, main3)
        x2 = _out_proj(x2, y_a.reshape(n, 1024), y_b.reshape(n, 1024), main2,
                       w_proj_nsa[l].astype(BF16), w_proj_fox[l].astype(BF16), w_out[l].astype(BF16),
                       fg, final=(l == depth - 1))
    return x2.reshape(b, t, d)
```

```python
import functools

import numpy as np
import jax
import jax.numpy as jnp
from jax import lax
from jax.experimental import pallas as pl
from jax.experimental.pallas import tpu as pltpu

F32 = jnp.float32
BF16 = jnp.bfloat16

D_MODEL = 1024
HEAD_DIM = 64
LANES = 128
NSA_HEADS = 16
NSA_GROUPS = 2
NSA_HPG = NSA_HEADS // NSA_GROUPS
CMP_BLOCK = 32
CMP_STRIDE = 16
CMP_HIDDEN = 2 * HEAD_DIM
SEL_BLOCK = 64
SEL_TOPK = 8
N_SEL = 32
WINDOW = 512
FOX_HEADS = 16
ROPE_THETA = 10000.0
RMS_EPS = 1e-6
NEG_INF = -1e30
FORCED_SCORE = 1e4
SCALE = HEAD_DIM ** -0.5
LOG2E = 1.4426950408889634
EXT = HEAD_DIM

COL_QA, COL_ZA, COL_QB, COL_KB, COL_VB, COL_ZB, COL_RA, COL_RB, COL_KV = (
    0, 1024, 2048, 3072, 4096, 5120, 6144, 7168, 8192)
MAIN_COLS = 8960
SMALL_COLS = 256

TQ_NSA = 128
TK_SLC = 256
TQ_FOX = 256
VMEM_LIMIT = 56 * 1024 * 1024


def _cparams(sem):
    return pltpu.CompilerParams(dimension_semantics=sem, vmem_limit_bytes=VMEM_LIMIT)


def _lane_iota(shape):
    return lax.broadcasted_iota(jnp.int32, shape, len(shape) - 1)


def _dot_nt(a, b):
    return lax.dot_general(a, b, (((1,), (1,)), ((), ())), preferred_element_type=F32)


def _dot(a, b):
    return jnp.dot(a, b, preferred_element_type=F32)


def _split3(x):
    hi = x.astype(BF16).astype(F32)
    r = x - hi
    mid = r.astype(BF16).astype(F32)
    lo = (r - mid).astype(BF16).astype(F32)
    return hi, mid, lo


def _split3_dot(a, m):
    hi, mid, lo = _split3(a)
    return _dot(hi.astype(BF16), m) + _dot(mid.astype(BF16), m) + _dot(lo.astype(BF16), m)


def _sigmoid(x):
    return 1.0 / (1.0 + jnp.exp(-x))


def _column(x, idx):
    return jnp.sum(jnp.where(_lane_iota(x.shape) == idx, x, 0.0), axis=-1, keepdims=True)


def _rope_slab(x, cos, sin):
    lane = _lane_iota(x.shape)
    swapped = jnp.where((lane % HEAD_DIM) < HEAD_DIM // 2,
                        pltpu.roll(x, LANES - HEAD_DIM // 2, 1), pltpu.roll(x, HEAD_DIM // 2, 1))
    return x * cos + swapped * sin


def _stack_heads(slabs):
    lane = _lane_iota(slabs[0].shape)
    parts = []
    for s in slabs:
        parts.append(jnp.where(lane < HEAD_DIM, s, jnp.zeros_like(s)))
        parts.append(jnp.where(lane >= HEAD_DIM, s, jnp.zeros_like(s)))
    return jnp.concatenate(parts, axis=0)


def _unstack_heads(o, nslab, rows):
    lane = _lane_iota((rows, LANES))
    out = []
    for j in range(nslab):
        a = o[(2 * j) * rows:(2 * j + 1) * rows]
        b = o[(2 * j + 1) * rows:(2 * j + 2) * rows]
        out.append(jnp.where(lane < HEAD_DIM, a, b))
    return out


def _inproj_kernel(x_ref, g_ref, w_ref, ws_ref, main_ref, small_ref, xn_ref):
    @pl.when(pl.program_id(1) == 0)
    def _():
        x = x_ref[...]
        y = x * lax.rsqrt(jnp.mean(x * x, axis=-1, keepdims=True) + RMS_EPS) * g_ref[...]
        xn = y.astype(BF16)
        xn_ref[...] = xn
        small_ref[...] = _dot(xn, ws_ref[...])

    main_ref[...] = _dot(xn_ref[...], w_ref[...]).astype(BF16)


def _inproj(x2, g, w_main, w_small, tm=1024, tn=1280):
    n = x2.shape[0]
    tm = min(tm, n)
    return pl.pallas_call(
        _inproj_kernel,
        grid=(n // tm, MAIN_COLS // tn),
        in_specs=[
            pl.BlockSpec((tm, D_MODEL), lambda i, j: (i, 0)),
            pl.BlockSpec((1, D_MODEL), lambda i, j: (0, 0)),
            pl.BlockSpec((D_MODEL, tn), lambda i, j: (0, j)),
            pl.BlockSpec((D_MODEL, SMALL_COLS), lambda i, j: (0, 0)),
        ],
        out_specs=[
            pl.BlockSpec((tm, tn), lambda i, j: (i, j)),
            pl.BlockSpec((tm, SMALL_COLS), lambda i, j: (i, 0)),
        ],
        out_shape=[jax.ShapeDtypeStruct((n, MAIN_COLS), BF16),
                   jax.ShapeDtypeStruct((n, SMALL_COLS), F32)],
        scratch_shapes=[pltpu.VMEM((tm, D_MODEL), BF16)],
        compiler_params=_cparams(("parallel", "arbitrary")),
        name="inproj",
    )(x2, g, w_main, w_small)


def _compress_kernel(xk_ref, xv_ref, pek_ref, pev_ref, w1k_ref, w1v_ref,
                     wak_ref, wbk_ref, wav_ref, wbv_ref, w2k_ref, w2v_ref, kc_ref, vc_ref):
    def one(x_ref, pe_ref, w1_ref, wa_ref, wb_ref, w2_ref, out_ref):
        x = x_ref[...]
        a = _dot(x, wa_ref[...])
        b = _dot(x, wb_ref[...])
        nc = a.shape[0]
        b_up = pltpu.roll(b, nc - 1, 0)
        pe_c = _dot(pe_ref[...].astype(BF16), w1_ref[...])[0:1]
        pe_c = jnp.concatenate([pe_c, pe_c], axis=1)
        hid = a + b_up + pe_c
        hid = (hid * _sigmoid(hid)).astype(BF16)
        for g in range(NSA_GROUPS):
            out_ref[g] = _dot(hid, w2_ref[g]).astype(BF16)

    one(xk_ref, pek_ref, w1k_ref, wak_ref, wbk_ref, w2k_ref, kc_ref)
    one(xv_ref, pev_ref, w1v_ref, wav_ref, wbv_ref, w2v_ref, vc_ref)


def _compress(xk, xv, pek, pev, w1k, w1v, wak, wbk, wav, wbv, w2k, w2v):
    b, nc, kw = xk.shape
    full = lambda a: pl.BlockSpec(a.shape, lambda i: (0,) * a.ndim)
    xs = pl.BlockSpec((None, nc, kw), lambda i: (i, 0, 0))
    os_ = pl.BlockSpec((None, NSA_GROUPS, nc, LANES), lambda i: (i, 0, 0, 0))
    return pl.pallas_call(
        _compress_kernel,
        grid=(b,),
        in_specs=[xs, xs] + [full(a) for a in (pek, pev, w1k, w1v, wak, wbk, wav, wbv, w2k, w2v)],
        out_specs=[os_, os_],
        out_shape=[jax.ShapeDtypeStruct((b, NSA_GROUPS, nc, LANES), BF16)] * 2,
        compiler_params=_cparams(("parallel",)),
        name="compress",
    )(xk, xv, pek, pev, w1k, w1v, wak, wbk, wav, wbv, w2k, w2v)


def _nsaprep_kernel(q_ref, s_ref, w_ref, sm_ref, cos_ref, sin_ref,
                    qh_ref, ks_ref, kw_ref, vs_ref, vw_ref, gn_ref, *, tt):
    cos = cos_ref[...]
    sin = sin_ref[...]
    lane = _lane_iota((tt, LANES))
    t = pl.program_id(1) * tt + lax.broadcasted_iota(jnp.int32, (tt, LANES), 0)
    low = lane < HEAD_DIM
    for j in range(NSA_HEADS // 2):
        x = _rope_slab(q_ref[:, j * LANES:(j + 1) * LANES].astype(F32), cos, sin) * (SCALE * LOG2E)
        qh_ref[2 * j] = jnp.where(low, x, 0.0).astype(BF16)
        qh_ref[2 * j + 1] = jnp.where(low, pltpu.roll(x, HEAD_DIM, 1), 0.0).astype(BF16)

    onehot = jnp.where((lane >= EXT) & (lane - EXT == t // SEL_BLOCK) & (lane < EXT + N_SEL), 1.0, 0.0)
    ks = _rope_slab(s_ref[:, 0:LANES].astype(F32), cos, sin)
    kw = _rope_slab(w_ref[:, 0:LANES].astype(F32), cos, sin)
    for g in range(NSA_GROUPS):
        ks_ref[g] = jnp.where(low, ks if g == 0 else pltpu.roll(ks, HEAD_DIM, 1), onehot).astype(BF16)
        kw_ref[g] = jnp.where(low, kw if g == 0 else pltpu.roll(kw, HEAD_DIM, 1), 0.0).astype(BF16)

    vs = s_ref[:, LANES:2 * LANES].astype(F32)
    vw = w_ref[:, LANES:2 * LANES].astype(F32)
    for g in range(NSA_GROUPS):
        vs_ref[g] = jnp.where(low, vs if g == 0 else pltpu.roll(vs, HEAD_DIM, 1), 1.0).astype(BF16)
        vw_ref[g] = jnp.where(low, vw if g == 0 else pltpu.roll(vw, HEAD_DIM, 1), 1.0).astype(BF16)
    gn_ref[...] = _sigmoid(sm_ref[:, 0:LANES])


def _nsaprep(main3, small3, cos, sin, tt=512):
    b, t, _ = main3.shape
    blk = COL_KV // (2 * LANES)
    kspec = pl.BlockSpec((None, NSA_GROUPS, tt, LANES), lambda i, j: (i, 0, j, 0))
    return pl.pallas_call(
        functools.partial(_nsaprep_kernel, tt=tt),
        grid=(b, t // tt),
        in_specs=[
            pl.BlockSpec((None, tt, 1024), lambda i, j: (i, j, COL_QA // 1024)),
            pl.BlockSpec((None, tt, 2 * LANES), lambda i, j: (i, j, blk + 1)),
            pl.BlockSpec((None, tt, 2 * LANES), lambda i, j: (i, j, blk + 2)),
            pl.BlockSpec((None, tt, SMALL_COLS), lambda i, j: (i, j, 0)),
            pl.BlockSpec((tt, LANES), lambda i, j: (j, 0)),
            pl.BlockSpec((tt, LANES), lambda i, j: (j, 0)),
        ],
        out_specs=[
            pl.BlockSpec((None, NSA_HEADS, tt, LANES), lambda i, j: (i, 0, j, 0)),
            kspec, kspec, kspec, kspec,
            pl.BlockSpec((None, tt, LANES), lambda i, j: (i, j, 0)),
        ],
        out_shape=[jax.ShapeDtypeStruct((b, NSA_HEADS, t, LANES), BF16)]
        + [jax.ShapeDtypeStruct((b, NSA_GROUPS, t, LANES), BF16)] * 4
        + [jax.ShapeDtypeStruct((b, t, LANES), F32)],
        compiler_params=_cparams(("parallel", "parallel")),
        name="nsa_prep",
    )(main3, main3, main3, small3, cos, sin)


def _cmp_kernel(q_ref, kc_ref, vc_ref, ovl_ref, sm_ref, o_ref, nsel_ref, *, tq):
    q0 = pl.program_id(1) * tq
    nc = kc_ref.shape[1]
    lane = _lane_iota((tq, LANES))
    t = q0 + lax.broadcasted_iota(jnp.int32, (tq, LANES), 0)
    gates = _sigmoid(sm_ref[:, 0:LANES])
    cmp_valid = (lane * CMP_STRIDE + CMP_BLOCK - 1 <= t) & (lane < nc)
    cur = t // SEL_BLOCK
    forced = (lane == 0) | (lane == cur) | (lane == cur - 1)
    blk_valid = lane * SEL_BLOCK <= t
    row_sel = lax.broadcasted_iota(jnp.int32, (N_SEL, tq), 0)
    for g in range(NSA_GROUPS):
        slabs = [(q_ref[:, (g * 4 + j) * LANES:(g * 4 + j + 1) * LANES].astype(F32) * SCALE).astype(BF16)
                 for j in range(4)]
        qs = _stack_heads(slabs)
        s3 = _dot_nt(qs, kc_ref[g]).reshape(NSA_HPG, tq, nc)
        s3 = jnp.where(cmp_valid[None], s3, NEG_INF)
        m = jnp.max(s3, axis=-1, keepdims=True)
        e = jnp.where(cmp_valid[None], jnp.exp(s3 - m), 0.0)
        l = jnp.sum(e, axis=-1, keepdims=True)
        p = (e / jnp.where(l > 0.0, l, 1.0)).astype(BF16)
        o3 = _dot(p.reshape(NSA_HPG * tq, nc), vc_ref[g]).reshape(NSA_HPG, tq, LANES)
        o3 = jnp.stack([o3[h] * _column(gates, g * NSA_HPG + h) for h in range(NSA_HPG)])
        for j, slab in enumerate(_unstack_heads(o3.reshape(NSA_HPG * tq, LANES), 4, tq)):
            o_ref[:, (g * 4 + j) * LANES:(g * 4 + j + 1) * LANES] = slab
        imp = _split3_dot(jnp.sum(p.astype(F32), axis=0), ovl_ref[...])
        score = jnp.where(forced, FORCED_SCORE, jnp.where(blk_valid, imp, -1.0))
        score = jnp.where(lane < N_SEL, score, -2.0)
        st = score.T[0:N_SEL, :]
        cnt = jnp.zeros((N_SEL, tq), F32)
        for j in range(N_SEL):
            rj = st[j:j + 1, :]
            beats = (rj > st) | ((rj == st) & (row_sel > j))
            cnt = cnt + jnp.where(beats, 1.0, 0.0)
        nsel_t = jnp.where(cnt < float(SEL_TOPK), 0.0, NEG_INF)
        nsel_t = jnp.concatenate([jnp.zeros((EXT, tq), F32), nsel_t,
                                  jnp.zeros((LANES - EXT - N_SEL, tq), F32)], axis=0)
        nsel_ref[g] = nsel_t.T.astype(BF16)


def _cmp_attn(main3, small3, kc, vc, ovl, tq=128):
    b, t, _ = main3.shape
    nc = kc.shape[2]
    return pl.pallas_call(
        functools.partial(_cmp_kernel, tq=tq),
        grid=(b, t // tq),
        in_specs=[
            pl.BlockSpec((None, tq, 1024), lambda i, j: (i, j, COL_QA // 1024)),
            pl.BlockSpec((None, NSA_GROUPS, nc, LANES), lambda i, j: (i, 0, 0, 0)),
            pl.BlockSpec((None, NSA_GROUPS, nc, LANES), lambda i, j: (i, 0, 0, 0)),
            pl.BlockSpec(ovl.shape, lambda i, j: (0, 0)),
            pl.BlockSpec((None, tq, SMALL_COLS), lambda i, j: (i, j, 0)),
        ],
        out_specs=[
            pl.BlockSpec((None, tq, 1024), lambda i, j: (i, j, 0)),
            pl.BlockSpec((None, NSA_GROUPS, tq, LANES), lambda i, j: (i, 0, j, 0)),
        ],
        out_shape=[jax.ShapeDtypeStruct((b, t, 1024), F32),
                   jax.ShapeDtypeStruct((b, NSA_GROUPS, t, LANES), BF16)],
        compiler_params=_cparams(("parallel", "parallel")),
        name="cmp_attn",
    )(main3, kc, vc, ovl, small3)


def _tile_lanes(x, n):
    return jnp.concatenate([x] * n, axis=1)


def _lane_block_max(mrun, s):
    for w in range(s.shape[1] // LANES):
        mrun = jnp.maximum(mrun, s[:, w * LANES:(w + 1) * LANES])
    return mrun


def _row_max_tiled(mrun, n):
    return _tile_lanes(jnp.broadcast_to(jnp.max(mrun, axis=1, keepdims=True), mrun.shape), n)


def _group_slabs(acc, gates, first_gate, tq):
    lane = _lane_iota((tq, LANES))
    heads = []
    for h in range(NSA_HPG):
        a = acc[h * tq:(h + 1) * tq]
        heads.append(a * pltpu.roll(_column(gates, first_gate + h) / a, HEAD_DIM, 1))
    return [jnp.where(lane < HEAD_DIM, heads[2 * j], pltpu.roll(heads[2 * j + 1], HEAD_DIM, 1))
            for j in range(NSA_HPG // 2)]


def _slc_kernel(qh_ref, k_ref, v_ref, nsel_ref, gn_ref, prev_ref, o_ref, s_ref):
    tq, tk = TQ_NSA, TK_SLC
    rows = NSA_HPG * tq
    t_total = k_ref.shape[0]
    g = pl.program_id(1)
    qry_l = lax.broadcasted_iota(jnp.int32, (tq, tk), 0)
    key_l = lax.broadcasted_iota(jnp.int32, (tq, tk), 1)

    nq = t_total // tq
    for n, i in enumerate([j // 2 if j % 2 else nq - 1 - j // 2 for j in range(nq)]):
        q0 = i * tq
        sbuf = s_ref.at[n % 2]
        qs = (qh_ref[:, q0:q0 + tq, :] + nsel_ref[q0:q0 + tq, :][None]).reshape(rows, LANES)
        c_last = q0 // tk
        mrun = jnp.full((rows, LANES), NEG_INF, F32)
        for c in range(c_last + 1):
            s = _dot_nt(qs, k_ref[c * tk:(c + 1) * tk, :])
            if c == c_last:
                causal = jnp.where(c * tk + key_l <= q0 + qry_l, 0.0, NEG_INF)
                s = (s.reshape(NSA_HPG, tq, tk) + causal[None]).reshape(rows, tk)
            sbuf[c] = s
            mrun = _lane_block_max(mrun, s)
        m = _row_max_tiled(mrun, tk // LANES)
        acc = jnp.zeros((rows, LANES), F32)
        for c in range(c_last + 1):
            p = jnp.exp2(sbuf[c] - m).astype(BF16)
            acc = acc + _dot(p, v_ref[c * tk:(c + 1) * tk, :])
        slabs = _group_slabs(acc, gn_ref[q0:q0 + tq, :], NSA_HEADS + g * NSA_HPG, tq)
        for j, slab in enumerate(slabs):
            cols = slice(j * LANES, (j + 1) * LANES)
            o_ref[q0:q0 + tq, cols] = prev_ref[q0:q0 + tq, cols] + slab


def _slc_attn(qh, ks, vs, nsel, gn, prev):
    b, _, t, _ = qh.shape
    gw = NSA_HPG * HEAD_DIM
    kv = pl.BlockSpec((None, None, t, LANES), lambda i, g: (i, g, 0, 0))
    return pl.pallas_call(
        _slc_kernel,
        grid=(b, NSA_GROUPS),
        in_specs=[
            pl.BlockSpec((None, NSA_HPG, t, LANES), lambda i, g: (i, g, 0, 0), pipeline_mode=pl.Buffered(1)),
            kv, kv, kv,
            pl.BlockSpec((None, t, LANES), lambda i, g: (i, 0, 0)),
            pl.BlockSpec((None, t, gw), lambda i, g: (i, 0, g), pipeline_mode=pl.Buffered(1)),
        ],
        out_specs=pl.BlockSpec((None, t, gw), lambda i, g: (i, 0, g)),
        out_shape=jax.ShapeDtypeStruct((b, t, 1024), F32),
        scratch_shapes=[pltpu.VMEM((2, t // TK_SLC, NSA_HPG * TQ_NSA, TK_SLC), F32)],
        compiler_params=_cparams(("parallel", "parallel")),
        name="slc_attn",
    )(qh, ks, vs, nsel, gn, prev)


def _win_kernel(qh_ref, k_ref, v_ref, gn_ref, prev_ref, z_ref, y_ref, s_ref):
    tq = TQ_NSA
    rows = NSA_HPG * tq
    t_total = k_ref.shape[0]
    g = pl.program_id(1)
    qry_l = lax.broadcasted_iota(jnp.int32, (tq, LANES), 0)
    key_l = lax.broadcasted_iota(jnp.int32, (tq, LANES), 1)

    for i in range(t_total // tq):
        q0 = i * tq
        k_lo = max(q0 - WINDOW, 0)
        width = q0 + tq - k_lo
        sbuf = s_ref.at[i % 2]
        qs = qh_ref[:, q0:q0 + tq, :].reshape(rows, LANES)
        s = _dot_nt(qs, k_ref[k_lo:q0 + tq, :])
        mrun = jnp.full((rows, LANES), NEG_INF, F32)
        for w in range(width // LANES):
            k0 = k_lo + w * LANES
            blk = s[:, w * LANES:(w + 1) * LANES]
            if k0 == q0 or k0 == q0 - WINDOW:
                dist = (k0 - q0) + key_l - qry_l
                band = jnp.where((dist <= 0) & (dist > -WINDOW), 0.0, NEG_INF)
                blk = (blk.reshape(NSA_HPG, tq, LANES) + band[None]).reshape(rows, LANES)
            sbuf[:, w * LANES:(w + 1) * LANES] = blk
            mrun = jnp.maximum(mrun, blk)
        p = jnp.exp2(sbuf[:, 0:width] - _row_max_tiled(mrun, width // LANES)).astype(BF16)
        acc = _dot(p, v_ref[k_lo:q0 + tq, :])
        slabs = _group_slabs(acc, gn_ref[q0:q0 + tq, :], 2 * NSA_HEADS + g * NSA_HPG, tq)
        for j, slab in enumerate(slabs):
            cols = slice(j * LANES, (j + 1) * LANES)
            z = z_ref[q0:q0 + tq, cols].astype(F32)
            y = (prev_ref[q0:q0 + tq, cols] + slab) * (z * _sigmoid(z))
            y_ref[q0:q0 + tq, cols] = y.astype(BF16)


def _win_attn(qh, kw, vw, gn, prev, main3):
    b, _, t, _ = qh.shape
    gw = NSA_HPG * HEAD_DIM
    kv = pl.BlockSpec((None, None, t, LANES), lambda i, g: (i, g, 0, 0))
    return pl.pallas_call(
        _win_kernel,
        grid=(b, NSA_GROUPS),
        in_specs=[
            pl.BlockSpec((None, NSA_HPG, t, LANES), lambda i, g: (i, g, 0, 0)),
            kv, kv,
            pl.BlockSpec((None, t, LANES), lambda i, g: (i, 0, 0)),
            pl.BlockSpec((None, t, gw), lambda i, g: (i, 0, g)),
            pl.BlockSpec((None, t, gw), lambda i, g: (i, 0, COL_ZA // gw + g)),
        ],
        out_specs=pl.BlockSpec((None, t, gw), lambda i, g: (i, 0, g)),
        out_shape=jax.ShapeDtypeStruct((b, t, 1024), BF16),
        scratch_shapes=[pltpu.VMEM((2, NSA_HPG * TQ_NSA, WINDOW + TQ_NSA), F32)],
        compiler_params=_cparams(("parallel", "parallel")),
        name="win_attn",
    )(qh, kw, vw, gn, prev, main3)


def _decay_kernel(sm_ref, b_ref, f_ref):
    t_total = sm_ref.shape[0]
    x = sm_ref[:, LANES:2 * LANES] + b_ref[...]
    f = jnp.minimum(x, 0.0) - jnp.log1p(jnp.exp(-jnp.abs(x)))
    row = lax.broadcasted_iota(jnp.int32, f.shape, 0)
    sh = 1
    while sh < t_total:
        f = f + jnp.where(row >= sh, pltpu.roll(f, sh, 0), 0.0)
        sh *= 2
    f_ref[...] = f


def _decay(small3, b_pad):
    b, t, _ = small3.shape
    return pl.pallas_call(
        _decay_kernel,
        grid=(b,),
        in_specs=[pl.BlockSpec((None, t, SMALL_COLS), lambda i: (i, 0, 0)),
                  pl.BlockSpec((1, LANES), lambda i: (0, 0))],
        out_specs=pl.BlockSpec((None, t, LANES), lambda i: (i, 0, 0)),
        out_shape=jax.ShapeDtypeStruct((b, t, LANES), F32),
        compiler_params=_cparams(("parallel",)),
        name="fox_decay",
    )(small3, b_pad)


def _foxprep_kernel(q_ref, k_ref, v_ref, f_ref, qf_ref, kf_ref, vf_ref, *, tt):
    p = pl.program_id(1)
    lane = _lane_iota((tt, LANES))
    q = q_ref[...].astype(F32) * (SCALE * LOG2E)
    k = k_ref[...].astype(F32)
    v = v_ref[...].astype(F32)
    f_all = f_ref[...]
    for e in range(2):
        own = (lane < HEAD_DIM) if e == 0 else (lane >= HEAD_DIM)
        ext0 = (1 - e) * HEAD_DIM
        qf_ref[e] = jnp.where(own, q, jnp.where((lane >= ext0) & (lane < ext0 + 3), 1.0, 0.0)).astype(BF16)
        hi, mid, lo = _split3(_column(f_all, 2 * p + e) * (-LOG2E))
        ext = jnp.where(lane == ext0, hi, jnp.where(lane == ext0 + 1, mid, jnp.where(lane == ext0 + 2, lo, 0.0)))
        kf_ref[e] = jnp.where(own, k, ext).astype(BF16)
        vf_ref[e] = jnp.where(own, v, 1.0).astype(BF16)


def _foxprep(main3, f, tt=1024):
    b, t, _ = main3.shape
    nslab = FOX_HEADS // 2

    def col(base):
        return pl.BlockSpec((None, tt, LANES), lambda i, p, j: (i, j, base // LANES + p))

    hspec = pl.BlockSpec((None, 2, tt, LANES), lambda i, p, j: (i, p, j, 0))
    return pl.pallas_call(
        functools.partial(_foxprep_kernel, tt=tt),
        grid=(b, nslab, t // tt),
        in_specs=[col(COL_QB), col(COL_KB), col(COL_VB),
                  pl.BlockSpec((None, tt, LANES), lambda i, p, j: (i, j, 0))],
        out_specs=[hspec, hspec, hspec],
        out_shape=[jax.ShapeDtypeStruct((b, FOX_HEADS, t, LANES), BF16)] * 3,
        compiler_params=_cparams(("parallel", "parallel", "parallel")),
        name="fox_prep",
    )(main3, main3, main3, f)


def _fox_kernel(qf_ref, kf_ref, vf_ref, z_ref, y_ref, s_ref):
    tq = TQ_FOX
    t_total = z_ref.shape[0]
    causal = (lax.broadcasted_iota(jnp.int32, (tq, tq), 0) >= lax.broadcasted_iota(jnp.int32, (tq, tq), 1))
    lane = _lane_iota((tq, LANES))

    nq = t_total // tq
    order = [j // 2 if j % 2 else nq - 1 - j // 2 for j in range(nq)]
    for n, i in enumerate(order):
        q0 = i * tq
        outs = []
        for h in range(2):
            sbuf = s_ref.at[n % 2, h]
            q = qf_ref[h, q0:q0 + tq, :]
            mrun = jnp.full((tq, LANES), NEG_INF, F32)
            for c in range(i + 1):
                s = _dot_nt(q, kf_ref[h, c * tq:(c + 1) * tq, :])
                if c == i:
                    s = jnp.where(causal, s, NEG_INF)
                sbuf[c] = s
                mrun = _lane_block_max(mrun, s)
            m = _row_max_tiled(mrun, tq // LANES)
            acc = jnp.zeros((tq, LANES), F32)
            for c in range(i + 1):
                p = jnp.exp2(sbuf[c] - m).astype(BF16)
                acc = acc + _dot(p, vf_ref[h, c * tq:(c + 1) * tq, :])
            outs.append(acc / pltpu.roll(acc, HEAD_DIM, 1))
        slab = jnp.where(lane < HEAD_DIM, outs[0], outs[1])
        z = z_ref[q0:q0 + tq, :].astype(F32)
        y_ref[q0:q0 + tq, :] = (slab * (z * _sigmoid(z))).astype(BF16)


def _fox_attn(qf, kf, vt, main3):
    b, _, t, _ = qf.shape
    nslab = FOX_HEADS // 2
    hspec = pl.BlockSpec((None, 2, t, LANES), lambda i, p: (i, p, 0, 0))
    return pl.pallas_call(
        _fox_kernel,
        grid=(b, nslab),
        in_specs=[hspec, hspec, hspec,
                  pl.BlockSpec((None, t, LANES), lambda i, p: (i, 0, COL_ZB // LANES + p))],
        out_specs=pl.BlockSpec((None, t, LANES), lambda i, p: (i, 0, p)),
        out_shape=jax.ShapeDtypeStruct((b, t, 1024), BF16),
        scratch_shapes=[pltpu.VMEM((2, 2, t // TQ_FOX, TQ_FOX, TQ_FOX), F32)],
        compiler_params=_cparams(("parallel", "parallel")),
        name="fox_attn",
    )(qf, kf, vt, main3)


def _out_kernel(x_ref, ya_ref, yb_ref, ra_ref, rb_ref, wn_ref, wf_ref, wo_ref, fg_ref, o_ref, *, final):
    ta = _dot(ya_ref[...], wn_ref[...])
    tb = _dot(yb_ref[...], wf_ref[...])
    merged = _sigmoid(ra_ref[...].astype(F32)) * ta + _sigmoid(rb_ref[...].astype(F32)) * tb
    out = x_ref[...] + _dot(merged.astype(BF16), wo_ref[...])
    if final:
        out = out * lax.rsqrt(jnp.mean(out * out, axis=-1, keepdims=True) + RMS_EPS) * fg_ref[...]
    o_ref[...] = out


def _out_proj(x2, ya2, yb2, main2, wn, wf, wo, fg, final, tm=512):
    n = x2.shape[0]
    tm = min(tm, n)
    row = lambda c: pl.BlockSpec((tm, 1024), lambda i: (i, c))
    w = pl.BlockSpec((1024, 1024), lambda i: (0, 0))
    return pl.pallas_call(
        functools.partial(_out_kernel, final=final),
        grid=(n // tm,),
        in_specs=[row(0), row(0), row(0), row(COL_RA // 1024), row(COL_RB // 1024), w, w, w,
                  pl.BlockSpec((1, 1024), lambda i: (0, 0))],
        out_specs=row(0),
        out_shape=jax.ShapeDtypeStruct((n, 1024), F32),
        compiler_params=_cparams(("parallel",)),
        name="out_proj",
    )(x2, ya2, yb2, main2, main2, wn, wf, wo, fg)


def _rope_tables(t):
    inv = ROPE_THETA ** (-jnp.arange(0, HEAD_DIM, 2, dtype=F32) / HEAD_DIM)
    ang = jnp.arange(t, dtype=F32)[:, None] * inv[None, :]
    c, s = jnp.cos(ang), jnp.sin(ang)
    return jnp.concatenate([c, c, c, c], axis=1), jnp.concatenate([-s, s, -s, s], axis=1)


def _overlap_matrix():
    n = np.arange(LANES)[:, None] * CMP_STRIDE
    j = np.arange(LANES)[None, :] * SEL_BLOCK
    m = (n < j + SEL_BLOCK) & (n + CMP_BLOCK > j) & (np.arange(LANES)[None, :] < N_SEL)
    return jnp.asarray(m.astype(np.float32), dtype=BF16)


def _reorder_w_in(w):
    qa, kv, ga, za, qb, kb, vb, fb, zb, mg = jnp.split(
        w, np.cumsum([1024, 768, 48, 1024, 1024, 1024, 1024, 16, 1024])[:9].tolist(), axis=1)
    main = jnp.concatenate([qa, za, qb, kb, vb, zb, mg, kv], axis=1).astype(BF16)
    d = w.shape[0]
    small = jnp.concatenate([ga, jnp.zeros((d, LANES - 48), F32), fb, jnp.zeros((d, LANES - 16), F32)],
                            axis=1).astype(BF16)
    return main, small


def _compress_weights(w1, w2):
    eye = jnp.eye(NSA_GROUPS, dtype=F32)
    w1r = w1.reshape(CMP_BLOCK, HEAD_DIM, CMP_HIDDEN)
    half = CMP_BLOCK // 2

    def blockdiag(wl):
        return jnp.einsum('ldh,gk->lgdkh', wl, eye).reshape(half * NSA_GROUPS * HEAD_DIM,
                                                            NSA_GROUPS * CMP_HIDDEN).astype(BF16)

    wa, wb = blockdiag(w1r[:half]), blockdiag(w1r[half:])
    w2d = jnp.einsum('hd,gk,c->gkhcd', w2, eye, jnp.ones((2,), F32)).reshape(
        NSA_GROUPS, NSA_GROUPS * CMP_HIDDEN, LANES).astype(BF16)
    return wa, wb, w2d


def kernel(x, norm_g, w_in, b_forget, cmp_pe_k, cmp_w1_k, cmp_w2_k, cmp_pe_v, cmp_w1_v, cmp_w2_v,
           w_proj_nsa, w_proj_fox, w_out, final_g):
    b, t, d = x.shape
    depth = norm_g.shape[0]
    n = b * t
    nc = t // CMP_STRIDE
    assert d == D_MODEL and nc == LANES and t // SEL_BLOCK == N_SEL
    cos, sin = _rope_tables(t)
    ovl = _overlap_matrix()
    fg = final_g.reshape(1, d)
    x2 = x.reshape(n, d)
    for l in range(depth):
        w_main, w_small = _reorder_w_in(w_in[l])
        main2, small2 = _inproj(x2, norm_g[l].reshape(1, d), w_main, w_small)
        main3 = main2.reshape(b, t, MAIN_COLS)
        small3 = small2.reshape(b, t, SMALL_COLS)
        xk = main3[:, :, COL_KV:COL_KV + LANES].reshape(b, nc, CMP_STRIDE * LANES)
        xv = main3[:, :, COL_KV + LANES:COL_KV + 2 * LANES].reshape(b, nc, CMP_STRIDE * LANES)
        wak, wbk, w2k = _compress_weights(cmp_w1_k[l], cmp_w2_k[l])
        wav, wbv, w2v = _compress_weights(cmp_w1_v[l], cmp_w2_v[l])
        pek = jnp.broadcast_to(cmp_pe_k[l].reshape(1, -1), (8, CMP_BLOCK * HEAD_DIM))
        pev = jnp.broadcast_to(cmp_pe_v[l].reshape(1, -1), (8, CMP_BLOCK * HEAD_DIM))
        kc, vc = _compress(xk, xv, pek, pev, cmp_w1_k[l].astype(BF16), cmp_w1_v[l].astype(BF16),
                           wak, wbk, wav, wbv, w2k, w2v)
        qh, ks, kw, vs, vw, gn = _nsaprep(main3, small3, cos, sin)
        o_cmp, nsel = _cmp_attn(main3, small3, kc, vc, ovl)
        o_cs = _slc_attn(qh, ks, vs, nsel, gn, o_cmp)
        y_a = _win_attn(qh, kw, vw, gn, o_cs, main3)
        b_pad = jnp.concatenate([b_forget[l], jnp.zeros((LANES - FOX_HEADS,), F32)]).reshape(1, LANES)
        qf, kf, vt = _foxprep(main3, _decay(small3, b_pad))
        y_b = _fox_attn(qf, kf, vt, main3)
        x2 = _out_proj(x2, y_a.reshape(n, 1024), y_b.reshape(n, 1024), main2,
                       w_proj_nsa[l].astype(BF16), w_proj_fox[l].astype(BF16), w_out[l].astype(BF16),
                       fg, final=(l == depth - 1))
    return x2.reshape(b, t, d)
```

```python
import functools

import numpy as np
import jax
import jax.numpy as jnp
from jax import lax
from jax.experimental import pallas as pl
from jax.experimental.pallas import tpu as pltpu

F32 = jnp.float32
BF16 = jnp.bfloat16

D_MODEL = 1024
HEAD_DIM = 64
LANES = 128
NSA_HEADS = 16
NSA_GROUPS = 2
NSA_HPG = NSA_HEADS // NSA_GROUPS
CMP_BLOCK = 32
CMP_STRIDE = 16
CMP_HIDDEN = 2 * HEAD_DIM
SEL_BLOCK = 64
SEL_TOPK = 8
N_SEL = 32
WINDOW = 512
FOX_HEADS = 16
ROPE_THETA = 10000.0
RMS_EPS = 1e-6
NEG_INF = -1e30
FORCED_SCORE = 1e4
SCALE = HEAD_DIM ** -0.5
LOG2E = 1.4426950408889634
EXT = HEAD_DIM

COL_QA, COL_ZA, COL_QB, COL_KB, COL_VB, COL_ZB, COL_RA, COL_RB, COL_KV = (
    0, 1024, 2048, 3072, 4096, 5120, 6144, 7168, 8192)
MAIN_COLS = 8960
SMALL_COLS = 256

TQ_NSA = 128
TK_SLC = 256
TQ_FOX = 256
VMEM_LIMIT = 56 * 1024 * 1024


def _cparams(sem):
    return pltpu.CompilerParams(dimension_semantics=sem, vmem_limit_bytes=VMEM_LIMIT)


def _lane_iota(shape):
    return lax.broadcasted_iota(jnp.int32, shape, len(shape) - 1)


def _dot_nt(a, b):
    return lax.dot_general(a, b, (((1,), (1,)), ((), ())), preferred_element_type=F32)


def _dot(a, b):
    return jnp.dot(a, b, preferred_element_type=F32)


def _split3(x):
    hi = x.astype(BF16).astype(F32)
    r = x - hi
    mid = r.astype(BF16).astype(F32)
    lo = (r - mid).astype(BF16).astype(F32)
    return hi, mid, lo


def _split3_dot(a, m):
    hi, mid, lo = _split3(a)
    return _dot(hi.astype(BF16), m) + _dot(mid.astype(BF16), m) + _dot(lo.astype(BF16), m)


def _sigmoid(x):
    return 1.0 / (1.0 + jnp.exp(-x))


def _column(x, idx):
    return jnp.sum(jnp.where(_lane_iota(x.shape) == idx, x, 0.0), axis=-1, keepdims=True)


def _rope_slab(x, cos, sin):
    lane = _lane_iota(x.shape)
    swapped = jnp.where((lane % HEAD_DIM) < HEAD_DIM // 2,
                        pltpu.roll(x, LANES - HEAD_DIM // 2, 1), pltpu.roll(x, HEAD_DIM // 2, 1))
    return x * cos + swapped * sin


def _stack_heads(slabs):
    lane = _lane_iota(slabs[0].shape)
    parts = []
    for s in slabs:
        parts.append(jnp.where(lane < HEAD_DIM, s, jnp.zeros_like(s)))
        parts.append(jnp.where(lane >= HEAD_DIM, s, jnp.zeros_like(s)))
    return jnp.concatenate(parts, axis=0)


def _unstack_heads(o, nslab, rows):
    lane = _lane_iota((rows, LANES))
    out = []
    for j in range(nslab):
        a = o[(2 * j) * rows:(2 * j + 1) * rows]
        b = o[(2 * j + 1) * rows:(2 * j + 2) * rows]
        out.append(jnp.where(lane < HEAD_DIM, a, b))
    return out


def _inproj_kernel(x_ref, g_ref, w_ref, ws_ref, main_ref, small_ref, xn_ref):
    @pl.when(pl.program_id(1) == 0)
    def _():
        x = x_ref[...]
        y = x * lax.rsqrt(jnp.mean(x * x, axis=-1, keepdims=True) + RMS_EPS) * g_ref[...]
        xn = y.astype(BF16)
        xn_ref[...] = xn
        small_ref[...] = _dot(xn, ws_ref[...])

    main_ref[...] = _dot(xn_ref[...], w_ref[...]).astype(BF16)


def _inproj(x2, g, w_main, w_small, tm=1024, tn=1280):
    n = x2.shape[0]
    tm = min(tm, n)
    return pl.pallas_call(
        _inproj_kernel,
        grid=(n // tm, MAIN_COLS // tn),
        in_specs=[
            pl.BlockSpec((tm, D_MODEL), lambda i, j: (i, 0)),
            pl.BlockSpec((1, D_MODEL), lambda i, j: (0, 0)),
            pl.BlockSpec((D_MODEL, tn), lambda i, j: (0, j)),
            pl.BlockSpec((D_MODEL, SMALL_COLS), lambda i, j: (0, 0)),
        ],
        out_specs=[
            pl.BlockSpec((tm, tn), lambda i, j: (i, j)),
            pl.BlockSpec((tm, SMALL_COLS), lambda i, j: (i, 0)),
        ],
        out_shape=[jax.ShapeDtypeStruct((n, MAIN_COLS), BF16),
                   jax.ShapeDtypeStruct((n, SMALL_COLS), F32)],
        scratch_shapes=[pltpu.VMEM((tm, D_MODEL), BF16)],
        compiler_params=_cparams(("parallel", "arbitrary")),
        name="inproj",
    )(x2, g, w_main, w_small)


def _compress_kernel(xk_ref, xv_ref, pek_ref, pev_ref, w1k_ref, w1v_ref,
                     wak_ref, wbk_ref, wav_ref, wbv_ref, w2k_ref, w2v_ref, kc_ref, vc_ref):
    def one(x_ref, pe_ref, w1_ref, wa_ref, wb_ref, w2_ref, out_ref):
        x = x_ref[...]
        a = _dot(x, wa_ref[...])
        b = _dot(x, wb_ref[...])
        nc = a.shape[0]
        b_up = pltpu.roll(b, nc - 1, 0)
        pe_c = _dot(pe_ref[...].astype(BF16), w1_ref[...])[0:1]
        pe_c = jnp.concatenate([pe_c, pe_c], axis=1)
        hid = a + b_up + pe_c
        hid = (hid * _sigmoid(hid)).astype(BF16)
        for g in range(NSA_GROUPS):
            out_ref[g] = _dot(hid, w2_ref[g]).astype(BF16)

    one(xk_ref, pek_ref, w1k_ref, wak_ref, wbk_ref, w2k_ref, kc_ref)
    one(xv_ref, pev_ref, w1v_ref, wav_ref, wbv_ref, w2v_ref, vc_ref)


def _compress(xk, xv, pek, pev, w1k, w1v, wak, wbk, wav, wbv, w2k, w2v):
    b, nc, kw = xk.shape
    full = lambda a: pl.BlockSpec(a.shape, lambda i: (0,) * a.ndim)
    xs = pl.BlockSpec((None, nc, kw), lambda i: (i, 0, 0))
    os_ = pl.BlockSpec((None, NSA_GROUPS, nc, LANES), lambda i: (i, 0, 0, 0))
    return pl.pallas_call(
        _compress_kernel,
        grid=(b,),
        in_specs=[xs, xs] + [full(a) for a in (pek, pev, w1k, w1v, wak, wbk, wav, wbv, w2k, w2v)],
        out_specs=[os_, os_],
        out_shape=[jax.ShapeDtypeStruct((b, NSA_GROUPS, nc, LANES), BF16)] * 2,
        compiler_params=_cparams(("parallel",)),
        name="compress",
    )(xk, xv, pek, pev, w1k, w1v, wak, wbk, wav, wbv, w2k, w2v)


def _nsaprep_kernel(q_ref, s_ref, w_ref, sm_ref, cos_ref, sin_ref, gmat_ref,
                    qh_ref, ks_ref, kw_ref, vs_ref, vw_ref, ge_ref, *, tt):
    cos = cos_ref[...]
    sin = sin_ref[...]
    lane = _lane_iota((tt, LANES))
    t = pl.program_id(1) * tt + lax.broadcasted_iota(jnp.int32, (tt, LANES), 0)
    low = lane < HEAD_DIM
    for j in range(NSA_HEADS // 2):
        x = _rope_slab(q_ref[:, j * LANES:(j + 1) * LANES].astype(F32), cos, sin) * (SCALE * LOG2E)
        qh_ref[2 * j] = jnp.where(low, x, 0.0).astype(BF16)
        qh_ref[2 * j + 1] = jnp.where(low, pltpu.roll(x, HEAD_DIM, 1), 0.0).astype(BF16)

    onehot = jnp.where((lane >= EXT) & (lane - EXT == t // SEL_BLOCK) & (lane < EXT + N_SEL), 1.0, 0.0)
    ks = _rope_slab(s_ref[:, 0:LANES].astype(F32), cos, sin)
    kw = _rope_slab(w_ref[:, 0:LANES].astype(F32), cos, sin)
    for g in range(NSA_GROUPS):
        ks_ref[g] = jnp.where(low, ks if g == 0 else pltpu.roll(ks, HEAD_DIM, 1), onehot).astype(BF16)
        kw_ref[g] = jnp.where(low, kw if g == 0 else pltpu.roll(kw, HEAD_DIM, 1), 0.0).astype(BF16)

    for src_ref, dst_ref in ((s_ref, vs_ref), (w_ref, vw_ref)):
        v = src_ref[:, LANES:2 * LANES].astype(F32)
        v_sw = pltpu.roll(v, HEAD_DIM, 1)
        dst_ref[0, 0] = jnp.where(low, v, 1.0).astype(BF16)
        dst_ref[0, 1] = jnp.where(low, 1.0, v_sw).astype(BF16)
        dst_ref[1, 0] = jnp.where(low, v_sw, 1.0).astype(BF16)
        dst_ref[1, 1] = jnp.where(low, 1.0, v).astype(BF16)
    gates = _sigmoid(sm_ref[:, 0:LANES]).astype(BF16)
    for br in range(3):
        ge_ref[br] = _dot(gates, gmat_ref[br]).astype(BF16)


def _nsaprep(main3, small3, cos, sin, gmat, tt=512):
    b, t, _ = main3.shape
    blk = COL_KV // (2 * LANES)
    kspec = pl.BlockSpec((None, NSA_GROUPS, tt, LANES), lambda i, j: (i, 0, j, 0))
    vspec = pl.BlockSpec((None, NSA_GROUPS, 2, tt, LANES), lambda i, j: (i, 0, 0, j, 0))
    return pl.pallas_call(
        functools.partial(_nsaprep_kernel, tt=tt),
        grid=(b, t // tt),
        in_specs=[
            pl.BlockSpec((None, tt, 1024), lambda i, j: (i, j, COL_QA // 1024)),
            pl.BlockSpec((None, tt, 2 * LANES), lambda i, j: (i, j, blk + 1)),
            pl.BlockSpec((None, tt, 2 * LANES), lambda i, j: (i, j, blk + 2)),
            pl.BlockSpec((None, tt, SMALL_COLS), lambda i, j: (i, j, 0)),
            pl.BlockSpec((tt, LANES), lambda i, j: (j, 0)),
            pl.BlockSpec((tt, LANES), lambda i, j: (j, 0)),
            pl.BlockSpec(gmat.shape, lambda i, j: (0, 0, 0)),
        ],
        out_specs=[
            pl.BlockSpec((None, NSA_HEADS, tt, LANES), lambda i, j: (i, 0, j, 0)),
            kspec, kspec, vspec, vspec,
            pl.BlockSpec((None, 3, tt, 1024), lambda i, j: (i, 0, j, 0)),
        ],
        out_shape=[jax.ShapeDtypeStruct((b, NSA_HEADS, t, LANES), BF16)]
        + [jax.ShapeDtypeStruct((b, NSA_GROUPS, t, LANES), BF16)] * 2
        + [jax.ShapeDtypeStruct((b, NSA_GROUPS, 2, t, LANES), BF16)] * 2
        + [jax.ShapeDtypeStruct((b, 3, t, 1024), BF16)],
        compiler_params=_cparams(("parallel", "parallel")),
        name="nsa_prep",
    )(main3, main3, main3, small3, cos, sin, gmat)


def _cmp_kernel(q_ref, kc_ref, vc_ref, ovl_ref, ge_ref, o_ref, nsel_ref, *, tq):
    q0 = pl.program_id(1) * tq
    nc = kc_ref.shape[1]
    lane = _lane_iota((tq, LANES))
    t = q0 + lax.broadcasted_iota(jnp.int32, (tq, LANES), 0)
    cmp_valid = (lane * CMP_STRIDE + CMP_BLOCK - 1 <= t) & (lane < nc)
    cur = t // SEL_BLOCK
    forced = (lane == 0) | (lane == cur) | (lane == cur - 1)
    blk_valid = lane * SEL_BLOCK <= t
    row_sel = lax.broadcasted_iota(jnp.int32, (N_SEL, tq), 0)
    for g in range(NSA_GROUPS):
        slabs = [(q_ref[:, (g * 4 + j) * LANES:(g * 4 + j + 1) * LANES].astype(F32) * SCALE).astype(BF16)
                 for j in range(4)]
        qs = _stack_heads(slabs)
        s3 = _dot_nt(qs, kc_ref[g]).reshape(NSA_HPG, tq, nc)
        s3 = jnp.where(cmp_valid[None], s3, NEG_INF)
        m = jnp.max(s3, axis=-1, keepdims=True)
        e = jnp.where(cmp_valid[None], jnp.exp(s3 - m), 0.0)
        l = jnp.sum(e, axis=-1, keepdims=True)
        p = (e / jnp.where(l > 0.0, l, 1.0)).astype(BF16)
        o = _dot(p.reshape(NSA_HPG * tq, nc), vc_ref[g])
        for j, slab in enumerate(_unstack_heads(o, 4, tq)):
            cols = slice((g * 4 + j) * LANES, (g * 4 + j + 1) * LANES)
            o_ref[:, cols] = slab * ge_ref[:, cols].astype(F32)
        imp = _split3_dot(jnp.sum(p.astype(F32), axis=0), ovl_ref[...])
        score = jnp.where(forced, FORCED_SCORE, jnp.where(blk_valid, imp, -1.0))
        score = jnp.where(lane < N_SEL, score, -2.0)
        st = score.T[0:N_SEL, :]
        cnt = jnp.zeros((N_SEL, tq), F32)
        for j in range(N_SEL):
            rj = st[j:j + 1, :]
            beats = (rj > st) | ((rj == st) & (row_sel > j))
            cnt = cnt + jnp.where(beats, 1.0, 0.0)
        nsel_t = jnp.where(cnt < float(SEL_TOPK), 0.0, NEG_INF)
        nsel_t = jnp.concatenate([jnp.zeros((EXT, tq), F32), nsel_t,
                                  jnp.zeros((LANES - EXT - N_SEL, tq), F32)], axis=0)
        nsel_ref[g] = nsel_t.T.astype(BF16)


def _cmp_attn(main3, ge, kc, vc, ovl, tq=128):
    b, t, _ = main3.shape
    nc = kc.shape[2]
    return pl.pallas_call(
        functools.partial(_cmp_kernel, tq=tq),
        grid=(b, t // tq),
        in_specs=[
            pl.BlockSpec((None, tq, 1024), lambda i, j: (i, j, COL_QA // 1024)),
            pl.BlockSpec((None, NSA_GROUPS, nc, LANES), lambda i, j: (i, 0, 0, 0)),
            pl.BlockSpec((None, NSA_GROUPS, nc, LANES), lambda i, j: (i, 0, 0, 0)),
            pl.BlockSpec(ovl.shape, lambda i, j: (0, 0)),
            pl.BlockSpec((None, None, tq, 1024), lambda i, j: (i, 0, j, 0)),
        ],
        out_specs=[
            pl.BlockSpec((None, tq, 1024), lambda i, j: (i, j, 0)),
            pl.BlockSpec((None, NSA_GROUPS, tq, LANES), lambda i, j: (i, 0, j, 0)),
        ],
        out_shape=[jax.ShapeDtypeStruct((b, t, 1024), F32),
                   jax.ShapeDtypeStruct((b, NSA_GROUPS, t, LANES), BF16)],
        compiler_params=_cparams(("parallel", "parallel")),
        name="cmp_attn",
    )(main3, kc, vc, ovl, ge)


def _tile_lanes(x, n):
    return jnp.concatenate([x] * n, axis=1)


def _lane_block_max(mrun, s):
    for w in range(s.shape[1] // LANES):
        mrun = jnp.maximum(mrun, s[:, w * LANES:(w + 1) * LANES])
    return mrun


def _row_max_tiled(mrun, n):
    return _tile_lanes(jnp.broadcast_to(jnp.max(mrun, axis=1, keepdims=True), mrun.shape), n)


def _emit_pipelined(units, score_stage, value_stage):
    def drain(gens):
        gens = [g for g in gens if g is not None]
        while gens:
            for g in list(gens):
                try:
                    next(g)
                except StopIteration:
                    gens.remove(g)

    drain([score_stage(units[0])])
    for n, u in enumerate(units):
        drain([score_stage(units[n + 1]) if n + 1 < len(units) else None, value_stage(u)])


EVEN_ODD = tuple(range(0, NSA_HPG, 2)) + tuple(range(1, NSA_HPG, 2))


def _group_slabs(acc_even, acc_odd, gates, tq):
    low = _lane_iota((tq, LANES)) < HEAD_DIM
    slabs = []
    for j in range(NSA_HPG // 2):
        e = acc_even[j * tq:(j + 1) * tq]
        o = acc_odd[j * tq:(j + 1) * tq]
        den = pltpu.roll(jnp.where(low, o, e), HEAD_DIM, 1)
        slabs.append(jnp.where(low, e, o) / den * gates[:, j * LANES:(j + 1) * LANES].astype(F32))
    return slabs


def _slc_kernel(qh_ref, k_ref, v_ref, nsel_ref, ge_ref, prev_ref, o_ref, s_ref):
    tq, tk = TQ_NSA, TK_SLC
    rows = NSA_HPG * tq
    t_total = k_ref.shape[0]
    qry_l = lax.broadcasted_iota(jnp.int32, (tq, tk), 0)
    key_l = lax.broadcasted_iota(jnp.int32, (tq, tk), 1)

    nq = t_total // tq
    order = [j // 2 if j % 2 else nq - 1 - j // 2 for j in range(nq)]
    row_max = {}

    def score_stage(i):
        q0 = i * tq
        sbuf = s_ref.at[order.index(i) % 2]
        nsel = nsel_ref[q0:q0 + tq, :]
        qs = jnp.concatenate([qh_ref[h, q0:q0 + tq, :] + nsel for h in EVEN_ODD], axis=0)
        c_last = q0 // tk
        mrun = jnp.full((rows, LANES), NEG_INF, F32)
        for c in range(c_last + 1):
            s = _dot_nt(qs, k_ref[c * tk:(c + 1) * tk, :])
            if c == c_last:
                causal = jnp.where(c * tk + key_l <= q0 + qry_l, 0.0, NEG_INF)
                s = (s.reshape(NSA_HPG, tq, tk) + causal[None]).reshape(rows, tk)
            sbuf[c] = s
            mrun = _lane_block_max(mrun, s)
            yield
        row_max[i] = _row_max_tiled(mrun, tk // LANES)

    def value_stage(i):
        q0 = i * tq
        sbuf = s_ref.at[order.index(i) % 2]
        m = row_max.pop(i)
        acc = [jnp.zeros((rows // 2, LANES), F32)] * 2
        for c in range(q0 // tk + 1):
            p = jnp.exp2(sbuf[c] - m).astype(BF16)
            for e in range(2):
                acc[e] = acc[e] + _dot(p[e * rows // 2:(e + 1) * rows // 2], v_ref[e, c * tk:(c + 1) * tk, :])
            yield
        for j, slab in enumerate(_group_slabs(acc[0], acc[1], ge_ref[q0:q0 + tq, :], tq)):
            cols = slice(j * LANES, (j + 1) * LANES)
            o_ref[q0:q0 + tq, cols] = prev_ref[q0:q0 + tq, cols] + slab

    _emit_pipelined(order, score_stage, value_stage)


def _slc_attn(qh, ks, vs, nsel, ge, prev):
    b, _, t, _ = qh.shape
    gw = NSA_HPG * HEAD_DIM
    kv = pl.BlockSpec((None, None, t, LANES), lambda i, g: (i, g, 0, 0))
    return pl.pallas_call(
        _slc_kernel,
        grid=(b, NSA_GROUPS),
        in_specs=[
            pl.BlockSpec((None, NSA_HPG, t, LANES), lambda i, g: (i, g, 0, 0), pipeline_mode=pl.Buffered(1)),
            kv,
            pl.BlockSpec((None, None, 2, t, LANES), lambda i, g: (i, g, 0, 0, 0)),
            kv,
            pl.BlockSpec((None, None, t, gw), lambda i, g: (i, 1, 0, g)),
            pl.BlockSpec((None, t, gw), lambda i, g: (i, 0, g), pipeline_mode=pl.Buffered(1)),
        ],
        out_specs=pl.BlockSpec((None, t, gw), lambda i, g: (i, 0, g)),
        out_shape=jax.ShapeDtypeStruct((b, t, 1024), F32),
        scratch_shapes=[pltpu.VMEM((2, t // TK_SLC, NSA_HPG * TQ_NSA, TK_SLC), F32)],
        compiler_params=_cparams(("parallel", "parallel")),
        name="slc_attn",
    )(qh, ks, vs, nsel, ge, prev)


def _win_kernel(qh_ref, k_ref, v_ref, ge_ref, prev_ref, z_ref, y_ref, s_ref):
    tq = TQ_NSA
    rows = NSA_HPG * tq
    t_total = k_ref.shape[0]
    qry_l = lax.broadcasted_iota(jnp.int32, (tq, LANES), 0)
    key_l = lax.broadcasted_iota(jnp.int32, (tq, LANES), 1)

    step = 2 * LANES
    row_max = {}

    def window(i):
        k_lo = max(i * tq - WINDOW, 0)
        return k_lo, [(c, min(step, i * tq + tq - k_lo - c)) for c in range(0, i * tq + tq - k_lo, step)]

    def score_stage(i):
        q0 = i * tq
        k_lo, pieces = window(i)
        sbuf = s_ref.at[i % 2]
        qs = jnp.concatenate([qh_ref[h, q0:q0 + tq, :] for h in EVEN_ODD], axis=0)
        mrun = jnp.full((rows, LANES), NEG_INF, F32)
        for c0, cw in pieces:
            s = _dot_nt(qs, k_ref[k_lo + c0:k_lo + c0 + cw, :])
            for w in range(cw // LANES):
                k0 = k_lo + c0 + w * LANES
                blk = s[:, w * LANES:(w + 1) * LANES]
                if k0 == q0 or k0 == q0 - WINDOW:
                    dist = (k0 - q0) + key_l - qry_l
                    band = jnp.where((dist <= 0) & (dist > -WINDOW), 0.0, NEG_INF)
                    blk = (blk.reshape(NSA_HPG, tq, LANES) + band[None]).reshape(rows, LANES)
                sbuf[:, c0 + w * LANES:c0 + (w + 1) * LANES] = blk
                mrun = jnp.maximum(mrun, blk)
            yield
        row_max[i] = jnp.broadcast_to(jnp.max(mrun, axis=1, keepdims=True), mrun.shape)

    def value_stage(i):
        q0 = i * tq
        k_lo, pieces = window(i)
        sbuf = s_ref.at[i % 2]
        m = row_max.pop(i)
        acc = [jnp.zeros((rows // 2, LANES), F32)] * 2
        for c0, cw in pieces:
            p = jnp.exp2(sbuf[:, c0:c0 + cw] - _tile_lanes(m, cw // LANES)).astype(BF16)
            for e in range(2):
                acc[e] = acc[e] + _dot(p[e * rows // 2:(e + 1) * rows // 2], v_ref[e, k_lo + c0:k_lo + c0 + cw, :])
            yield
        for j, slab in enumerate(_group_slabs(acc[0], acc[1], ge_ref[q0:q0 + tq, :], tq)):
            cols = slice(j * LANES, (j + 1) * LANES)
            z = z_ref[q0:q0 + tq, cols].astype(F32)
            y = (prev_ref[q0:q0 + tq, cols] + slab) * (z * _sigmoid(z))
            y_ref[q0:q0 + tq, cols] = y.astype(BF16)

    _emit_pipelined(list(range(t_total // tq)), score_stage, value_stage)


def _win_attn(qh, kw, vw, ge, prev, main3):
    b, _, t, _ = qh.shape
    gw = NSA_HPG * HEAD_DIM
    kv = pl.BlockSpec((None, None, t, LANES), lambda i, g: (i, g, 0, 0))
    return pl.pallas_call(
        _win_kernel,
        grid=(b, NSA_GROUPS),
        in_specs=[
            pl.BlockSpec((None, NSA_HPG, t, LANES), lambda i, g: (i, g, 0, 0)),
            kv,
            pl.BlockSpec((None, None, 2, t, LANES), lambda i, g: (i, g, 0, 0, 0)),
            pl.BlockSpec((None, None, t, gw), lambda i, g: (i, 2, 0, g)),
            pl.BlockSpec((None, t, gw), lambda i, g: (i, 0, g)),
            pl.BlockSpec((None, t, gw), lambda i, g: (i, 0, COL_ZA // gw + g)),
        ],
        out_specs=pl.BlockSpec((None, t, gw), lambda i, g: (i, 0, g)),
        out_shape=jax.ShapeDtypeStruct((b, t, 1024), BF16),
        scratch_shapes=[pltpu.VMEM((2, NSA_HPG * TQ_NSA, WINDOW + TQ_NSA), F32)],
        compiler_params=_cparams(("parallel", "parallel")),
        name="win_attn",
    )(qh, kw, vw, ge, prev, main3)


def _decay_kernel(sm_ref, b_ref, f_ref):
    t_total = sm_ref.shape[0]
    x = sm_ref[:, LANES:2 * LANES] + b_ref[...]
    f = jnp.minimum(x, 0.0) - jnp.log1p(jnp.exp(-jnp.abs(x)))
    row = lax.broadcasted_iota(jnp.int32, f.shape, 0)
    sh = 1
    while sh < t_total:
        f = f + jnp.where(row >= sh, pltpu.roll(f, sh, 0), 0.0)
        sh *= 2
    hi, mid, lo = _split3(f * (-LOG2E))
    lane = _lane_iota(f.shape)
    pieces = jnp.where(lane < FOX_HEADS, hi,
                       jnp.where(lane < 2 * FOX_HEADS, pltpu.roll(mid, FOX_HEADS, 1),
                                 jnp.where(lane < 3 * FOX_HEADS, pltpu.roll(lo, 2 * FOX_HEADS, 1), 0.0)))
    f_ref[...] = pieces.astype(BF16)


def _decay(small3, b_pad):
    b, t, _ = small3.shape
    return pl.pallas_call(
        _decay_kernel,
        grid=(b,),
        in_specs=[pl.BlockSpec((None, t, SMALL_COLS), lambda i: (i, 0, 0)),
                  pl.BlockSpec((1, LANES), lambda i: (0, 0))],
        out_specs=pl.BlockSpec((None, t, LANES), lambda i: (i, 0, 0)),
        out_shape=jax.ShapeDtypeStruct((b, t, LANES), BF16),
        compiler_params=_cparams(("parallel",)),
        name="fox_decay",
    )(small3, b_pad)


def _fox_kernel(q_ref, k_ref, v_ref, f_ref, z_ref, y_ref, qf_ref, kf_ref, vf_ref, s_ref):
    tq = TQ_FOX
    t_total = z_ref.shape[0]
    slab_idx = pl.program_id(1)
    causal = (lax.broadcasted_iota(jnp.int32, (tq, tq), 0) >= lax.broadcasted_iota(jnp.int32, (tq, tq), 1))
    lane = _lane_iota((tq, LANES))

    src = lax.broadcasted_iota(jnp.int32, (LANES, LANES), 0)
    dst = lax.broadcasted_iota(jnp.int32, (LANES, LANES), 1)
    for r0 in range(0, t_total, tq):
        rs = slice(r0, r0 + tq)
        q = q_ref[rs, :].astype(F32) * (SCALE * LOG2E)
        k = k_ref[rs, :].astype(F32)
        v = v_ref[rs, :].astype(F32)
        for e in range(2):
            own = (lane < HEAD_DIM) if e == 0 else (lane >= HEAD_DIM)
            ext0 = (1 - e) * HEAD_DIM
            head = 2 * slab_idx + e
            place = sum(jnp.where((src == j * FOX_HEADS + head) & (dst == ext0 + j), 1.0, 0.0) for j in range(3))
            k_ext = _dot(f_ref[rs, :], place.astype(BF16))
            qf_ref[e, rs, :] = jnp.where(own, q, jnp.where((lane >= ext0) & (lane < ext0 + 3), 1.0, 0.0)).astype(BF16)
            kf_ref[e, rs, :] = jnp.where(own, k, k_ext).astype(BF16)
            vf_ref[e, rs, :] = jnp.where(own, v, 1.0).astype(BF16)

    nq = t_total // tq
    units = [(i, h) for i in [j // 2 if j % 2 else nq - 1 - j // 2 for j in range(nq)] for h in range(2)]
    row_max, outs = {}, {}

    def score_stage(u):
        i, h = u
        sbuf = s_ref.at[units.index(u) % 2]
        q = qf_ref[h, i * tq:(i + 1) * tq, :]
        mrun = jnp.full((tq, LANES), NEG_INF, F32)
        for c in range(i + 1):
            s = _dot_nt(q, kf_ref[h, c * tq:(c + 1) * tq, :])
            if c == i:
                s = jnp.where(causal, s, NEG_INF)
            sbuf[c] = s
            mrun = _lane_block_max(mrun, s)
            yield
        row_max[u] = _row_max_tiled(mrun, tq // LANES)

    def value_stage(u):
        i, h = u
        sbuf = s_ref.at[units.index(u) % 2]
        m = row_max.pop(u)
        acc = jnp.zeros((tq, LANES), F32)
        for c in range(i + 1):
            p = jnp.exp2(sbuf[c] - m).astype(BF16)
            acc = acc + _dot(p, vf_ref[h, c * tq:(c + 1) * tq, :])
            yield
        outs[u] = acc / pltpu.roll(acc, HEAD_DIM, 1)
        if h == 1:
            slab = jnp.where(lane < HEAD_DIM, outs.pop((i, 0)), outs.pop((i, 1)))
            z = z_ref[i * tq:(i + 1) * tq, :].astype(F32)
            y_ref[i * tq:(i + 1) * tq, :] = (slab * (z * _sigmoid(z))).astype(BF16)

    _emit_pipelined(units, score_stage, value_stage)


def _fox_attn(main3, f3):
    b, t, _ = main3.shape
    nslab = FOX_HEADS // 2

    def col(base):
        return pl.BlockSpec((None, t, LANES), lambda i, p: (i, 0, base // LANES + p))

    return pl.pallas_call(
        _fox_kernel,
        grid=(b, nslab),
        in_specs=[col(COL_QB), col(COL_KB), col(COL_VB),
                  pl.BlockSpec((None, t, LANES), lambda i, p: (i, 0, 0)),
                  col(COL_ZB)],
        out_specs=pl.BlockSpec((None, t, LANES), lambda i, p: (i, 0, p)),
        out_shape=jax.ShapeDtypeStruct((b, t, 1024), BF16),
        scratch_shapes=[pltpu.VMEM((2, t, LANES), BF16), pltpu.VMEM((2, t, LANES), BF16),
                        pltpu.VMEM((2, t, LANES), BF16),
                        pltpu.VMEM((2, t // TQ_FOX, TQ_FOX, TQ_FOX), F32)],
        compiler_params=_cparams(("parallel", "parallel")),
        name="fox_attn",
    )(main3, main3, main3, f3, main3)


def _out_kernel(x_ref, ya_ref, yb_ref, ra_ref, rb_ref, wn_ref, wf_ref, wo_ref, fg_ref, o_ref, *, final):
    ta = _dot(ya_ref[...], wn_ref[...])
    tb = _dot(yb_ref[...], wf_ref[...])
    merged = _sigmoid(ra_ref[...].astype(F32)) * ta + _sigmoid(rb_ref[...].astype(F32)) * tb
    out = x_ref[...] + _dot(merged.astype(BF16), wo_ref[...])
    if final:
        out = out * lax.rsqrt(jnp.mean(out * out, axis=-1, keepdims=True) + RMS_EPS) * fg_ref[...]
    o_ref[...] = out


def _out_proj(x2, ya2, yb2, main2, wn, wf, wo, fg, final, tm=512):
    n = x2.shape[0]
    tm = min(tm, n)
    row = lambda c: pl.BlockSpec((tm, 1024), lambda i: (i, c))
    w = pl.BlockSpec((1024, 1024), lambda i: (0, 0))
    return pl.pallas_call(
        functools.partial(_out_kernel, final=final),
        grid=(n // tm,),
        in_specs=[row(0), row(0), row(0), row(COL_RA // 1024), row(COL_RB // 1024), w, w, w,
                  pl.BlockSpec((1, 1024), lambda i: (0, 0))],
        out_specs=row(0),
        out_shape=jax.ShapeDtypeStruct((n, 1024), F32),
        compiler_params=_cparams(("parallel",)),
        name="out_proj",
    )(x2, ya2, yb2, main2, main2, wn, wf, wo, fg)


def _rope_tables(t):
    inv = ROPE_THETA ** (-jnp.arange(0, HEAD_DIM, 2, dtype=F32) / HEAD_DIM)
    ang = jnp.arange(t, dtype=F32)[:, None] * inv[None, :]
    c, s = jnp.cos(ang), jnp.sin(ang)
    return jnp.concatenate([c, c, c, c], axis=1), jnp.concatenate([-s, s, -s, s], axis=1)


def _overlap_matrix():
    n = np.arange(LANES)[:, None] * CMP_STRIDE
    j = np.arange(LANES)[None, :] * SEL_BLOCK
    m = (n < j + SEL_BLOCK) & (n + CMP_BLOCK > j) & (np.arange(LANES)[None, :] < N_SEL)
    return jnp.asarray(m.astype(np.float32), dtype=BF16)


def _gate_expansion():
    m = np.zeros((3, LANES, NSA_HEADS * HEAD_DIM), np.float32)
    for br in range(3):
        for h in range(NSA_HEADS):
            m[br, br * NSA_HEADS + h, h * HEAD_DIM:(h + 1) * HEAD_DIM] = 1.0
    return jnp.asarray(m, dtype=BF16)


def _reorder_w_in(w):
    qa, kv, ga, za, qb, kb, vb, fb, zb, mg = jnp.split(
        w, np.cumsum([1024, 768, 48, 1024, 1024, 1024, 1024, 16, 1024])[:9].tolist(), axis=1)
    main = jnp.concatenate([qa, za, qb, kb, vb, zb, mg, kv], axis=1).astype(BF16)
    d = w.shape[0]
    small = jnp.concatenate([ga, jnp.zeros((d, LANES - 48), F32), fb, jnp.zeros((d, LANES - 16), F32)],
                            axis=1).astype(BF16)
    return main, small


def _compress_weights(w1, w2):
    eye = jnp.eye(NSA_GROUPS, dtype=F32)
    w1r = w1.reshape(CMP_BLOCK, HEAD_DIM, CMP_HIDDEN)
    half = CMP_BLOCK // 2

    def blockdiag(wl):
        return jnp.einsum('ldh,gk->lgdkh', wl, eye).reshape(half * NSA_GROUPS * HEAD_DIM,
                                                            NSA_GROUPS * CMP_HIDDEN).astype(BF16)

    wa, wb = blockdiag(w1r[:half]), blockdiag(w1r[half:])
    w2d = jnp.einsum('hd,gk,c->gkhcd', w2, eye, jnp.ones((2,), F32)).reshape(
        NSA_GROUPS, NSA_GROUPS * CMP_HIDDEN, LANES).astype(BF16)
    return wa, wb, w2d


def kernel(x, norm_g, w_in, b_forget, cmp_pe_k, cmp_w1_k, cmp_w2_k, cmp_pe_v, cmp_w1_v, cmp_w2_v,
           w_proj_nsa, w_proj_fox, w_out, final_g):
    b, t, d = x.shape
    depth = norm_g.shape[0]
    n = b * t
    nc = t // CMP_STRIDE
    assert d == D_MODEL and nc == LANES and t // SEL_BLOCK == N_SEL
    cos, sin = _rope_tables(t)
    ovl = _overlap_matrix()
    gmat = _gate_expansion()
    fg = final_g.reshape(1, d)
    x2 = x.reshape(n, d)
    for l in range(depth):
        w_main, w_small = _reorder_w_in(w_in[l])
        main2, small2 = _inproj(x2, norm_g[l].reshape(1, d), w_main, w_small)
        main3 = main2.reshape(b, t, MAIN_COLS)
        small3 = small2.reshape(b, t, SMALL_COLS)
        xk = main3[:, :, COL_KV:COL_KV + LANES].reshape(b, nc, CMP_STRIDE * LANES)
        xv = main3[:, :, COL_KV + LANES:COL_KV + 2 * LANES].reshape(b, nc, CMP_STRIDE * LANES)
        wak, wbk, w2k = _compress_weights(cmp_w1_k[l], cmp_w2_k[l])
        wav, wbv, w2v = _compress_weights(cmp_w1_v[l], cmp_w2_v[l])
        pek = jnp.broadcast_to(cmp_pe_k[l].reshape(1, -1), (8, CMP_BLOCK * HEAD_DIM))
        pev = jnp.broadcast_to(cmp_pe_v[l].reshape(1, -1), (8, CMP_BLOCK * HEAD_DIM))
        kc, vc = _compress(xk, xv, pek, pev, cmp_w1_k[l].astype(BF16), cmp_w1_v[l].astype(BF16),
                           wak, wbk, wav, wbv, w2k, w2v)
        qh, ks, kw, vs, vw, ge = _nsaprep(main3, small3, cos, sin, gmat)
        o_cmp, nsel = _cmp_attn(main3, ge, kc, vc, ovl)
        o_cs = _slc_attn(qh, ks, vs, nsel, ge, o_cmp)
        y_a = _win_attn(qh, kw, vw, ge, o_cs, main3)
        b_pad = jnp.concatenate([b_forget[l], jnp.zeros((LANES - FOX_HEADS,), F32)]).reshape(1, LANES)
        y_b = _fox_attn(main3, _decay(small3, b_pad))
        x2 = _out_proj(x2, y_a.reshape(n, 1024), y_b.reshape(n, 1024), main2,
                       w_proj_nsa[l].astype(BF16), w_proj_fox[l].astype(BF16), w_out[l].astype(BF16),
                       fg, final=(l == depth - 1))
    return x2.reshape(b, t, d)
```

```python
import functools

import numpy as np
import jax
import jax.numpy as jnp
from jax import lax
from jax.experimental import pallas as pl
from jax.experimental.pallas import tpu as pltpu

F32 = jnp.float32
BF16 = jnp.bfloat16

D_MODEL = 1024
HEAD_DIM = 64
LANES = 128
NSA_HEADS = 16
NSA_GROUPS = 2
NSA_HPG = NSA_HEADS // NSA_GROUPS
CMP_BLOCK = 32
CMP_STRIDE = 16
CMP_HIDDEN = 2 * HEAD_DIM
SEL_BLOCK = 64
SEL_TOPK = 8
N_SEL = 32
WINDOW = 512
FOX_HEADS = 16
ROPE_THETA = 10000.0
RMS_EPS = 1e-6
NEG_INF = -1e30
FORCED_SCORE = 1e4
SCALE = HEAD_DIM ** -0.5
LOG2E = 1.4426950408889634
EXT = HEAD_DIM

COL_QA, COL_ZA, COL_QB, COL_KB, COL_VB, COL_ZB, COL_RA, COL_RB, COL_KV = (
    0, 1024, 2048, 3072, 4096, 5120, 6144, 7168, 8192)
MAIN_COLS = 8960
SMALL_COLS = 256

TQ_NSA = 128
TK_SLC = 256
TQ_FOX = 256
VMEM_LIMIT = 56 * 1024 * 1024


def _cparams(sem):
    return pltpu.CompilerParams(dimension_semantics=sem, vmem_limit_bytes=VMEM_LIMIT)


def _lane_iota(shape):
    return lax.broadcasted_iota(jnp.int32, shape, len(shape) - 1)


def _dot_nt(a, b):
    return lax.dot_general(a, b, (((1,), (1,)), ((), ())), preferred_element_type=F32)


def _dot(a, b):
    return jnp.dot(a, b, preferred_element_type=F32)


def _split3(x):
    hi = x.astype(BF16).astype(F32)
    r = x - hi
    mid = r.astype(BF16).astype(F32)
    lo = (r - mid).astype(BF16).astype(F32)
    return hi, mid, lo


def _split3_dot(a, m):
    hi, mid, lo = _split3(a)
    return _dot(hi.astype(BF16), m) + _dot(mid.astype(BF16), m) + _dot(lo.astype(BF16), m)


def _sigmoid(x):
    return 1.0 / (1.0 + jnp.exp(-x))


def _column(x, idx):
    return jnp.sum(jnp.where(_lane_iota(x.shape) == idx, x, 0.0), axis=-1, keepdims=True)


def _rope_slab(x, cos, sin):
    lane = _lane_iota(x.shape)
    swapped = jnp.where((lane % HEAD_DIM) < HEAD_DIM // 2,
                        pltpu.roll(x, LANES - HEAD_DIM // 2, 1), pltpu.roll(x, HEAD_DIM // 2, 1))
    return x * cos + swapped * sin


def _stack_heads(slabs):
    lane = _lane_iota(slabs[0].shape)
    parts = []
    for s in slabs:
        parts.append(jnp.where(lane < HEAD_DIM, s, jnp.zeros_like(s)))
        parts.append(jnp.where(lane >= HEAD_DIM, s, jnp.zeros_like(s)))
    return jnp.concatenate(parts, axis=0)


def _unstack_heads(o, nslab, rows):
    lane = _lane_iota((rows, LANES))
    out = []
    for j in range(nslab):
        a = o[(2 * j) * rows:(2 * j + 1) * rows]
        b = o[(2 * j + 1) * rows:(2 * j + 2) * rows]
        out.append(jnp.where(lane < HEAD_DIM, a, b))
    return out


def _inproj_kernel(x_ref, g_ref, w_ref, ws_ref, main_ref, small_ref, xn_ref):
    @pl.when(pl.program_id(1) == 0)
    def _():
        x = x_ref[...]
        y = x * lax.rsqrt(jnp.mean(x * x, axis=-1, keepdims=True) + RMS_EPS) * g_ref[...]
        xn = y.astype(BF16)
        xn_ref[...] = xn
        small_ref[...] = _dot(xn, ws_ref[...])

    main_ref[...] = _dot(xn_ref[...], w_ref[...]).astype(BF16)


def _inproj(x2, g, w_main, w_small, tm=2048, tn=1280):
    n = x2.shape[0]
    tm = min(tm, n)
    return pl.pallas_call(
        _inproj_kernel,
        grid=(n // tm, MAIN_COLS // tn),
        in_specs=[
            pl.BlockSpec((tm, D_MODEL), lambda i, j: (i, 0)),
            pl.BlockSpec((1, D_MODEL), lambda i, j: (0, 0)),
            pl.BlockSpec((D_MODEL, tn), lambda i, j: (0, j)),
            pl.BlockSpec((D_MODEL, SMALL_COLS), lambda i, j: (0, 0)),
        ],
        out_specs=[
            pl.BlockSpec((tm, tn), lambda i, j: (i, j)),
            pl.BlockSpec((tm, SMALL_COLS), lambda i, j: (i, 0)),
        ],
        out_shape=[jax.ShapeDtypeStruct((n, MAIN_COLS), BF16),
                   jax.ShapeDtypeStruct((n, SMALL_COLS), F32)],
        scratch_shapes=[pltpu.VMEM((tm, D_MODEL), BF16)],
        compiler_params=_cparams(("parallel", "arbitrary")),
        name="inproj",
    )(x2, g, w_main, w_small)


def _compress_kernel(xk_ref, xv_ref, pek_ref, pev_ref, w1k_ref, w1v_ref,
                     wak_ref, wbk_ref, wav_ref, wbv_ref, w2k_ref, w2v_ref, kc_ref, vc_ref):
    def one(x_ref, pe_ref, w1_ref, wa_ref, wb_ref, w2_ref, out_ref, out_scale):
        x = x_ref[...]
        a = _dot(x, wa_ref[...])
        b = _dot(x, wb_ref[...])
        nc = a.shape[0]
        b_up = pltpu.roll(b, nc - 1, 0)
        pe_c = _dot(pe_ref[...].astype(BF16), w1_ref[...])[0:1]
        pe_c = jnp.concatenate([pe_c, pe_c], axis=1)
        hid = a + b_up + pe_c
        hid = (hid * _sigmoid(hid)).astype(BF16)
        for g in range(NSA_GROUPS):
            out_ref[g] = (_dot(hid, w2_ref[g]) * out_scale).astype(BF16)

    one(xk_ref, pek_ref, w1k_ref, wak_ref, wbk_ref, w2k_ref, kc_ref, SCALE * LOG2E)
    one(xv_ref, pev_ref, w1v_ref, wav_ref, wbv_ref, w2v_ref, vc_ref, 1.0)


def _compress(xk, xv, pek, pev, w1k, w1v, wak, wbk, wav, wbv, w2k, w2v):
    b, nc, kw = xk.shape
    full = lambda a: pl.BlockSpec(a.shape, lambda i: (0,) * a.ndim)
    xs = pl.BlockSpec((None, nc, kw), lambda i: (i, 0, 0))
    os_ = pl.BlockSpec((None, NSA_GROUPS, nc, LANES), lambda i: (i, 0, 0, 0))
    return pl.pallas_call(
        _compress_kernel,
        grid=(b,),
        in_specs=[xs, xs] + [full(a) for a in (pek, pev, w1k, w1v, wak, wbk, wav, wbv, w2k, w2v)],
        out_specs=[os_, os_],
        out_shape=[jax.ShapeDtypeStruct((b, NSA_GROUPS, nc, LANES), BF16)] * 2,
        compiler_params=_cparams(("parallel",)),
        name="compress",
    )(xk, xv, pek, pev, w1k, w1v, wak, wbk, wav, wbv, w2k, w2v)


def _nsaprep_kernel(q_ref, s_ref, w_ref, sm_ref, cos_ref, sin_ref, gmat_ref,
                    qh_ref, ks_ref, kw_ref, vs_ref, vw_ref, ge_ref, *, tt):
    cos = cos_ref[...]
    sin = sin_ref[...]
    lane = _lane_iota((tt, LANES))
    t = pl.program_id(1) * tt + lax.broadcasted_iota(jnp.int32, (tt, LANES), 0)
    low = lane < HEAD_DIM
    for j in range(NSA_HEADS // 2):
        x = _rope_slab(q_ref[:, j * LANES:(j + 1) * LANES].astype(F32), cos, sin) * (SCALE * LOG2E)
        qh_ref[2 * j] = jnp.where(low, x, 0.0).astype(BF16)
        qh_ref[2 * j + 1] = jnp.where(low, pltpu.roll(x, HEAD_DIM, 1), 0.0).astype(BF16)

    onehot = jnp.where((lane >= EXT) & (lane - EXT == t // SEL_BLOCK) & (lane < EXT + N_SEL), 1.0, 0.0)
    ks = _rope_slab(s_ref[:, 0:LANES].astype(F32), cos, sin)
    kw = _rope_slab(w_ref[:, 0:LANES].astype(F32), cos, sin)
    for g in range(NSA_GROUPS):
        ks_ref[g] = jnp.where(low, ks if g == 0 else pltpu.roll(ks, HEAD_DIM, 1), onehot).astype(BF16)
        kw_ref[g] = jnp.where(low, kw if g == 0 else pltpu.roll(kw, HEAD_DIM, 1), 0.0).astype(BF16)

    for src_ref, dst_ref in ((s_ref, vs_ref), (w_ref, vw_ref)):
        v = src_ref[:, LANES:2 * LANES].astype(F32)
        v_sw = pltpu.roll(v, HEAD_DIM, 1)
        dst_ref[0, 0] = jnp.where(low, v, 1.0).astype(BF16)
        dst_ref[0, 1] = jnp.where(low, 1.0, v_sw).astype(BF16)
        dst_ref[1, 0] = jnp.where(low, v_sw, 1.0).astype(BF16)
        dst_ref[1, 1] = jnp.where(low, 1.0, v).astype(BF16)
    gates = _sigmoid(sm_ref[:, 0:LANES]).astype(BF16)
    for br in range(3):
        ge_ref[br] = _dot(gates, gmat_ref[br]).astype(BF16)


def _nsaprep(main3, small3, cos, sin, gmat, tt=512):
    b, t, _ = main3.shape
    blk = COL_KV // (2 * LANES)
    kspec = pl.BlockSpec((None, NSA_GROUPS, tt, LANES), lambda i, j: (i, 0, j, 0))
    vspec = pl.BlockSpec((None, NSA_GROUPS, 2, tt, LANES), lambda i, j: (i, 0, 0, j, 0))
    return pl.pallas_call(
        functools.partial(_nsaprep_kernel, tt=tt),
        grid=(b, t // tt),
        in_specs=[
            pl.BlockSpec((None, tt, 1024), lambda i, j: (i, j, COL_QA // 1024)),
            pl.BlockSpec((None, tt, 2 * LANES), lambda i, j: (i, j, blk + 1)),
            pl.BlockSpec((None, tt, 2 * LANES), lambda i, j: (i, j, blk + 2)),
            pl.BlockSpec((None, tt, SMALL_COLS), lambda i, j: (i, j, 0)),
            pl.BlockSpec((tt, LANES), lambda i, j: (j, 0)),
            pl.BlockSpec((tt, LANES), lambda i, j: (j, 0)),
            pl.BlockSpec(gmat.shape, lambda i, j: (0, 0, 0)),
        ],
        out_specs=[
            pl.BlockSpec((None, NSA_HEADS, tt, LANES), lambda i, j: (i, 0, j, 0)),
            kspec, kspec, vspec, vspec,
            pl.BlockSpec((None, 3, tt, 1024), lambda i, j: (i, 0, j, 0)),
        ],
        out_shape=[jax.ShapeDtypeStruct((b, NSA_HEADS, t, LANES), BF16)]
        + [jax.ShapeDtypeStruct((b, NSA_GROUPS, t, LANES), BF16)] * 2
        + [jax.ShapeDtypeStruct((b, NSA_GROUPS, 2, t, LANES), BF16)] * 2
        + [jax.ShapeDtypeStruct((b, 3, t, 1024), BF16)],
        compiler_params=_cparams(("parallel", "parallel")),
        name="nsa_prep",
    )(main3, main3, main3, small3, cos, sin, gmat)


def _cmp_kernel(q_ref, kc_ref, vc_ref, ovl_ref, ge_ref, o_ref, nsel_ref, *, tq):
    q0 = pl.program_id(1) * tq
    nc = kc_ref.shape[1]
    lane = _lane_iota((tq, LANES))
    t = q0 + lax.broadcasted_iota(jnp.int32, (tq, LANES), 0)
    cmp_valid = (lane * CMP_STRIDE + CMP_BLOCK - 1 <= t) & (lane < nc)
    row_valid = (t >= CMP_BLOCK - 1)[None]
    cur = t // SEL_BLOCK
    forced = (lane == 0) | (lane == cur) | (lane == cur - 1)
    blk_valid = lane * SEL_BLOCK <= t
    row8 = lax.broadcasted_iota(jnp.int32, (8, tq), 0)
    for g in range(NSA_GROUPS):
        qs = _stack_heads([q_ref[:, (g * 4 + j) * LANES:(g * 4 + j + 1) * LANES] for j in range(4)])
        s3 = _dot_nt(qs, kc_ref[g]).reshape(NSA_HPG, tq, nc)
        s3 = jnp.where(cmp_valid[None], s3, NEG_INF)
        m = jnp.max(s3, axis=-1, keepdims=True)
        e = jnp.exp2(s3 - m)
        l = _dot(e.reshape(NSA_HPG * tq, nc).astype(BF16), jnp.ones((nc, LANES), BF16))
        inv = jnp.where(row_valid, 1.0 / l.reshape(NSA_HPG, tq, LANES), 0.0)
        p = (e * inv).astype(BF16)
        o = _dot(p.reshape(NSA_HPG * tq, nc), vc_ref[g])
        for j, slab in enumerate(_unstack_heads(o, 4, tq)):
            cols = slice((g * 4 + j) * LANES, (g * 4 + j + 1) * LANES)
            o_ref[:, cols] = slab * ge_ref[:, cols].astype(F32)
        imp = _split3_dot(jnp.sum(p.astype(F32), axis=0), ovl_ref[...])
        score = jnp.where(forced, FORCED_SCORE, jnp.where(blk_valid, imp, -1.0))
        score = jnp.where(lane < N_SEL, score, -2.0)
        st = score.T[0:N_SEL, :]
        cnt = [jnp.zeros((8, tq), F32) for _ in range(N_SEL // 8)]
        for j in range(N_SEL):
            rj = st[j:j + 1, :]
            for r in range(N_SEL // 8):
                blk = st[8 * r:8 * r + 8, :]
                if j < 8 * r:
                    beats = rj >= blk
                elif j >= 8 * r + 8:
                    beats = rj > blk
                else:
                    beats = (rj > blk) | ((rj == blk) & (row8 + 8 * r > j))
                cnt[r] = cnt[r] + jnp.where(beats, 1.0, 0.0)
        nsel_t = jnp.where(jnp.concatenate(cnt, axis=0) < float(SEL_TOPK), 0.0, NEG_INF)
        nsel_t = jnp.concatenate([jnp.zeros((EXT, tq), F32), nsel_t,
                                  jnp.zeros((LANES - EXT - N_SEL, tq), F32)], axis=0)
        nsel_ref[g] = nsel_t.T.astype(BF16)


def _cmp_attn(main3, ge, kc, vc, ovl, tq=256):
    b, t, _ = main3.shape
    nc = kc.shape[2]
    return pl.pallas_call(
        functools.partial(_cmp_kernel, tq=tq),
        grid=(b, t // tq),
        in_specs=[
            pl.BlockSpec((None, tq, 1024), lambda i, j: (i, j, COL_QA // 1024)),
            pl.BlockSpec((None, NSA_GROUPS, nc, LANES), lambda i, j: (i, 0, 0, 0)),
            pl.BlockSpec((None, NSA_GROUPS, nc, LANES), lambda i, j: (i, 0, 0, 0)),
            pl.BlockSpec(ovl.shape, lambda i, j: (0, 0)),
            pl.BlockSpec((None, None, tq, 1024), lambda i, j: (i, 0, j, 0)),
        ],
        out_specs=[
            pl.BlockSpec((None, tq, 1024), lambda i, j: (i, j, 0)),
            pl.BlockSpec((None, NSA_GROUPS, tq, LANES), lambda i, j: (i, 0, j, 0)),
        ],
        out_shape=[jax.ShapeDtypeStruct((b, t, 1024), F32),
                   jax.ShapeDtypeStruct((b, NSA_GROUPS, t, LANES), BF16)],
        compiler_params=_cparams(("parallel", "parallel")),
        name="cmp_attn",
    )(main3, kc, vc, ovl, ge)


def _tile_lanes(x, n):
    return jnp.concatenate([x] * n, axis=1)


def _lane_block_max(mrun, s):
    for w in range(s.shape[1] // LANES):
        mrun = jnp.maximum(mrun, s[:, w * LANES:(w + 1) * LANES])
    return mrun


def _row_max_tiled(mrun, n):
    return _tile_lanes(jnp.broadcast_to(jnp.max(mrun, axis=1, keepdims=True), mrun.shape), n)


def _emit_pipelined(units, score_stage, value_stage):
    def drain(gens):
        gens = [g for g in gens if g is not None]
        while gens:
            for g in list(gens):
                try:
                    next(g)
                except StopIteration:
                    gens.remove(g)

    drain([score_stage(units[0])])
    for n, u in enumerate(units):
        drain([score_stage(units[n + 1]) if n + 1 < len(units) else None, value_stage(u)])


EVEN_ODD = tuple(range(0, NSA_HPG, 2)) + tuple(range(1, NSA_HPG, 2))


def _group_slabs(acc_even, acc_odd, gates, tq):
    low = _lane_iota((tq, LANES)) < HEAD_DIM
    slabs = []
    for j in range(NSA_HPG // 2):
        e = acc_even[j * tq:(j + 1) * tq]
        o = acc_odd[j * tq:(j + 1) * tq]
        den = pltpu.roll(jnp.where(low, o, e), HEAD_DIM, 1)
        slabs.append(jnp.where(low, e, o) / den * gates[:, j * LANES:(j + 1) * LANES].astype(F32))
    return slabs


def _slc_kernel(qh_ref, k_ref, v_ref, nsel_ref, ge_ref, prev_ref, o_ref, s_ref):
    tq, tk = TQ_NSA, TK_SLC
    rows = NSA_HPG * tq
    t_total = k_ref.shape[0]
    qry_l = lax.broadcasted_iota(jnp.int32, (tq, tk), 0)
    key_l = lax.broadcasted_iota(jnp.int32, (tq, tk), 1)

    nq = t_total // tq
    order = [j // 2 if j % 2 else nq - 1 - j // 2 for j in range(nq)]
    row_max = {}

    def score_stage(i):
        q0 = i * tq
        sbuf = s_ref.at[order.index(i) % 2]
        nsel = nsel_ref[q0:q0 + tq, :]
        qs = jnp.concatenate([qh_ref[h, q0:q0 + tq, :] + nsel for h in EVEN_ODD], axis=0)
        c_last = q0 // tk
        mrun = jnp.full((rows, LANES), NEG_INF, F32)
        for c in range(c_last + 1):
            s = _dot_nt(qs, k_ref[c * tk:(c + 1) * tk, :])
            if c == c_last:
                causal = jnp.where(c * tk + key_l <= q0 + qry_l, 0.0, NEG_INF)
                s = (s.reshape(NSA_HPG, tq, tk) + causal[None]).reshape(rows, tk)
            sbuf[c] = s
            mrun = _lane_block_max(mrun, s)
            yield
        row_max[i] = _row_max_tiled(mrun, tk // LANES)

    def value_stage(i):
        q0 = i * tq
        sbuf = s_ref.at[order.index(i) % 2]
        m = row_max.pop(i)
        acc = [jnp.zeros((rows // 2, LANES), F32)] * 2
        for c in range(q0 // tk + 1):
            p = jnp.exp2(sbuf[c] - m).astype(BF16)
            for e in range(2):
                acc[e] = acc[e] + _dot(p[e * rows // 2:(e + 1) * rows // 2], v_ref[e, c * tk:(c + 1) * tk, :])
            yield
        for j, slab in enumerate(_group_slabs(acc[0], acc[1], ge_ref[q0:q0 + tq, :], tq)):
            cols = slice(j * LANES, (j + 1) * LANES)
            o_ref[q0:q0 + tq, cols] = prev_ref[q0:q0 + tq, cols] + slab

    _emit_pipelined(order, score_stage, value_stage)


def _slc_attn(qh, ks, vs, nsel, ge, prev):
    b, _, t, _ = qh.shape
    gw = NSA_HPG * HEAD_DIM
    kv = pl.BlockSpec((None, None, t, LANES), lambda i, g: (i, g, 0, 0))
    return pl.pallas_call(
        _slc_kernel,
        grid=(b, NSA_GROUPS),
        in_specs=[
            pl.BlockSpec((None, NSA_HPG, t, LANES), lambda i, g: (i, g, 0, 0), pipeline_mode=pl.Buffered(1)),
            kv,
            pl.BlockSpec((None, None, 2, t, LANES), lambda i, g: (i, g, 0, 0, 0)),
            kv,
            pl.BlockSpec((None, None, t, gw), lambda i, g: (i, 1, 0, g)),
            pl.BlockSpec((None, t, gw), lambda i, g: (i, 0, g), pipeline_mode=pl.Buffered(1)),
        ],
        out_specs=pl.BlockSpec((None, t, gw), lambda i, g: (i, 0, g)),
        out_shape=jax.ShapeDtypeStruct((b, t, 1024), F32),
        scratch_shapes=[pltpu.VMEM((2, t // TK_SLC, NSA_HPG * TQ_NSA, TK_SLC), F32)],
        compiler_params=_cparams(("parallel", "parallel")),
        name="slc_attn",
    )(qh, ks, vs, nsel, ge, prev)


def _win_kernel(qh_ref, k_ref, v_ref, ge_ref, prev_ref, z_ref, y_ref, s_ref):
    tq = TQ_NSA
    rows = NSA_HPG * tq
    t_total = k_ref.shape[0]
    qry_l = lax.broadcasted_iota(jnp.int32, (tq, LANES), 0)
    key_l = lax.broadcasted_iota(jnp.int32, (tq, LANES), 1)

    step = 2 * LANES
    row_max = {}

    def window(i):
        k_lo = max(i * tq - WINDOW, 0)
        return k_lo, [(c, min(step, i * tq + tq - k_lo - c)) for c in range(0, i * tq + tq - k_lo, step)]

    def score_stage(i):
        q0 = i * tq
        k_lo, pieces = window(i)
        sbuf = s_ref.at[i % 2]
        qs = jnp.concatenate([qh_ref[h, q0:q0 + tq, :] for h in EVEN_ODD], axis=0)
        mrun = jnp.full((rows, LANES), NEG_INF, F32)
        for c0, cw in pieces:
            s = _dot_nt(qs, k_ref[k_lo + c0:k_lo + c0 + cw, :])
            for w in range(cw // LANES):
                k0 = k_lo + c0 + w * LANES
                blk = s[:, w * LANES:(w + 1) * LANES]
                if k0 == q0 or k0 == q0 - WINDOW:
                    dist = (k0 - q0) + key_l - qry_l
                    band = jnp.where((dist <= 0) & (dist > -WINDOW), 0.0, NEG_INF)
                    blk = (blk.reshape(NSA_HPG, tq, LANES) + band[None]).reshape(rows, LANES)
                sbuf[:, c0 + w * LANES:c0 + (w + 1) * LANES] = blk
                mrun = jnp.maximum(mrun, blk)
            yield
        row_max[i] = jnp.broadcast_to(jnp.max(mrun, axis=1, keepdims=True), mrun.shape)

    def value_stage(i):
        q0 = i * tq
        k_lo, pieces = window(i)
        sbuf = s_ref.at[i % 2]
        m = row_max.pop(i)
        acc = [jnp.zeros((rows // 2, LANES), F32)] * 2
        for c0, cw in pieces:
            p = jnp.exp2(sbuf[:, c0:c0 + cw] - _tile_lanes(m, cw // LANES)).astype(BF16)
            for e in range(2):
                acc[e] = acc[e] + _dot(p[e * rows // 2:(e + 1) * rows // 2], v_ref[e, k_lo + c0:k_lo + c0 + cw, :])
            yield
        for j, slab in enumerate(_group_slabs(acc[0], acc[1], ge_ref[q0:q0 + tq, :], tq)):
            cols = slice(j * LANES, (j + 1) * LANES)
            z = z_ref[q0:q0 + tq, cols].astype(F32)
            y = (prev_ref[q0:q0 + tq, cols] + slab) * (z * _sigmoid(z))
            y_ref[q0:q0 + tq, cols] = y.astype(BF16)

    _emit_pipelined(list(range(t_total // tq)), score_stage, value_stage)


def _win_attn(qh, kw, vw, ge, prev, main3):
    b, _, t, _ = qh.shape
    gw = NSA_HPG * HEAD_DIM
    kv = pl.BlockSpec((None, None, t, LANES), lambda i, g: (i, g, 0, 0))
    return pl.pallas_call(
        _win_kernel,
        grid=(b, NSA_GROUPS),
        in_specs=[
            pl.BlockSpec((None, NSA_HPG, t, LANES), lambda i, g: (i, g, 0, 0)),
            kv,
            pl.BlockSpec((None, None, 2, t, LANES), lambda i, g: (i, g, 0, 0, 0)),
            pl.BlockSpec((None, None, t, gw), lambda i, g: (i, 2, 0, g)),
            pl.BlockSpec((None, t, gw), lambda i, g: (i, 0, g)),
            pl.BlockSpec((None, t, gw), lambda i, g: (i, 0, COL_ZA // gw + g)),
        ],
        out_specs=pl.BlockSpec((None, t, gw), lambda i, g: (i, 0, g)),
        out_shape=jax.ShapeDtypeStruct((b, t, 1024), BF16),
        scratch_shapes=[pltpu.VMEM((2, NSA_HPG * TQ_NSA, WINDOW + TQ_NSA), F32)],
        compiler_params=_cparams(("parallel", "parallel")),
        name="win_attn",
    )(qh, kw, vw, ge, prev, main3)


def _decay_kernel(sm_ref, b_ref, f_ref):
    t_total = sm_ref.shape[0]
    x = sm_ref[:, LANES:2 * LANES] + b_ref[...]
    f = jnp.minimum(x, 0.0) - jnp.log1p(jnp.exp(-jnp.abs(x)))
    row = lax.broadcasted_iota(jnp.int32, f.shape, 0)
    sh = 1
    while sh < t_total:
        f = f + jnp.where(row >= sh, pltpu.roll(f, sh, 0), 0.0)
        sh *= 2
    hi, mid, lo = _split3(f * (-LOG2E))
    lane = _lane_iota(f.shape)
    pieces = jnp.where(lane < FOX_HEADS, hi,
                       jnp.where(lane < 2 * FOX_HEADS, pltpu.roll(mid, FOX_HEADS, 1),
                                 jnp.where(lane < 3 * FOX_HEADS, pltpu.roll(lo, 2 * FOX_HEADS, 1), 0.0)))
    f_ref[...] = pieces.astype(BF16)


def _decay(small3, b_pad):
    b, t, _ = small3.shape
    return pl.pallas_call(
        _decay_kernel,
        grid=(b,),
        in_specs=[pl.BlockSpec((None, t, SMALL_COLS), lambda i: (i, 0, 0)),
                  pl.BlockSpec((1, LANES), lambda i: (0, 0))],
        out_specs=pl.BlockSpec((None, t, LANES), lambda i: (i, 0, 0)),
        out_shape=jax.ShapeDtypeStruct((b, t, LANES), BF16),
        compiler_params=_cparams(("parallel",)),
        name="fox_decay",
    )(small3, b_pad)


def _fox_kernel(q_ref, k_ref, v_ref, f_ref, z_ref, y_ref, qf_ref, kf_ref, vf_ref, s_ref):
    tq = TQ_FOX
    t_total = z_ref.shape[0]
    slab_idx = pl.program_id(1)
    causal = (lax.broadcasted_iota(jnp.int32, (tq, tq), 0) >= lax.broadcasted_iota(jnp.int32, (tq, tq), 1))
    lane = _lane_iota((tq, LANES))

    src = lax.broadcasted_iota(jnp.int32, (LANES, LANES), 0)
    dst = lax.broadcasted_iota(jnp.int32, (LANES, LANES), 1)
    for e in range(2):
        own = (lane < HEAD_DIM) if e == 0 else (lane >= HEAD_DIM)
        ext0 = (1 - e) * HEAD_DIM
        head = 2 * slab_idx + e
        place = sum(jnp.where((src == j * FOX_HEADS + head) & (dst == ext0 + j), 1.0, 0.0)
                    for j in range(3)).astype(BF16)
        q_ext = jnp.where((lane >= ext0) & (lane < ext0 + 3), 1.0, 0.0).astype(BF16)
        for r0 in range(0, t_total, tq):
            rs = slice(r0, r0 + tq)
            k_ext = _dot(f_ref[rs, :], place).astype(BF16)
            qf_ref[e, rs, :] = jnp.where(own, q_ref[rs, :], q_ext)
            kf_ref[e, rs, :] = jnp.where(own, k_ref[rs, :], k_ext)
            vf_ref[e, rs, :] = jnp.where(own, v_ref[rs, :], jnp.ones((tq, LANES), BF16))

    nq = t_total // tq
    units = [(i, h) for i in [j // 2 if j % 2 else nq - 1 - j // 2 for j in range(nq)] for h in range(2)]
    row_max, outs = {}, {}

    def score_stage(u):
        i, h = u
        sbuf = s_ref.at[units.index(u) % 2]
        q = qf_ref[h, i * tq:(i + 1) * tq, :]
        mrun = jnp.full((tq, LANES), NEG_INF, F32)
        for c in range(i + 1):
            s = _dot_nt(q, kf_ref[h, c * tq:(c + 1) * tq, :])
            if c == i:
                s = jnp.where(causal, s, NEG_INF)
            sbuf[c] = s
            mrun = _lane_block_max(mrun, s)
            yield
        row_max[u] = _row_max_tiled(mrun, tq // LANES)

    def value_stage(u):
        i, h = u
        sbuf = s_ref.at[units.index(u) % 2]
        m = row_max.pop(u)
        acc = jnp.zeros((tq, LANES), F32)
        for c in range(i + 1):
            p = jnp.exp2(sbuf[c] - m).astype(BF16)
            acc = acc + _dot(p, vf_ref[h, c * tq:(c + 1) * tq, :])
            yield
        outs[u] = acc
        if h == 1:
            low = lane < HEAD_DIM
            a0, a1 = outs.pop((i, 0)), outs.pop((i, 1))
            slab = jnp.where(low, a0, a1) / pltpu.roll(jnp.where(low, a1, a0), HEAD_DIM, 1)
            z = z_ref[i * tq:(i + 1) * tq, :].astype(F32)
            y_ref[i * tq:(i + 1) * tq, :] = (slab * (z * _sigmoid(z))).astype(BF16)

    _emit_pipelined(units, score_stage, value_stage)


def _fox_attn(main3, f3):
    b, t, _ = main3.shape
    nslab = FOX_HEADS // 2

    def col(base):
        return pl.BlockSpec((None, t, LANES), lambda i, p: (i, 0, base // LANES + p))

    return pl.pallas_call(
        _fox_kernel,
        grid=(b, nslab),
        in_specs=[col(COL_QB), col(COL_KB), col(COL_VB),
                  pl.BlockSpec((None, t, LANES), lambda i, p: (i, 0, 0)),
                  col(COL_ZB)],
        out_specs=pl.BlockSpec((None, t, LANES), lambda i, p: (i, 0, p)),
        out_shape=jax.ShapeDtypeStruct((b, t, 1024), BF16),
        scratch_shapes=[pltpu.VMEM((2, t, LANES), BF16), pltpu.VMEM((2, t, LANES), BF16),
                        pltpu.VMEM((2, t, LANES), BF16),
                        pltpu.VMEM((2, t // TQ_FOX, TQ_FOX, TQ_FOX), F32)],
        compiler_params=_cparams(("parallel", "parallel")),
        name="fox_attn",
    )(main3, main3, main3, f3, main3)


def _out_kernel(x_ref, ya_ref, yb_ref, ra_ref, rb_ref, wn_ref, wf_ref, wo_ref, fg_ref, o_ref, *, final):
    ta = _dot(ya_ref[...], wn_ref[...])
    tb = _dot(yb_ref[...], wf_ref[...])
    merged = _sigmoid(ra_ref[...].astype(F32)) * ta + _sigmoid(rb_ref[...].astype(F32)) * tb
    out = x_ref[...] + _dot(merged.astype(BF16), wo_ref[...])
    if final:
        out = out * lax.rsqrt(jnp.mean(out * out, axis=-1, keepdims=True) + RMS_EPS) * fg_ref[...]
    o_ref[...] = out


def _out_proj(x2, ya2, yb2, main2, wn, wf, wo, fg, final, tm=512):
    n = x2.shape[0]
    tm = min(tm, n)
    row = lambda c: pl.BlockSpec((tm, 1024), lambda i: (i, c))
    w = pl.BlockSpec((1024, 1024), lambda i: (0, 0))
    return pl.pallas_call(
        functools.partial(_out_kernel, final=final),
        grid=(n // tm,),
        in_specs=[row(0), row(0), row(0), row(COL_RA // 1024), row(COL_RB // 1024), w, w, w,
                  pl.BlockSpec((1, 1024), lambda i: (0, 0))],
        out_specs=row(0),
        out_shape=jax.ShapeDtypeStruct((n, 1024), F32),
        compiler_params=_cparams(("parallel",)),
        name="out_proj",
    )(x2, ya2, yb2, main2, main2, wn, wf, wo, fg)


def _rope_tables(t):
    inv = ROPE_THETA ** (-jnp.arange(0, HEAD_DIM, 2, dtype=F32) / HEAD_DIM)
    ang = jnp.arange(t, dtype=F32)[:, None] * inv[None, :]
    c, s = jnp.cos(ang), jnp.sin(ang)
    return jnp.concatenate([c, c, c, c], axis=1), jnp.concatenate([-s, s, -s, s], axis=1)


def _overlap_matrix():
    n = np.arange(LANES)[:, None] * CMP_STRIDE
    j = np.arange(LANES)[None, :] * SEL_BLOCK
    m = (n < j + SEL_BLOCK) & (n + CMP_BLOCK > j) & (np.arange(LANES)[None, :] < N_SEL)
    return jnp.asarray(m.astype(np.float32), dtype=BF16)


def _gate_expansion():
    m = np.zeros((3, LANES, NSA_HEADS * HEAD_DIM), np.float32)
    for br in range(3):
        for h in range(NSA_HEADS):
            m[br, br * NSA_HEADS + h, h * HEAD_DIM:(h + 1) * HEAD_DIM] = 1.0
    return jnp.asarray(m, dtype=BF16)


def _reorder_w_in(w):
    qa, kv, ga, za, qb, kb, vb, fb, zb, mg = jnp.split(
        w, np.cumsum([1024, 768, 48, 1024, 1024, 1024, 1024, 16, 1024])[:9].tolist(), axis=1)
    main = jnp.concatenate([qa, za, qb * (SCALE * LOG2E), kb, vb, zb, mg, kv], axis=1).astype(BF16)
    d = w.shape[0]
    small = jnp.concatenate([ga, jnp.zeros((d, LANES - 48), F32), fb, jnp.zeros((d, LANES - 16), F32)],
                            axis=1).astype(BF16)
    return main, small


def _compress_weights(w1, w2):
    eye = jnp.eye(NSA_GROUPS, dtype=F32)
    w1r = w1.reshape(CMP_BLOCK, HEAD_DIM, CMP_HIDDEN)
    half = CMP_BLOCK // 2

    def blockdiag(wl):
        return jnp.einsum('ldh,gk->lgdkh', wl, eye).reshape(half * NSA_GROUPS * HEAD_DIM,
                                                            NSA_GROUPS * CMP_HIDDEN).astype(BF16)

    wa, wb = blockdiag(w1r[:half]), blockdiag(w1r[half:])
    w2d = jnp.einsum('hd,gk,c->gkhcd', w2, eye, jnp.ones((2,), F32)).reshape(
        NSA_GROUPS, NSA_GROUPS * CMP_HIDDEN, LANES).astype(BF16)
    return wa, wb, w2d


def kernel(x, norm_g, w_in, b_forget, cmp_pe_k, cmp_w1_k, cmp_w2_k, cmp_pe_v, cmp_w1_v, cmp_w2_v,
           w_proj_nsa, w_proj_fox, w_out, final_g):
    b, t, d = x.shape
    depth = norm_g.shape[0]
    n = b * t
    nc = t // CMP_STRIDE
    assert d == D_MODEL and nc == LANES and t // SEL_BLOCK == N_SEL
    cos, sin = _rope_tables(t)
    ovl = _overlap_matrix()
    gmat = _gate_expansion()
    fg = final_g.reshape(1, d)
    x2 = x.reshape(n, d)
    for l in range(depth):
        w_main, w_small = _reorder_w_in(w_in[l])
        main2, small2 = _inproj(x2, norm_g[l].reshape(1, d), w_main, w_small)
        main3 = main2.reshape(b, t, MAIN_COLS)
        small3 = small2.reshape(b, t, SMALL_COLS)
        xk = main3[:, :, COL_KV:COL_KV + LANES].reshape(b, nc, CMP_STRIDE * LANES)
        xv = main3[:, :, COL_KV + LANES:COL_KV + 2 * LANES].reshape(b, nc, CMP_STRIDE * LANES)
        wak, wbk, w2k = _compress_weights(cmp_w1_k[l], cmp_w2_k[l])
        wav, wbv, w2v = _compress_weights(cmp_w1_v[l], cmp_w2_v[l])
        pek = jnp.broadcast_to(cmp_pe_k[l].reshape(1, -1), (8, CMP_BLOCK * HEAD_DIM))
        pev = jnp.broadcast_to(cmp_pe_v[l].reshape(1, -1), (8, CMP_BLOCK * HEAD_DIM))
        kc, vc = _compress(xk, xv, pek, pev, cmp_w1_k[l].astype(BF16), cmp_w1_v[l].astype(BF16),
                           wak, wbk, wav, wbv, w2k, w2v)
        qh, ks, kw, vs, vw, ge = _nsaprep(main3, small3, cos, sin, gmat)
        o_cmp, nsel = _cmp_attn(main3, ge, kc, vc, ovl)
        o_cs = _slc_attn(qh, ks, vs, nsel, ge, o_cmp)
        y_a = _win_attn(qh, kw, vw, ge, o_cs, main3)
        b_pad = jnp.concatenate([b_forget[l], jnp.zeros((LANES - FOX_HEADS,), F32)]).reshape(1, LANES)
        y_b = _fox_attn(main3, _decay(small3, b_pad))
        x2 = _out_proj(x2, y_a.reshape(n, 1024), y_b.reshape(n, 1024), main2,
                       w_proj_nsa[l].astype(BF16), w_proj_fox[l].astype(BF16), w_out[l].astype(BF16),
                       fg, final=(l == depth - 1))
    return x2.reshape(b, t, d)
```

```python
import functools

import numpy as np
import jax
import jax.numpy as jnp
from jax import lax
from jax.experimental import pallas as pl
from jax.experimental.pallas import tpu as pltpu

F32 = jnp.float32
BF16 = jnp.bfloat16

D_MODEL = 1024
HEAD_DIM = 64
LANES = 128
NSA_HEADS = 16
NSA_GROUPS = 2
NSA_HPG = NSA_HEADS // NSA_GROUPS
CMP_BLOCK = 32
CMP_STRIDE = 16
CMP_HIDDEN = 2 * HEAD_DIM
SEL_BLOCK = 64
SEL_TOPK = 8
N_SEL = 32
WINDOW = 512
FOX_HEADS = 16
ROPE_THETA = 10000.0
RMS_EPS = 1e-6
NEG_INF = -1e30
FORCED_SCORE = 1e4
SCALE = HEAD_DIM ** -0.5
LOG2E = 1.4426950408889634
EXT = HEAD_DIM

COL_QA, COL_ZA, COL_QB, COL_KB, COL_VB, COL_ZB, COL_RA, COL_RB, COL_KV = (
    0, 1024, 2048, 3072, 4096, 5120, 6144, 7168, 8192)
MAIN_COLS = 8960
SMALL_COLS = 256

TQ_NSA = 128
TK_SLC = 256
TQ_FOX = 256
VMEM_LIMIT = 56 * 1024 * 1024


def _cparams(sem):
    return pltpu.CompilerParams(dimension_semantics=sem, vmem_limit_bytes=VMEM_LIMIT)


def _lane_iota(shape):
    return lax.broadcasted_iota(jnp.int32, shape, len(shape) - 1)


def _dot_nt(a, b):
    return lax.dot_general(a, b, (((1,), (1,)), ((), ())), preferred_element_type=F32)


def _dot(a, b):
    return jnp.dot(a, b, preferred_element_type=F32)


def _split3(x):
    hi = x.astype(BF16).astype(F32)
    r = x - hi
    mid = r.astype(BF16).astype(F32)
    lo = (r - mid).astype(BF16).astype(F32)
    return hi, mid, lo


def _split3_dot(a, m):
    hi, mid, lo = _split3(a)
    return _dot(hi.astype(BF16), m) + _dot(mid.astype(BF16), m) + _dot(lo.astype(BF16), m)


def _sigmoid(x):
    return 1.0 / (1.0 + jnp.exp(-x))


def _column(x, idx):
    return jnp.sum(jnp.where(_lane_iota(x.shape) == idx, x, 0.0), axis=-1, keepdims=True)


def _rope_slab(x, cos, sin):
    lane = _lane_iota(x.shape)
    swapped = jnp.where((lane % HEAD_DIM) < HEAD_DIM // 2,
                        pltpu.roll(x, LANES - HEAD_DIM // 2, 1), pltpu.roll(x, HEAD_DIM // 2, 1))
    return x * cos + swapped * sin


def _stack_heads(slabs):
    lane = _lane_iota(slabs[0].shape)
    parts = []
    for s in slabs:
        parts.append(jnp.where(lane < HEAD_DIM, s, jnp.zeros_like(s)))
        parts.append(jnp.where(lane >= HEAD_DIM, s, jnp.zeros_like(s)))
    return jnp.concatenate(parts, axis=0)


def _unstack_heads(o, nslab, rows):
    lane = _lane_iota((rows, LANES))
    out = []
    for j in range(nslab):
        a = o[(2 * j) * rows:(2 * j + 1) * rows]
        b = o[(2 * j + 1) * rows:(2 * j + 2) * rows]
        out.append(jnp.where(lane < HEAD_DIM, a, b))
    return out


def _inproj_kernel(x_ref, g_ref, w_ref, ws_ref, main_ref, small_ref, kc_ref, vc_ref, xn_ref, *, kv_off):
    @pl.when(pl.program_id(1) == 0)
    def _():
        x = x_ref[...]
        y = x * lax.rsqrt(jnp.mean(x * x, axis=-1, keepdims=True) + RMS_EPS) * g_ref[...]
        xn = y.astype(BF16)
        xn_ref[...] = xn
        small_ref[...] = _dot(xn, ws_ref[...])

    res = _dot(xn_ref[...], w_ref[...]).astype(BF16)
    main_ref[...] = res

    @pl.when(pl.program_id(1) == pl.num_programs(1) - 1)
    def _():
        kc_ref[...] = res[:, kv_off:kv_off + LANES]
        vc_ref[...] = res[:, kv_off + LANES:kv_off + 2 * LANES]


def _inproj(x2, g, w_main, w_small, tm=2048, tn=1280):
    n = x2.shape[0]
    tm = min(tm, n)
    kv_off = COL_KV - (MAIN_COLS - tn)
    assert 0 <= kv_off and kv_off + 2 * LANES <= tn
    return pl.pallas_call(
        functools.partial(_inproj_kernel, kv_off=kv_off),
        grid=(n // tm, MAIN_COLS // tn),
        in_specs=[
            pl.BlockSpec((tm, D_MODEL), lambda i, j: (i, 0)),
            pl.BlockSpec((1, D_MODEL), lambda i, j: (0, 0)),
            pl.BlockSpec((D_MODEL, tn), lambda i, j: (0, j)),
            pl.BlockSpec((D_MODEL, SMALL_COLS), lambda i, j: (0, 0)),
        ],
        out_specs=[
            pl.BlockSpec((tm, tn), lambda i, j: (i, j)),
            pl.BlockSpec((tm, SMALL_COLS), lambda i, j: (i, 0)),
            pl.BlockSpec((tm, LANES), lambda i, j: (i, 0)),
            pl.BlockSpec((tm, LANES), lambda i, j: (i, 0)),
        ],
        out_shape=[jax.ShapeDtypeStruct((n, MAIN_COLS), BF16),
                   jax.ShapeDtypeStruct((n, SMALL_COLS), F32),
                   jax.ShapeDtypeStruct((n, LANES), BF16),
                   jax.ShapeDtypeStruct((n, LANES), BF16)],
        scratch_shapes=[pltpu.VMEM((tm, D_MODEL), BF16)],
        compiler_params=_cparams(("parallel", "arbitrary")),
        name="inproj",
    )(x2, g, w_main, w_small)


def _compress_kernel(xk_ref, xv_ref, pek_ref, pev_ref, w1k_ref, w1v_ref,
                     wak_ref, wbk_ref, wav_ref, wbv_ref, w2k_ref, w2v_ref, kc_ref, vc_ref):
    def one(x_ref, pe_ref, w1_ref, wa_ref, wb_ref, w2_ref, out_ref, out_scale):
        x = x_ref[...]
        a = _dot(x, wa_ref[...])
        b = _dot(x, wb_ref[...])
        nc = a.shape[0]
        b_up = pltpu.roll(b, nc - 1, 0)
        pe_c = _dot(pe_ref[...].astype(BF16), w1_ref[...])[0:1]
        pe_c = jnp.concatenate([pe_c, pe_c], axis=1)
        hid = a + b_up + pe_c
        hid = (hid * _sigmoid(hid)).astype(BF16)
        for g in range(NSA_GROUPS):
            out_ref[g] = (_dot(hid, w2_ref[g]) * out_scale).astype(BF16)

    one(xk_ref, pek_ref, w1k_ref, wak_ref, wbk_ref, w2k_ref, kc_ref, SCALE * LOG2E)
    one(xv_ref, pev_ref, w1v_ref, wav_ref, wbv_ref, w2v_ref, vc_ref, 1.0)


def _compress(xk, xv, pek, pev, w1k, w1v, wak, wbk, wav, wbv, w2k, w2v):
    b, nc, kw = xk.shape
    full = lambda a: pl.BlockSpec(a.shape, lambda i: (0,) * a.ndim)
    xs = pl.BlockSpec((None, nc, kw), lambda i: (i, 0, 0))
    os_ = pl.BlockSpec((None, NSA_GROUPS, nc, LANES), lambda i: (i, 0, 0, 0))
    return pl.pallas_call(
        _compress_kernel,
        grid=(b,),
        in_specs=[xs, xs] + [full(a) for a in (pek, pev, w1k, w1v, wak, wbk, wav, wbv, w2k, w2v)],
        out_specs=[os_, os_],
        out_shape=[jax.ShapeDtypeStruct((b, NSA_GROUPS, nc, LANES), BF16)] * 2,
        compiler_params=_cparams(("parallel",)),
        name="compress",
    )(xk, xv, pek, pev, w1k, w1v, wak, wbk, wav, wbv, w2k, w2v)


def _nsaprep_kernel(q_ref, s_ref, w_ref, sm_ref, cos_ref, sin_ref, gmat_ref,
                    qh_ref, ks_ref, kw_ref, vs_ref, vw_ref, ge_ref, *, tt):
    cos = cos_ref[...]
    sin = sin_ref[...]
    lane = _lane_iota((tt, LANES))
    t = pl.program_id(1) * tt + lax.broadcasted_iota(jnp.int32, (tt, LANES), 0)
    low = lane < HEAD_DIM
    for j in range(NSA_HEADS // 2):
        x = _rope_slab(q_ref[:, j * LANES:(j + 1) * LANES].astype(F32), cos, sin) * (SCALE * LOG2E)
        qh_ref[2 * j] = jnp.where(low, x, 0.0).astype(BF16)
        qh_ref[2 * j + 1] = jnp.where(low, pltpu.roll(x, HEAD_DIM, 1), 0.0).astype(BF16)

    onehot = jnp.where((lane >= EXT) & (lane - EXT == t // SEL_BLOCK) & (lane < EXT + N_SEL), 1.0, 0.0)
    ks = _rope_slab(s_ref[:, 0:LANES].astype(F32), cos, sin)
    kw = _rope_slab(w_ref[:, 0:LANES].astype(F32), cos, sin)
    for g in range(NSA_GROUPS):
        ks_ref[g] = jnp.where(low, ks if g == 0 else pltpu.roll(ks, HEAD_DIM, 1), onehot).astype(BF16)
        kw_ref[g] = jnp.where(low, kw if g == 0 else pltpu.roll(kw, HEAD_DIM, 1), 0.0).astype(BF16)

    for src_ref, dst_ref in ((s_ref, vs_ref), (w_ref, vw_ref)):
        v = src_ref[:, LANES:2 * LANES].astype(F32)
        v_sw = pltpu.roll(v, HEAD_DIM, 1)
        dst_ref[0, 0] = jnp.where(low, v, 1.0).astype(BF16)
        dst_ref[0, 1] = jnp.where(low, 1.0, v_sw).astype(BF16)
        dst_ref[1, 0] = jnp.where(low, v_sw, 1.0).astype(BF16)
        dst_ref[1, 1] = jnp.where(low, 1.0, v).astype(BF16)
    gates = _sigmoid(sm_ref[:, 0:LANES]).astype(BF16)
    for br in range(3):
        ge_ref[br] = _dot(gates, gmat_ref[br]).astype(BF16)


def _cmp_kernel(q_ref, kc_ref, vc_ref, ovl_ref, ge_ref, o_ref, nsel_ref, *, tq):
    q0 = pl.program_id(1) * tq
    nc = kc_ref.shape[1]
    lane = _lane_iota((tq, LANES))
    t = q0 + lax.broadcasted_iota(jnp.int32, (tq, LANES), 0)
    cmp_valid = (lane * CMP_STRIDE + CMP_BLOCK - 1 <= t) & (lane < nc)
    row_valid = (t >= CMP_BLOCK - 1)[None]
    cur = t // SEL_BLOCK
    forced = (lane == 0) | (lane == cur) | (lane == cur - 1)
    blk_valid = lane * SEL_BLOCK <= t
    row8 = lax.broadcasted_iota(jnp.int32, (8, tq), 0)
    for g in range(NSA_GROUPS):
        qs = _stack_heads([q_ref[:, (g * 4 + j) * LANES:(g * 4 + j + 1) * LANES] for j in range(4)])
        s3 = _dot_nt(qs, kc_ref[g]).reshape(NSA_HPG, tq, nc)
        s3 = jnp.where(cmp_valid[None], s3, NEG_INF)
        m = jnp.max(s3, axis=-1, keepdims=True)
        e = jnp.exp2(s3 - m)
        l = _dot(e.reshape(NSA_HPG * tq, nc).astype(BF16), jnp.ones((nc, LANES), BF16))
        inv = jnp.where(row_valid, 1.0 / l.reshape(NSA_HPG, tq, LANES), 0.0)
        p = (e * inv).astype(BF16)
        o = _dot(p.reshape(NSA_HPG * tq, nc), vc_ref[g])
        for j, slab in enumerate(_unstack_heads(o, 4, tq)):
            cols = slice((g * 4 + j) * LANES, (g * 4 + j + 1) * LANES)
            o_ref[:, cols] = slab * ge_ref[:, cols].astype(F32)
        imp = _split3_dot(jnp.sum(p.astype(F32), axis=0), ovl_ref[...])
        score = jnp.where(forced, FORCED_SCORE, jnp.where(blk_valid, imp, -1.0))
        score = jnp.where(lane < N_SEL, score, -2.0)
        st = score.T[0:N_SEL, :]
        cnt = [jnp.zeros((8, tq), F32) for _ in range(N_SEL // 8)]
        for j in range(N_SEL):
            rj = st[j:j + 1, :]
            for r in range(N_SEL // 8):
                blk = st[8 * r:8 * r + 8, :]
                if j < 8 * r:
                    beats = rj >= blk
                elif j >= 8 * r + 8:
                    beats = rj > blk
                else:
                    beats = (rj > blk) | ((rj == blk) & (row8 + 8 * r > j))
                cnt[r] = cnt[r] + jnp.where(beats, 1.0, 0.0)
        nsel_t = jnp.where(jnp.concatenate(cnt, axis=0) < float(SEL_TOPK), 0.0, NEG_INF)
        nsel_t = jnp.concatenate([jnp.zeros((EXT, tq), F32), nsel_t,
                                  jnp.zeros((LANES - EXT - N_SEL, tq), F32)], axis=0)
        nsel_ref[g] = nsel_t.T.astype(BF16)


def _nsafront_kernel(q_ref, s_ref, w_ref, sm_ref, cos_ref, sin_ref, gmat_ref, kc_ref, vc_ref, ovl_ref,
                     qh_ref, ks_ref, kw_ref, vs_ref, vw_ref, ge_ref, o_ref, nsel_ref, *, tq):
    _nsaprep_kernel(q_ref, s_ref, w_ref, sm_ref, cos_ref, sin_ref, gmat_ref,
                    qh_ref, ks_ref, kw_ref, vs_ref, vw_ref, ge_ref, tt=tq)
    _cmp_kernel(q_ref, kc_ref, vc_ref, ovl_ref, ge_ref.at[0], o_ref, nsel_ref, tq=tq)


def _nsa_front(main3, small3, cos, sin, gmat, kc, vc, ovl, tq=256):
    b, t, _ = main3.shape
    nc = kc.shape[2]
    blk = COL_KV // (2 * LANES)
    kspec = pl.BlockSpec((None, NSA_GROUPS, tq, LANES), lambda i, j: (i, 0, j, 0))
    vspec = pl.BlockSpec((None, NSA_GROUPS, 2, tq, LANES), lambda i, j: (i, 0, 0, j, 0))
    cspec = pl.BlockSpec((None, NSA_GROUPS, nc, LANES), lambda i, j: (i, 0, 0, 0))
    return pl.pallas_call(
        functools.partial(_nsafront_kernel, tq=tq),
        grid=(b, t // tq),
        in_specs=[
            pl.BlockSpec((None, tq, 1024), lambda i, j: (i, j, COL_QA // 1024)),
            pl.BlockSpec((None, tq, 2 * LANES), lambda i, j: (i, j, blk + 1)),
            pl.BlockSpec((None, tq, 2 * LANES), lambda i, j: (i, j, blk + 2)),
            pl.BlockSpec((None, tq, SMALL_COLS), lambda i, j: (i, j, 0)),
            pl.BlockSpec((tq, LANES), lambda i, j: (j, 0)),
            pl.BlockSpec((tq, LANES), lambda i, j: (j, 0)),
            pl.BlockSpec(gmat.shape, lambda i, j: (0, 0, 0)),
            cspec, cspec,
            pl.BlockSpec(ovl.shape, lambda i, j: (0, 0)),
        ],
        out_specs=[
            pl.BlockSpec((None, NSA_HEADS, tq, LANES), lambda i, j: (i, 0, j, 0)),
            kspec, kspec, vspec, vspec,
            pl.BlockSpec((None, 3, tq, 1024), lambda i, j: (i, 0, j, 0)),
            pl.BlockSpec((None, tq, 1024), lambda i, j: (i, j, 0)),
            kspec,
        ],
        out_shape=[jax.ShapeDtypeStruct((b, NSA_HEADS, t, LANES), BF16)]
        + [jax.ShapeDtypeStruct((b, NSA_GROUPS, t, LANES), BF16)] * 2
        + [jax.ShapeDtypeStruct((b, NSA_GROUPS, 2, t, LANES), BF16)] * 2
        + [jax.ShapeDtypeStruct((b, 3, t, 1024), BF16),
           jax.ShapeDtypeStruct((b, t, 1024), F32),
           jax.ShapeDtypeStruct((b, NSA_GROUPS, t, LANES), BF16)],
        compiler_params=_cparams(("parallel", "parallel")),
        name="nsa_front",
    )(main3, main3, main3, small3, cos, sin, gmat, kc, vc, ovl)


def _tile_lanes(x, n):
    return jnp.concatenate([x] * n, axis=1)


def _lane_block_max(mrun, s):
    for w in range(s.shape[1] // LANES):
        mrun = jnp.maximum(mrun, s[:, w * LANES:(w + 1) * LANES])
    return mrun


def _row_max_tiled(mrun, n):
    return _tile_lanes(jnp.broadcast_to(jnp.max(mrun, axis=1, keepdims=True), mrun.shape), n)


def _emit_pipelined(units, score_stage, value_stage):
    def drain(gens):
        gens = [g for g in gens if g is not None]
        while gens:
            for g in list(gens):
                try:
                    next(g)
                except StopIteration:
                    gens.remove(g)

    drain([score_stage(units[0])])
    for n, u in enumerate(units):
        drain([score_stage(units[n + 1]) if n + 1 < len(units) else None, value_stage(u)])


EVEN_ODD = tuple(range(0, NSA_HPG, 2)) + tuple(range(1, NSA_HPG, 2))


def _group_slabs(acc_even, acc_odd, gates, tq):
    low = _lane_iota((tq, LANES)) < HEAD_DIM
    slabs = []
    for j in range(NSA_HPG // 2):
        e = acc_even[j * tq:(j + 1) * tq]
        o = acc_odd[j * tq:(j + 1) * tq]
        den = pltpu.roll(jnp.where(low, o, e), HEAD_DIM, 1)
        slabs.append(jnp.where(low, e, o) / den * gates[:, j * LANES:(j + 1) * LANES].astype(F32))
    return slabs


def _slc_kernel(qh_ref, k_ref, v_ref, nsel_ref, ge_ref, prev_ref, o_ref, s_ref):
    tq, tk = TQ_NSA, TK_SLC
    rows = NSA_HPG * tq
    t_total = k_ref.shape[0]
    nq = t_total // tq
    order = [j // 2 if j % 2 else nq - 1 - j // 2 for j in range(nq)]
    row_max = {}

    def key_chunks(i):
        c_last = i * tq // tk
        return [(c * tk, tk) for c in range(c_last)] + [(c_last * tk, (i + 1) * tq - c_last * tk)]

    def score_stage(i):
        q0 = i * tq
        sbuf = s_ref.at[order.index(i) % 2]
        nsel = nsel_ref[q0:q0 + tq, :]
        qs = jnp.concatenate([qh_ref[h, q0:q0 + tq, :] + nsel for h in EVEN_ODD], axis=0)
        mrun = jnp.full((rows, LANES), NEG_INF, F32)
        pieces = key_chunks(i)
        for n, (k0, w) in enumerate(pieces):
            s = _dot_nt(qs, k_ref[k0:k0 + w, :])
            if n == len(pieces) - 1:
                key = k0 + lax.broadcasted_iota(jnp.int32, (tq, w), 1)
                qry = q0 + lax.broadcasted_iota(jnp.int32, (tq, w), 0)
                causal = jnp.where(key <= qry, 0.0, NEG_INF)
                s = (s.reshape(NSA_HPG, tq, w) + causal[None]).reshape(rows, w)
            sbuf[n, :, 0:w] = s
            mrun = _lane_block_max(mrun, s)
            yield
        row_max[i] = _row_max_tiled(mrun, tk // LANES)

    def value_stage(i):
        q0 = i * tq
        sbuf = s_ref.at[order.index(i) % 2]
        m = row_max.pop(i)
        acc = [jnp.zeros((rows // 2, LANES), F32)] * 2
        for n, (k0, w) in enumerate(key_chunks(i)):
            p = jnp.exp2(sbuf[n, :, 0:w] - m[:, 0:w]).astype(BF16)
            for e in range(2):
                acc[e] = acc[e] + _dot(p[e * rows // 2:(e + 1) * rows // 2], v_ref[e, k0:k0 + w, :])
            yield
        for j, slab in enumerate(_group_slabs(acc[0], acc[1], ge_ref[q0:q0 + tq, :], tq)):
            cols = slice(j * LANES, (j + 1) * LANES)
            o_ref[q0:q0 + tq, cols] = prev_ref[q0:q0 + tq, cols] + slab

    _emit_pipelined(order, score_stage, value_stage)


def _slc_attn(qh, ks, vs, nsel, ge, prev):
    b, _, t, _ = qh.shape
    gw = NSA_HPG * HEAD_DIM
    kv = pl.BlockSpec((None, None, t, LANES), lambda i, g: (i, g, 0, 0))
    return pl.pallas_call(
        _slc_kernel,
        grid=(b, NSA_GROUPS),
        in_specs=[
            pl.BlockSpec((None, NSA_HPG, t, LANES), lambda i, g: (i, g, 0, 0), pipeline_mode=pl.Buffered(1)),
            kv,
            pl.BlockSpec((None, None, 2, t, LANES), lambda i, g: (i, g, 0, 0, 0)),
            kv,
            pl.BlockSpec((None, None, t, gw), lambda i, g: (i, 1, 0, g)),
            pl.BlockSpec((None, t, gw), lambda i, g: (i, 0, g), pipeline_mode=pl.Buffered(1)),
        ],
        out_specs=pl.BlockSpec((None, t, gw), lambda i, g: (i, 0, g)),
        out_shape=jax.ShapeDtypeStruct((b, t, 1024), F32),
        scratch_shapes=[pltpu.VMEM((2, t // TK_SLC, NSA_HPG * TQ_NSA, TK_SLC), F32)],
        compiler_params=_cparams(("parallel", "parallel")),
        name="slc_attn",
    )(qh, ks, vs, nsel, ge, prev)


def _win_kernel(qh_ref, k_ref, v_ref, ge_ref, prev_ref, z_ref, y_ref, s_ref):
    tq = TQ_NSA
    rows = NSA_HPG * tq
    t_total = k_ref.shape[0]
    qry_l = lax.broadcasted_iota(jnp.int32, (tq, LANES), 0)
    key_l = lax.broadcasted_iota(jnp.int32, (tq, LANES), 1)

    step = 2 * LANES
    row_max = {}

    def window(i):
        k_lo = max(i * tq - WINDOW, 0)
        return k_lo, [(c, min(step, i * tq + tq - k_lo - c)) for c in range(0, i * tq + tq - k_lo, step)]

    def score_stage(i):
        q0 = i * tq
        k_lo, pieces = window(i)
        sbuf = s_ref.at[i % 2]
        qs = jnp.concatenate([qh_ref[h, q0:q0 + tq, :] for h in EVEN_ODD], axis=0)
        mrun = jnp.full((rows, LANES), NEG_INF, F32)
        for c0, cw in pieces:
            s = _dot_nt(qs, k_ref[k_lo + c0:k_lo + c0 + cw, :])
            for w in range(cw // LANES):
                k0 = k_lo + c0 + w * LANES
                blk = s[:, w * LANES:(w + 1) * LANES]
                if k0 == q0 or k0 == q0 - WINDOW:
                    dist = (k0 - q0) + key_l - qry_l
                    band = jnp.where((dist <= 0) & (dist > -WINDOW), 0.0, NEG_INF)
                    blk = (blk.reshape(NSA_HPG, tq, LANES) + band[None]).reshape(rows, LANES)
                sbuf[:, c0 + w * LANES:c0 + (w + 1) * LANES] = blk
                mrun = jnp.maximum(mrun, blk)
            yield
        row_max[i] = jnp.broadcast_to(jnp.max(mrun, axis=1, keepdims=True), mrun.shape)

    def value_stage(i):
        q0 = i * tq
        k_lo, pieces = window(i)
        sbuf = s_ref.at[i % 2]
        m = row_max.pop(i)
        acc = [jnp.zeros((rows // 2, LANES), F32)] * 2
        for c0, cw in pieces:
            p = jnp.exp2(sbuf[:, c0:c0 + cw] - _tile_lanes(m, cw // LANES)).astype(BF16)
            for e in range(2):
                acc[e] = acc[e] + _dot(p[e * rows // 2:(e + 1) * rows // 2], v_ref[e, k_lo + c0:k_lo + c0 + cw, :])
            yield
        for j, slab in enumerate(_group_slabs(acc[0], acc[1], ge_ref[q0:q0 + tq, :], tq)):
            cols = slice(j * LANES, (j + 1) * LANES)
            z = z_ref[q0:q0 + tq, cols].astype(F32)
            y = (prev_ref[q0:q0 + tq, cols] + slab) * (z * _sigmoid(z))
            y_ref[q0:q0 + tq, cols] = y.astype(BF16)

    _emit_pipelined(list(range(t_total // tq)), score_stage, value_stage)


def _win_attn(qh, kw, vw, ge, prev, main3):
    b, _, t, _ = qh.shape
    gw = NSA_HPG * HEAD_DIM
    kv = pl.BlockSpec((None, None, t, LANES), lambda i, g: (i, g, 0, 0))
    return pl.pallas_call(
        _win_kernel,
        grid=(b, NSA_GROUPS),
        in_specs=[
            pl.BlockSpec((None, NSA_HPG, t, LANES), lambda i, g: (i, g, 0, 0)),
            kv,
            pl.BlockSpec((None, None, 2, t, LANES), lambda i, g: (i, g, 0, 0, 0)),
            pl.BlockSpec((None, None, t, gw), lambda i, g: (i, 2, 0, g)),
            pl.BlockSpec((None, t, gw), lambda i, g: (i, 0, g)),
            pl.BlockSpec((None, t, gw), lambda i, g: (i, 0, COL_ZA // gw + g)),
        ],
        out_specs=pl.BlockSpec((None, t, gw), lambda i, g: (i, 0, g)),
        out_shape=jax.ShapeDtypeStruct((b, t, 1024), BF16),
        scratch_shapes=[pltpu.VMEM((2, NSA_HPG * TQ_NSA, WINDOW + TQ_NSA), F32)],
        compiler_params=_cparams(("parallel", "parallel")),
        name="win_attn",
    )(qh, kw, vw, ge, prev, main3)


def _decay_kernel(sm_ref, b_ref, f_ref):
    t_total = sm_ref.shape[0]
    x = sm_ref[:, LANES:2 * LANES] + b_ref[...]
    f = jnp.minimum(x, 0.0) - jnp.log1p(jnp.exp(-jnp.abs(x)))
    row = lax.broadcasted_iota(jnp.int32, f.shape, 0)
    sh = 1
    while sh < t_total:
        f = f + jnp.where(row >= sh, pltpu.roll(f, sh, 0), 0.0)
        sh *= 2
    hi, mid, lo = _split3(f * (-LOG2E))
    lane = _lane_iota(f.shape)
    pieces = jnp.where(lane < FOX_HEADS, hi,
                       jnp.where(lane < 2 * FOX_HEADS, pltpu.roll(mid, FOX_HEADS, 1),
                                 jnp.where(lane < 3 * FOX_HEADS, pltpu.roll(lo, 2 * FOX_HEADS, 1), 0.0)))
    f_ref[...] = pieces.astype(BF16)


def _decay(small3, b_pad):
    b, t, _ = small3.shape
    return pl.pallas_call(
        _decay_kernel,
        grid=(b,),
        in_specs=[pl.BlockSpec((None, t, SMALL_COLS), lambda i: (i, 0, 0)),
                  pl.BlockSpec((1, LANES), lambda i: (0, 0))],
        out_specs=pl.BlockSpec((None, t, LANES), lambda i: (i, 0, 0)),
        out_shape=jax.ShapeDtypeStruct((b, t, LANES), BF16),
        compiler_params=_cparams(("parallel",)),
        name="fox_decay",
    )(small3, b_pad)


def _fox_kernel(q_ref, k_ref, v_ref, f_ref, z_ref, y_ref, qf_ref, kf_ref, vf_ref, s_ref):
    tq = TQ_FOX
    t_total = z_ref.shape[0]
    slab_idx = pl.program_id(1)
    causal = (lax.broadcasted_iota(jnp.int32, (tq, tq), 0) >= lax.broadcasted_iota(jnp.int32, (tq, tq), 1))
    lane = _lane_iota((tq, LANES))

    src = lax.broadcasted_iota(jnp.int32, (LANES, LANES), 0)
    dst = lax.broadcasted_iota(jnp.int32, (LANES, LANES), 1)
    for e in range(2):
        own = (lane < HEAD_DIM) if e == 0 else (lane >= HEAD_DIM)
        ext0 = (1 - e) * HEAD_DIM
        head = 2 * slab_idx + e
        place = sum(jnp.where((src == j * FOX_HEADS + head) & (dst == ext0 + j), 1.0, 0.0)
                    for j in range(3)).astype(BF16)
        q_ext = jnp.where((lane >= ext0) & (lane < ext0 + 3), 1.0, 0.0).astype(BF16)
        for r0 in range(0, t_total, tq):
            rs = slice(r0, r0 + tq)
            k_ext = _dot(f_ref[rs, :], place).astype(BF16)
            qf_ref[e, rs, :] = jnp.where(own, q_ref[rs, :], q_ext)
            kf_ref[e, rs, :] = jnp.where(own, k_ref[rs, :], k_ext)
            vf_ref[e, rs, :] = jnp.where(own, v_ref[rs, :], jnp.ones((tq, LANES), BF16))

    nq = t_total // tq
    units = [(i, h) for i in [j // 2 if j % 2 else nq - 1 - j // 2 for j in range(nq)] for h in range(2)]
    row_max, outs = {}, {}

    def score_stage(u):
        i, h = u
        sbuf = s_ref.at[units.index(u) % 2]
        q = qf_ref[h, i * tq:(i + 1) * tq, :]
        mrun = jnp.full((tq, LANES), NEG_INF, F32)
        for c in range(i + 1):
            s = _dot_nt(q, kf_ref[h, c * tq:(c + 1) * tq, :])
            if c == i:
                s = jnp.where(causal, s, NEG_INF)
            sbuf[c] = s
            mrun = _lane_block_max(mrun, s)
            yield
        row_max[u] = _row_max_tiled(mrun, tq // LANES)

    def value_stage(u):
        i, h = u
        sbuf = s_ref.at[units.index(u) % 2]
        m = row_max.pop(u)
        acc = jnp.zeros((tq, LANES), F32)
        for c in range(i + 1):
            p = jnp.exp2(sbuf[c] - m).astype(BF16)
            acc = acc + _dot(p, vf_ref[h, c * tq:(c + 1) * tq, :])
            yield
        outs[u] = acc
        if h == 1:
            low = lane < HEAD_DIM
            a0, a1 = outs.pop((i, 0)), outs.pop((i, 1))
            slab = jnp.where(low, a0, a1) / pltpu.roll(jnp.where(low, a1, a0), HEAD_DIM, 1)
            z = z_ref[i * tq:(i + 1) * tq, :].astype(F32)
            y_ref[i * tq:(i + 1) * tq, :] = (slab * (z * _sigmoid(z))).astype(BF16)

    _emit_pipelined(units, score_stage, value_stage)


def _fox_attn(main3, f3):
    b, t, _ = main3.shape
    nslab = FOX_HEADS // 2

    def col(base):
        return pl.BlockSpec((None, t, LANES), lambda i, p: (i, 0, base // LANES + p))

    return pl.pallas_call(
        _fox_kernel,
        grid=(b, nslab),
        in_specs=[col(COL_QB), col(COL_KB), col(COL_VB),
                  pl.BlockSpec((None, t, LANES), lambda i, p: (i, 0, 0)),
                  col(COL_ZB)],
        out_specs=pl.BlockSpec((None, t, LANES), lambda i, p: (i, 0, p)),
        out_shape=jax.ShapeDtypeStruct((b, t, 1024), BF16),
        scratch_shapes=[pltpu.VMEM((2, t, LANES), BF16), pltpu.VMEM((2, t, LANES), BF16),
                        pltpu.VMEM((2, t, LANES), BF16),
                        pltpu.VMEM((2, t // TQ_FOX, TQ_FOX, TQ_FOX), F32)],
        compiler_params=_cparams(("parallel", "parallel")),
        name="fox_attn",
    )(main3, main3, main3, f3, main3)


def _out_kernel(x_ref, ya_ref, yb_ref, ra_ref, rb_ref, wn_ref, wf_ref, wo_ref, fg_ref, o_ref, *, final):
    ta = _dot(ya_ref[...], wn_ref[...])
    tb = _dot(yb_ref[...], wf_ref[...])
    merged = _sigmoid(ra_ref[...].astype(F32)) * ta + _sigmoid(rb_ref[...].astype(F32)) * tb
    out = x_ref[...] + _dot(merged.astype(BF16), wo_ref[...])
    if final:
        out = out * lax.rsqrt(jnp.mean(out * out, axis=-1, keepdims=True) + RMS_EPS) * fg_ref[...]
    o_ref[...] = out


def _out_proj(x2, ya2, yb2, main2, wn, wf, wo, fg, final, tm=1024):
    n = x2.shape[0]
    tm = min(tm, n)
    row = lambda c: pl.BlockSpec((tm, 1024), lambda i: (i, c))
    w = pl.BlockSpec((1024, 1024), lambda i: (0, 0))
    return pl.pallas_call(
        functools.partial(_out_kernel, final=final),
        grid=(n // tm,),
        in_specs=[row(0), row(0), row(0), row(COL_RA // 1024), row(COL_RB // 1024), w, w, w,
                  pl.BlockSpec((1, 1024), lambda i: (0, 0))],
        out_specs=row(0),
        out_shape=jax.ShapeDtypeStruct((n, 1024), F32),
        compiler_params=_cparams(("parallel",)),
        name="out_proj",
    )(x2, ya2, yb2, main2, main2, wn, wf, wo, fg)


def _rope_tables(t):
    inv = ROPE_THETA ** (-jnp.arange(0, HEAD_DIM, 2, dtype=F32) / HEAD_DIM)
    ang = jnp.arange(t, dtype=F32)[:, None] * inv[None, :]
    c, s = jnp.cos(ang), jnp.sin(ang)
    return jnp.concatenate([c, c, c, c], axis=1), jnp.concatenate([-s, s, -s, s], axis=1)


def _overlap_matrix():
    n = np.arange(LANES)[:, None] * CMP_STRIDE
    j = np.arange(LANES)[None, :] * SEL_BLOCK
    m = (n < j + SEL_BLOCK) & (n + CMP_BLOCK > j) & (np.arange(LANES)[None, :] < N_SEL)
    return jnp.asarray(m.astype(np.float32), dtype=BF16)


def _gate_expansion():
    m = np.zeros((3, LANES, NSA_HEADS * HEAD_DIM), np.float32)
    for br in range(3):
        for h in range(NSA_HEADS):
            m[br, br * NSA_HEADS + h, h * HEAD_DIM:(h + 1) * HEAD_DIM] = 1.0
    return jnp.asarray(m, dtype=BF16)


def _reorder_w_in(w):
    qa, kv, ga, za, qb, kb, vb, fb, zb, mg = jnp.split(
        w, np.cumsum([1024, 768, 48, 1024, 1024, 1024, 1024, 16, 1024])[:9].tolist(), axis=1)
    main = jnp.concatenate([qa, za, qb * (SCALE * LOG2E), kb, vb, zb, mg, kv], axis=1).astype(BF16)
    d = w.shape[0]
    small = jnp.concatenate([ga, jnp.zeros((d, LANES - 48), F32), fb, jnp.zeros((d, LANES - 16), F32)],
                            axis=1).astype(BF16)
    return main, small


def _compress_weights(w1, w2):
    eye = jnp.eye(NSA_GROUPS, dtype=F32)
    w1r = w1.reshape(CMP_BLOCK, HEAD_DIM, CMP_HIDDEN)
    half = CMP_BLOCK // 2

    def blockdiag(wl):
        return jnp.einsum('ldh,gk->lgdkh', wl, eye).reshape(half * NSA_GROUPS * HEAD_DIM,
                                                            NSA_GROUPS * CMP_HIDDEN).astype(BF16)

    wa, wb = blockdiag(w1r[:half]), blockdiag(w1r[half:])
    w2d = jnp.einsum('hd,gk,c->gkhcd', w2, eye, jnp.ones((2,), F32)).reshape(
        NSA_GROUPS, NSA_GROUPS * CMP_HIDDEN, LANES).astype(BF16)
    return wa, wb, w2d


def kernel(x, norm_g, w_in, b_forget, cmp_pe_k, cmp_w1_k, cmp_w2_k, cmp_pe_v, cmp_w1_v, cmp_w2_v,
           w_proj_nsa, w_proj_fox, w_out, final_g):
    b, t, d = x.shape
    depth = norm_g.shape[0]
    n = b * t
    nc = t // CMP_STRIDE
    assert d == D_MODEL and nc == LANES and t // SEL_BLOCK == N_SEL
    cos, sin = _rope_tables(t)
    ovl = _overlap_matrix()
    gmat = _gate_expansion()
    fg = final_g.reshape(1, d)
    x2 = x.reshape(n, d)
    for l in range(depth):
        w_main, w_small = _reorder_w_in(w_in[l])
        main2, small2, kc2, vc2 = _inproj(x2, norm_g[l].reshape(1, d), w_main, w_small)
        main3 = main2.reshape(b, t, MAIN_COLS)
        small3 = small2.reshape(b, t, SMALL_COLS)
        xk = kc2.reshape(b, nc, CMP_STRIDE * LANES)
        xv = vc2.reshape(b, nc, CMP_STRIDE * LANES)
        wak, wbk, w2k = _compress_weights(cmp_w1_k[l], cmp_w2_k[l])
        wav, wbv, w2v = _compress_weights(cmp_w1_v[l], cmp_w2_v[l])
        pek = jnp.broadcast_to(cmp_pe_k[l].reshape(1, -1), (8, CMP_BLOCK * HEAD_DIM))
        pev = jnp.broadcast_to(cmp_pe_v[l].reshape(1, -1), (8, CMP_BLOCK * HEAD_DIM))
        kc, vc = _compress(xk, xv, pek, pev, cmp_w1_k[l].astype(BF16), cmp_w1_v[l].astype(BF16),
                           wak, wbk, wav, wbv, w2k, w2v)
        qh, ks, kw, vs, vw, ge, o_cmp, nsel = _nsa_front(main3, small3, cos, sin, gmat, kc, vc, ovl)
        o_cs = _slc_attn(qh, ks, vs, nsel, ge, o_cmp)
        y_a = _win_attn(qh, kw, vw, ge, o_cs, main3)
        b_pad = jnp.concatenate([b_forget[l], jnp.zeros((LANES - FOX_HEADS,), F32)]).reshape(1, LANES)
        y_b = _fox_attn(main3, _decay(small3, b_pad))
        x2 = _out_proj(x2, y_a.reshape(n, 1024), y_b.reshape(n, 1024), main2,
                       w_proj_nsa[l].astype(BF16), w_proj_fox[l].astype(BF16), w_out[l].astype(BF16),
                       fg, final=(l == depth - 1))
    return x2.reshape(b, t, d)
```

```python
import functools

import numpy as np
import jax
import jax.numpy as jnp
from jax import lax
from jax.experimental import pallas as pl
from jax.experimental.pallas import tpu as pltpu

F32 = jnp.float32
BF16 = jnp.bfloat16

D_MODEL = 1024
HEAD_DIM = 64
LANES = 128
NSA_HEADS = 16
NSA_GROUPS = 2
NSA_HPG = NSA_HEADS // NSA_GROUPS
CMP_BLOCK = 32
CMP_STRIDE = 16
CMP_HIDDEN = 2 * HEAD_DIM
SEL_BLOCK = 64
SEL_TOPK = 8
N_SEL = 32
WINDOW = 512
FOX_HEADS = 16
ROPE_THETA = 10000.0
RMS_EPS = 1e-6
NEG_INF = -1e30
FORCED_SCORE = 1e4
SCALE = HEAD_DIM ** -0.5
LOG2E = 1.4426950408889634
EXT = HEAD_DIM

COL_QA, COL_ZA, COL_QB, COL_KB, COL_VB, COL_ZB, COL_RA, COL_RB, COL_KV = (
    0, 1024, 2048, 3072, 4096, 5120, 6144, 7168, 8192)
MAIN_COLS = 8960
SMALL_COLS = 256

TQ_NSA = 128
TK_SLC = 256
TQ_FOX = 256
VMEM_LIMIT = 56 * 1024 * 1024


def _cparams(sem):
    return pltpu.CompilerParams(dimension_semantics=sem, vmem_limit_bytes=VMEM_LIMIT)


def _lane_iota(shape):
    return lax.broadcasted_iota(jnp.int32, shape, len(shape) - 1)


def _dot_nt(a, b):
    return lax.dot_general(a, b, (((1,), (1,)), ((), ())), preferred_element_type=F32)


def _dot(a, b):
    return jnp.dot(a, b, preferred_element_type=F32)


def _split3(x):
    hi = x.astype(BF16).astype(F32)
    r = x - hi
    mid = r.astype(BF16).astype(F32)
    lo = (r - mid).astype(BF16).astype(F32)
    return hi, mid, lo


def _split3_dot(a, m):
    hi, mid, lo = _split3(a)
    return _dot(hi.astype(BF16), m) + _dot(mid.astype(BF16), m) + _dot(lo.astype(BF16), m)


def _sigmoid(x):
    return 1.0 / (1.0 + jnp.exp(-x))


def _column(x, idx):
    return jnp.sum(jnp.where(_lane_iota(x.shape) == idx, x, 0.0), axis=-1, keepdims=True)


def _rope_slab(x, cos, sin):
    lane = _lane_iota(x.shape)
    swapped = jnp.where((lane % HEAD_DIM) < HEAD_DIM // 2,
                        pltpu.roll(x, LANES - HEAD_DIM // 2, 1), pltpu.roll(x, HEAD_DIM // 2, 1))
    return x * cos + swapped * sin


def _stack_heads(slabs):
    lane = _lane_iota(slabs[0].shape)
    parts = []
    for s in slabs:
        parts.append(jnp.where(lane < HEAD_DIM, s, jnp.zeros_like(s)))
        parts.append(jnp.where(lane >= HEAD_DIM, s, jnp.zeros_like(s)))
    return jnp.concatenate(parts, axis=0)


def _unstack_heads(o, nslab, rows):
    lane = _lane_iota((rows, LANES))
    out = []
    for j in range(nslab):
        a = o[(2 * j) * rows:(2 * j + 1) * rows]
        b = o[(2 * j + 1) * rows:(2 * j + 2) * rows]
        out.append(jnp.where(lane < HEAD_DIM, a, b))
    return out


def _inproj_kernel(x_ref, g_ref, w_ref, ws_ref, main_ref, small_ref, kc_ref, vc_ref, xn_ref, *, kv_off):
    @pl.when(pl.program_id(1) == 0)
    def _():
        x = x_ref[...]
        y = x * lax.rsqrt(jnp.mean(x * x, axis=-1, keepdims=True) + RMS_EPS) * g_ref[...]
        xn = y.astype(BF16)
        xn_ref[...] = xn
        small_ref[...] = _dot(xn, ws_ref[...])

    res = _dot(xn_ref[...], w_ref[...]).astype(BF16)
    main_ref[...] = res

    @pl.when(pl.program_id(1) == pl.num_programs(1) - 1)
    def _():
        kc_ref[...] = res[:, kv_off:kv_off + LANES]
        vc_ref[...] = res[:, kv_off + LANES:kv_off + 2 * LANES]


def _inproj(x2, g, w_main, w_small, tm=2048, tn=1280):
    n = x2.shape[0]
    tm = min(tm, n)
    kv_off = COL_KV - (MAIN_COLS - tn)
    assert 0 <= kv_off and kv_off + 2 * LANES <= tn
    return pl.pallas_call(
        functools.partial(_inproj_kernel, kv_off=kv_off),
        grid=(n // tm, MAIN_COLS // tn),
        in_specs=[
            pl.BlockSpec((tm, D_MODEL), lambda i, j: (i, 0)),
            pl.BlockSpec((1, D_MODEL), lambda i, j: (0, 0)),
            pl.BlockSpec((D_MODEL, tn), lambda i, j: (0, j)),
            pl.BlockSpec((D_MODEL, SMALL_COLS), lambda i, j: (0, 0)),
        ],
        out_specs=[
            pl.BlockSpec((tm, tn), lambda i, j: (i, j)),
            pl.BlockSpec((tm, SMALL_COLS), lambda i, j: (i, 0)),
            pl.BlockSpec((tm, LANES), lambda i, j: (i, 0)),
            pl.BlockSpec((tm, LANES), lambda i, j: (i, 0)),
        ],
        out_shape=[jax.ShapeDtypeStruct((n, MAIN_COLS), BF16),
                   jax.ShapeDtypeStruct((n, SMALL_COLS), F32),
                   jax.ShapeDtypeStruct((n, LANES), BF16),
                   jax.ShapeDtypeStruct((n, LANES), BF16)],
        scratch_shapes=[pltpu.VMEM((tm, D_MODEL), BF16)],
        compiler_params=_cparams(("parallel", "arbitrary")),
        name="inproj",
    )(x2, g, w_main, w_small)


def _compress_kernel(xk_ref, xv_ref, pek_ref, pev_ref, w1k_ref, w1v_ref,
                     wak_ref, wbk_ref, wav_ref, wbv_ref, w2k_ref, w2v_ref, kc_ref, vc_ref):
    def one(x_ref, pe_ref, w1_ref, wa_ref, wb_ref, w2_ref, out_ref, out_scale):
        x = x_ref[...]
        a = _dot(x, wa_ref[...])
        b = _dot(x, wb_ref[...])
        nc = a.shape[0]
        b_up = pltpu.roll(b, nc - 1, 0)
        pe_c = _dot(pe_ref[...].astype(BF16), w1_ref[...])[0:1]
        pe_c = jnp.concatenate([pe_c, pe_c], axis=1)
        hid = a + b_up + pe_c
        hid = (hid * _sigmoid(hid)).astype(BF16)
        for g in range(NSA_GROUPS):
            out_ref[g] = (_dot(hid, w2_ref[g]) * out_scale).astype(BF16)

    one(xk_ref, pek_ref, w1k_ref, wak_ref, wbk_ref, w2k_ref, kc_ref, SCALE * LOG2E)
    one(xv_ref, pev_ref, w1v_ref, wav_ref, wbv_ref, w2v_ref, vc_ref, 1.0)


def _compress(xk, xv, pek, pev, w1k, w1v, wak, wbk, wav, wbv, w2k, w2v):
    b, nc, kw = xk.shape
    full = lambda a: pl.BlockSpec(a.shape, lambda i: (0,) * a.ndim)
    xs = pl.BlockSpec((None, nc, kw), lambda i: (i, 0, 0))
    os_ = pl.BlockSpec((None, NSA_GROUPS, nc, LANES), lambda i: (i, 0, 0, 0))
    return pl.pallas_call(
        _compress_kernel,
        grid=(b,),
        in_specs=[xs, xs] + [full(a) for a in (pek, pev, w1k, w1v, wak, wbk, wav, wbv, w2k, w2v)],
        out_specs=[os_, os_],
        out_shape=[jax.ShapeDtypeStruct((b, NSA_GROUPS, nc, LANES), BF16)] * 2,
        compiler_params=_cparams(("parallel",)),
        name="compress",
    )(xk, xv, pek, pev, w1k, w1v, wak, wbk, wav, wbv, w2k, w2v)


def _nsaprep_kernel(q_ref, s_ref, w_ref, sm_ref, cos_ref, sin_ref, gmat_ref,
                    qh_ref, ks_ref, kw_ref, vs_ref, vw_ref, ge_ref, *, tt):
    cos = cos_ref[...]
    sin = sin_ref[...]
    lane = _lane_iota((tt, LANES))
    t = pl.program_id(1) * tt + lax.broadcasted_iota(jnp.int32, (tt, LANES), 0)
    low = lane < HEAD_DIM
    for j in range(NSA_HEADS // 2):
        x = _rope_slab(q_ref[:, j * LANES:(j + 1) * LANES].astype(F32), cos, sin) * (SCALE * LOG2E)
        qh_ref[2 * j] = jnp.where(low, x, 0.0).astype(BF16)
        qh_ref[2 * j + 1] = jnp.where(low, pltpu.roll(x, HEAD_DIM, 1), 0.0).astype(BF16)

    onehot = jnp.where((lane >= EXT) & (lane - EXT == t // SEL_BLOCK) & (lane < EXT + N_SEL), 1.0, 0.0)
    ks = _rope_slab(s_ref[:, 0:LANES].astype(F32), cos, sin)
    kw = _rope_slab(w_ref[:, 0:LANES].astype(F32), cos, sin)
    for g in range(NSA_GROUPS):
        ks_ref[g] = jnp.where(low, ks if g == 0 else pltpu.roll(ks, HEAD_DIM, 1), onehot).astype(BF16)
        kw_ref[g] = jnp.where(low, kw if g == 0 else pltpu.roll(kw, HEAD_DIM, 1), 0.0).astype(BF16)

    for src_ref, dst_ref in ((s_ref, vs_ref), (w_ref, vw_ref)):
        v = src_ref[:, LANES:2 * LANES].astype(F32)
        v_sw = pltpu.roll(v, HEAD_DIM, 1)
        dst_ref[0, 0] = jnp.where(low, v, 1.0).astype(BF16)
        dst_ref[0, 1] = jnp.where(low, 1.0, v_sw).astype(BF16)
        dst_ref[1, 0] = jnp.where(low, v_sw, 1.0).astype(BF16)
        dst_ref[1, 1] = jnp.where(low, 1.0, v).astype(BF16)
    gates = _sigmoid(sm_ref[:, 0:LANES]).astype(BF16)
    for br in range(3):
        ge_ref[br] = _dot(gates, gmat_ref[br]).astype(BF16)


def _cmp_kernel(q_ref, kc_ref, vc_ref, ovl_ref, ge_ref, o_ref, nsel_ref, *, tq):
    q0 = pl.program_id(1) * tq
    nc = kc_ref.shape[1]
    lane = _lane_iota((tq, LANES))
    t = q0 + lax.broadcasted_iota(jnp.int32, (tq, LANES), 0)
    cmp_valid = (lane * CMP_STRIDE + CMP_BLOCK - 1 <= t) & (lane < nc)
    row_valid = (t >= CMP_BLOCK - 1)[None]
    cur = t // SEL_BLOCK
    forced = (lane == 0) | (lane == cur) | (lane == cur - 1)
    blk_valid = lane * SEL_BLOCK <= t
    row8 = lax.broadcasted_iota(jnp.int32, (8, tq), 0)
    for g in range(NSA_GROUPS):
        qs = _stack_heads([q_ref[:, (g * 4 + j) * LANES:(g * 4 + j + 1) * LANES] for j in range(4)])
        s3 = _dot_nt(qs, kc_ref[g]).reshape(NSA_HPG, tq, nc)
        s3 = jnp.where(cmp_valid[None], s3, NEG_INF)
        m = jnp.max(s3, axis=-1, keepdims=True)
        e = jnp.exp2(s3 - m)
        l = _dot(e.reshape(NSA_HPG * tq, nc).astype(BF16), jnp.ones((nc, LANES), BF16))
        inv = jnp.where(row_valid, 1.0 / l.reshape(NSA_HPG, tq, LANES), 0.0)
        p = (e * inv).astype(BF16)
        o = _dot(p.reshape(NSA_HPG * tq, nc), vc_ref[g])
        for j, slab in enumerate(_unstack_heads(o, 4, tq)):
            cols = slice((g * 4 + j) * LANES, (g * 4 + j + 1) * LANES)
            o_ref[:, cols] = slab * ge_ref[:, cols].astype(F32)
        imp = _split3_dot(jnp.sum(p.astype(F32), axis=0), ovl_ref[...])
        score = jnp.where(forced, FORCED_SCORE, jnp.where(blk_valid, imp, -1.0))
        score = jnp.where(lane < N_SEL, score, -2.0)
        st = score.T[0:N_SEL, :]
        cnt = [jnp.zeros((8, tq), F32) for _ in range(N_SEL // 8)]
        for j in range(N_SEL):
            rj = st[j:j + 1, :]
            for r in range(N_SEL // 8):
                blk = st[8 * r:8 * r + 8, :]
                if j < 8 * r:
                    beats = rj >= blk
                elif j >= 8 * r + 8:
                    beats = rj > blk
                else:
                    beats = (rj > blk) | ((rj == blk) & (row8 + 8 * r > j))
                cnt[r] = cnt[r] + jnp.where(beats, 1.0, 0.0)
        nsel_t = jnp.where(jnp.concatenate(cnt, axis=0) < float(SEL_TOPK), 0.0, NEG_INF)
        nsel_t = jnp.concatenate([jnp.zeros((EXT, tq), F32), nsel_t,
                                  jnp.zeros((LANES - EXT - N_SEL, tq), F32)], axis=0)
        nsel_ref[g] = nsel_t.T.astype(BF16)


def _nsafront_kernel(q_ref, s_ref, w_ref, sm_ref, cos_ref, sin_ref, gmat_ref, kc_ref, vc_ref, ovl_ref,
                     qh_ref, ks_ref, kw_ref, vs_ref, vw_ref, ge_ref, o_ref, nsel_ref, *, tq):
    _nsaprep_kernel(q_ref, s_ref, w_ref, sm_ref, cos_ref, sin_ref, gmat_ref,
                    qh_ref, ks_ref, kw_ref, vs_ref, vw_ref, ge_ref, tt=tq)
    _cmp_kernel(q_ref, kc_ref, vc_ref, ovl_ref, ge_ref.at[0], o_ref, nsel_ref, tq=tq)


def _nsa_front(main3, small3, cos, sin, gmat, kc, vc, ovl, tq=256):
    b, t, _ = main3.shape
    nc = kc.shape[2]
    blk = COL_KV // (2 * LANES)
    kspec = pl.BlockSpec((None, NSA_GROUPS, tq, LANES), lambda i, j: (i, 0, j, 0))
    vspec = pl.BlockSpec((None, NSA_GROUPS, 2, tq, LANES), lambda i, j: (i, 0, 0, j, 0))
    cspec = pl.BlockSpec((None, NSA_GROUPS, nc, LANES), lambda i, j: (i, 0, 0, 0))
    return pl.pallas_call(
        functools.partial(_nsafront_kernel, tq=tq),
        grid=(b, t // tq),
        in_specs=[
            pl.BlockSpec((None, tq, 1024), lambda i, j: (i, j, COL_QA // 1024)),
            pl.BlockSpec((None, tq, 2 * LANES), lambda i, j: (i, j, blk + 1)),
            pl.BlockSpec((None, tq, 2 * LANES), lambda i, j: (i, j, blk + 2)),
            pl.BlockSpec((None, tq, SMALL_COLS), lambda i, j: (i, j, 0)),
            pl.BlockSpec((tq, LANES), lambda i, j: (j, 0)),
            pl.BlockSpec((tq, LANES), lambda i, j: (j, 0)),
            pl.BlockSpec(gmat.shape, lambda i, j: (0, 0, 0)),
            cspec, cspec,
            pl.BlockSpec(ovl.shape, lambda i, j: (0, 0)),
        ],
        out_specs=[
            pl.BlockSpec((None, NSA_HEADS, tq, LANES), lambda i, j: (i, 0, j, 0)),
            kspec, kspec, vspec, vspec,
            pl.BlockSpec((None, 3, tq, 1024), lambda i, j: (i, 0, j, 0)),
            pl.BlockSpec((None, tq, 1024), lambda i, j: (i, j, 0)),
            kspec,
        ],
        out_shape=[jax.ShapeDtypeStruct((b, NSA_HEADS, t, LANES), BF16)]
        + [jax.ShapeDtypeStruct((b, NSA_GROUPS, t, LANES), BF16)] * 2
        + [jax.ShapeDtypeStruct((b, NSA_GROUPS, 2, t, LANES), BF16)] * 2
        + [jax.ShapeDtypeStruct((b, 3, t, 1024), BF16),
           jax.ShapeDtypeStruct((b, t, 1024), F32),
           jax.ShapeDtypeStruct((b, NSA_GROUPS, t, LANES), BF16)],
        compiler_params=_cparams(("parallel", "parallel")),
        name="nsa_front",
    )(main3, main3, main3, small3, cos, sin, gmat, kc, vc, ovl)


def _tile_lanes(x, n):
    return jnp.concatenate([x] * n, axis=1)


def _lane_block_max(mrun, s):
    for w in range(s.shape[1] // LANES):
        mrun = jnp.maximum(mrun, s[:, w * LANES:(w + 1) * LANES])
    return mrun


def _row_max_tiled(mrun, n):
    return _tile_lanes(jnp.broadcast_to(jnp.max(mrun, axis=1, keepdims=True), mrun.shape), n)


def _emit_pipelined(units, score_stage, value_stage):
    def drain(gens):
        gens = [g for g in gens if g is not None]
        while gens:
            for g in list(gens):
                try:
                    next(g)
                except StopIteration:
                    gens.remove(g)

    drain([score_stage(units[0])])
    for n, u in enumerate(units):
        drain([score_stage(units[n + 1]) if n + 1 < len(units) else None, value_stage(u)])


EVEN_ODD = tuple(range(0, NSA_HPG, 2)) + tuple(range(1, NSA_HPG, 2))


def _group_slabs(acc_even, acc_odd, gates, tq):
    low = _lane_iota((tq, LANES)) < HEAD_DIM
    slabs = []
    for j in range(NSA_HPG // 2):
        e = acc_even[j * tq:(j + 1) * tq]
        o = acc_odd[j * tq:(j + 1) * tq]
        den = pltpu.roll(jnp.where(low, o, e), HEAD_DIM, 1)
        slabs.append(jnp.where(low, e, o) / den * gates[:, j * LANES:(j + 1) * LANES].astype(F32))
    return slabs


def _slc_kernel(qh_ref, k_ref, v_ref, nsel_ref, ge_ref, prev_ref, o_ref, s_ref):
    tq, tk = TQ_NSA, TK_SLC
    rows = NSA_HPG * tq
    t_total = k_ref.shape[0]
    nq = t_total // tq
    order = [j // 2 if j % 2 else nq - 1 - j // 2 for j in range(nq)]
    row_max = {}

    def key_chunks(i):
        c_last = i * tq // tk
        return [(c * tk, tk) for c in range(c_last)] + [(c_last * tk, (i + 1) * tq - c_last * tk)]

    def score_stage(i):
        q0 = i * tq
        sbuf = s_ref.at[order.index(i) % 2]
        nsel = nsel_ref[q0:q0 + tq, :]
        qs = jnp.concatenate([qh_ref[h, q0:q0 + tq, :] + nsel for h in EVEN_ODD], axis=0)
        mrun = jnp.full((rows, LANES), NEG_INF, F32)
        pieces = key_chunks(i)
        for n, (k0, w) in enumerate(pieces):
            s = _dot_nt(qs, k_ref[k0:k0 + w, :])
            if n == len(pieces) - 1:
                key = k0 + lax.broadcasted_iota(jnp.int32, (tq, w), 1)
                qry = q0 + lax.broadcasted_iota(jnp.int32, (tq, w), 0)
                causal = jnp.where(key <= qry, 0.0, NEG_INF)
                s = (s.reshape(NSA_HPG, tq, w) + causal[None]).reshape(rows, w)
            sbuf[n, :, 0:w] = s
            mrun = _lane_block_max(mrun, s)
            yield
        row_max[i] = _row_max_tiled(mrun, tk // LANES)

    def value_stage(i):
        q0 = i * tq
        sbuf = s_ref.at[order.index(i) % 2]
        m = row_max.pop(i)
        acc = [jnp.zeros((rows // 2, LANES), F32)] * 2
        for n, (k0, w) in enumerate(key_chunks(i)):
            p = jnp.exp2(sbuf[n, :, 0:w] - m[:, 0:w]).astype(BF16)
            for e in range(2):
                acc[e] = acc[e] + _dot(p[e * rows // 2:(e + 1) * rows // 2], v_ref[e, k0:k0 + w, :])
            yield
        for j, slab in enumerate(_group_slabs(acc[0], acc[1], ge_ref[q0:q0 + tq, :], tq)):
            cols = slice(j * LANES, (j + 1) * LANES)
            o_ref[q0:q0 + tq, cols] = prev_ref[q0:q0 + tq, cols] + slab

    _emit_pipelined(order, score_stage, value_stage)


def _slc_attn(qh, ks, vs, nsel, ge, prev):
    b, _, t, _ = qh.shape
    gw = NSA_HPG * HEAD_DIM
    kv = pl.BlockSpec((None, None, t, LANES), lambda i, g: (i, g, 0, 0))
    return pl.pallas_call(
        _slc_kernel,
        grid=(b, NSA_GROUPS),
        in_specs=[
            pl.BlockSpec((None, NSA_HPG, t, LANES), lambda i, g: (i, g, 0, 0), pipeline_mode=pl.Buffered(1)),
            kv,
            pl.BlockSpec((None, None, 2, t, LANES), lambda i, g: (i, g, 0, 0, 0)),
            kv,
            pl.BlockSpec((None, None, t, gw), lambda i, g: (i, 1, 0, g)),
            pl.BlockSpec((None, t, gw), lambda i, g: (i, 0, g), pipeline_mode=pl.Buffered(1)),
        ],
        out_specs=pl.BlockSpec((None, t, gw), lambda i, g: (i, 0, g)),
        out_shape=jax.ShapeDtypeStruct((b, t, 1024), F32),
        scratch_shapes=[pltpu.VMEM((2, t // TK_SLC, NSA_HPG * TQ_NSA, TK_SLC), F32)],
        compiler_params=_cparams(("parallel", "parallel")),
        name="slc_attn",
    )(qh, ks, vs, nsel, ge, prev)


def _win_kernel(qh_ref, k_ref, v_ref, ge_ref, prev_ref, z_ref, y_ref, s_ref):
    tq = TQ_NSA
    rows = NSA_HPG * tq
    t_total = k_ref.shape[0]
    qry_l = lax.broadcasted_iota(jnp.int32, (tq, LANES), 0)
    key_l = lax.broadcasted_iota(jnp.int32, (tq, LANES), 1)

    step = 2 * LANES
    row_max = {}

    def window(i):
        k_lo = max(i * tq - WINDOW, 0)
        return k_lo, [(c, min(step, i * tq + tq - k_lo - c)) for c in range(0, i * tq + tq - k_lo, step)]

    def score_stage(i):
        q0 = i * tq
        k_lo, pieces = window(i)
        sbuf = s_ref.at[i % 2]
        qs = jnp.concatenate([qh_ref[h, q0:q0 + tq, :] for h in EVEN_ODD], axis=0)
        mrun = jnp.full((rows, LANES), NEG_INF, F32)
        for c0, cw in pieces:
            s = _dot_nt(qs, k_ref[k_lo + c0:k_lo + c0 + cw, :])
            for w in range(cw // LANES):
                k0 = k_lo + c0 + w * LANES
                blk = s[:, w * LANES:(w + 1) * LANES]
                if k0 == q0 or k0 == q0 - WINDOW:
                    dist = (k0 - q0) + key_l - qry_l
                    band = jnp.where((dist <= 0) & (dist > -WINDOW), 0.0, NEG_INF)
                    blk = (blk.reshape(NSA_HPG, tq, LANES) + band[None]).reshape(rows, LANES)
                sbuf[:, c0 + w * LANES:c0 + (w + 1) * LANES] = blk
                mrun = jnp.maximum(mrun, blk)
            yield
        row_max[i] = jnp.broadcast_to(jnp.max(mrun, axis=1, keepdims=True), mrun.shape)

    def value_stage(i):
        q0 = i * tq
        k_lo, pieces = window(i)
        sbuf = s_ref.at[i % 2]
        m = row_max.pop(i)
        acc = [jnp.zeros((rows // 2, LANES), F32)] * 2
        for c0, cw in pieces:
            p = jnp.exp2(sbuf[:, c0:c0 + cw] - _tile_lanes(m, cw // LANES)).astype(BF16)
            for e in range(2):
                acc[e] = acc[e] + _dot(p[e * rows // 2:(e + 1) * rows // 2], v_ref[e, k_lo + c0:k_lo + c0 + cw, :])
            yield
        for j, slab in enumerate(_group_slabs(acc[0], acc[1], ge_ref[q0:q0 + tq, :], tq)):
            cols = slice(j * LANES, (j + 1) * LANES)
            z = z_ref[q0:q0 + tq, cols].astype(F32)
            y = (prev_ref[q0:q0 + tq, cols] + slab) * (z * _sigmoid(z))
            y_ref[q0:q0 + tq, cols] = y.astype(BF16)

    _emit_pipelined(list(range(t_total // tq)), score_stage, value_stage)


def _win_attn(qh, kw, vw, ge, prev, main3):
    b, _, t, _ = qh.shape
    gw = NSA_HPG * HEAD_DIM
    kv = pl.BlockSpec((None, None, t, LANES), lambda i, g: (i, g, 0, 0))
    return pl.pallas_call(
        _win_kernel,
        grid=(b, NSA_GROUPS),
        in_specs=[
            pl.BlockSpec((None, NSA_HPG, t, LANES), lambda i, g: (i, g, 0, 0)),
            kv,
            pl.BlockSpec((None, None, 2, t, LANES), lambda i, g: (i, g, 0, 0, 0)),
            pl.BlockSpec((None, None, t, gw), lambda i, g: (i, 2, 0, g)),
            pl.BlockSpec((None, t, gw), lambda i, g: (i, 0, g)),
            pl.BlockSpec((None, t, gw), lambda i, g: (i, 0, COL_ZA // gw + g)),
        ],
        out_specs=pl.BlockSpec((None, t, gw), lambda i, g: (i, 0, g)),
        out_shape=jax.ShapeDtypeStruct((b, t, 1024), BF16),
        scratch_shapes=[pltpu.VMEM((2, NSA_HPG * TQ_NSA, WINDOW + TQ_NSA), F32)],
        compiler_params=_cparams(("parallel", "parallel")),
        name="win_attn",
    )(qh, kw, vw, ge, prev, main3)


def _decay_kernel(sm_ref, b_ref, f_ref):
    t_total = sm_ref.shape[0]
    x = sm_ref[:, LANES:2 * LANES] + b_ref[...]
    f = jnp.minimum(x, 0.0) - jnp.log1p(jnp.exp(-jnp.abs(x)))
    row = lax.broadcasted_iota(jnp.int32, f.shape, 0)
    sh = 1
    while sh < t_total:
        f = f + jnp.where(row >= sh, pltpu.roll(f, sh, 0), 0.0)
        sh *= 2
    hi, mid, lo = _split3(f * (-LOG2E))
    lane = _lane_iota(f.shape)
    pieces = jnp.where(lane < FOX_HEADS, hi,
                       jnp.where(lane < 2 * FOX_HEADS, pltpu.roll(mid, FOX_HEADS, 1),
                                 jnp.where(lane < 3 * FOX_HEADS, pltpu.roll(lo, 2 * FOX_HEADS, 1), 0.0)))
    f_ref[...] = pieces.astype(BF16)


def _decay(small3, b_pad):
    b, t, _ = small3.shape
    return pl.pallas_call(
        _decay_kernel,
        grid=(b,),
        in_specs=[pl.BlockSpec((None, t, SMALL_COLS), lambda i: (i, 0, 0)),
                  pl.BlockSpec((1, LANES), lambda i: (0, 0))],
        out_specs=pl.BlockSpec((None, t, LANES), lambda i: (i, 0, 0)),
        out_shape=jax.ShapeDtypeStruct((b, t, LANES), BF16),
        compiler_params=_cparams(("parallel",)),
        name="fox_decay",
    )(small3, b_pad)


def _fox_kernel(q_ref, k_ref, v_ref, f_ref, z_ref, y_ref, qf_ref, kf_ref, vf_ref, s_ref):
    tq = TQ_FOX
    t_total = z_ref.shape[0]
    slab_idx = pl.program_id(1)
    causal = (lax.broadcasted_iota(jnp.int32, (tq, tq), 0) >= lax.broadcasted_iota(jnp.int32, (tq, tq), 1))
    lane = _lane_iota((tq, LANES))

    src = lax.broadcasted_iota(jnp.int32, (LANES, LANES), 0)
    dst = lax.broadcasted_iota(jnp.int32, (LANES, LANES), 1)
    for e in range(2):
        own = (lane < HEAD_DIM) if e == 0 else (lane >= HEAD_DIM)
        ext0 = (1 - e) * HEAD_DIM
        head = 2 * slab_idx + e
        place = sum(jnp.where((src == j * FOX_HEADS + head) & (dst == ext0 + j), 1.0, 0.0)
                    for j in range(3)).astype(BF16)
        q_ext = jnp.where((lane >= ext0) & (lane < ext0 + 3), 1.0, 0.0).astype(BF16)
        for r0 in range(0, t_total, tq):
            rs = slice(r0, r0 + tq)
            k_ext = _dot(f_ref[rs, :], place).astype(BF16)
            qf_ref[e, rs, :] = jnp.where(own, q_ref[rs, :], q_ext)
            kf_ref[e, rs, :] = jnp.where(own, k_ref[rs, :], k_ext)
            vf_ref[e, rs, :] = jnp.where(own, v_ref[rs, :], jnp.ones((tq, LANES), BF16))

    nq = t_total // tq
    units = [j // 2 if j % 2 else nq - 1 - j // 2 for j in range(nq)]
    row_max = {}

    def score_stage(i):
        sbuf = s_ref.at[units.index(i) % 2]
        qs = [qf_ref[h, i * tq:(i + 1) * tq, :] for h in range(2)]
        mrun = [jnp.full((tq, LANES), NEG_INF, F32)] * 2
        for c in range(i + 1):
            for h in range(2):
                s = _dot_nt(qs[h], kf_ref[h, c * tq:(c + 1) * tq, :])
                if c == i:
                    s = jnp.where(causal, s, NEG_INF)
                sbuf[h, c] = s
                mrun[h] = _lane_block_max(mrun[h], s)
            yield
        row_max[i] = [_row_max_tiled(mrun[h], tq // LANES) for h in range(2)]

    def value_stage(i):
        sbuf = s_ref.at[units.index(i) % 2]
        m = row_max.pop(i)
        acc = [jnp.zeros((tq, LANES), F32)] * 2
        for c in range(i + 1):
            for h in range(2):
                p = jnp.exp2(sbuf[h, c] - m[h]).astype(BF16)
                acc[h] = acc[h] + _dot(p, vf_ref[h, c * tq:(c + 1) * tq, :])
            yield
        low = lane < HEAD_DIM
        slab = jnp.where(low, acc[0], acc[1]) / pltpu.roll(jnp.where(low, acc[1], acc[0]), HEAD_DIM, 1)
        z = z_ref[i * tq:(i + 1) * tq, :].astype(F32)
        y_ref[i * tq:(i + 1) * tq, :] = (slab * (z * _sigmoid(z))).astype(BF16)

    _emit_pipelined(units, score_stage, value_stage)


def _fox_attn(main3, f3):
    b, t, _ = main3.shape
    nslab = FOX_HEADS // 2

    def col(base):
        return pl.BlockSpec((None, t, LANES), lambda i, p: (i, 0, base // LANES + p))

    return pl.pallas_call(
        _fox_kernel,
        grid=(b, nslab),
        in_specs=[col(COL_QB), col(COL_KB), col(COL_VB),
                  pl.BlockSpec((None, t, LANES), lambda i, p: (i, 0, 0)),
                  col(COL_ZB)],
        out_specs=pl.BlockSpec((None, t, LANES), lambda i, p: (i, 0, p)),
        out_shape=jax.ShapeDtypeStruct((b, t, 1024), BF16),
        scratch_shapes=[pltpu.VMEM((2, t, LANES), BF16), pltpu.VMEM((2, t, LANES), BF16),
                        pltpu.VMEM((2, t, LANES), BF16),
                        pltpu.VMEM((2, 2, t // TQ_FOX, TQ_FOX, TQ_FOX), F32)],
        compiler_params=_cparams(("parallel", "parallel")),
        name="fox_attn",
    )(main3, main3, main3, f3, main3)


def _out_kernel(x_ref, ya_ref, yb_ref, ra_ref, rb_ref, wn_ref, wf_ref, wo_ref, fg_ref, o_ref, *, final):
    ta = _dot(ya_ref[...], wn_ref[...])
    tb = _dot(yb_ref[...], wf_ref[...])
    merged = _sigmoid(ra_ref[...].astype(F32)) * ta + _sigmoid(rb_ref[...].astype(F32)) * tb
    out = x_ref[...] + _dot(merged.astype(BF16), wo_ref[...])
    if final:
        out = out * lax.rsqrt(jnp.mean(out * out, axis=-1, keepdims=True) + RMS_EPS) * fg_ref[...]
    o_ref[...] = out


def _out_proj(x2, ya2, yb2, main2, wn, wf, wo, fg, final, tm=1024):
    n = x2.shape[0]
    tm = min(tm, n)
    row = lambda c: pl.BlockSpec((tm, 1024), lambda i: (i, c))
    w = pl.BlockSpec((1024, 1024), lambda i: (0, 0))
    return pl.pallas_call(
        functools.partial(_out_kernel, final=final),
        grid=(n // tm,),
        in_specs=[row(0), row(0), row(0), row(COL_RA // 1024), row(COL_RB // 1024), w, w, w,
                  pl.BlockSpec((1, 1024), lambda i: (0, 0))],
        out_specs=row(0),
        out_shape=jax.ShapeDtypeStruct((n, 1024), F32),
        compiler_params=_cparams(("parallel",)),
        name="out_proj",
    )(x2, ya2, yb2, main2, main2, wn, wf, wo, fg)


def _rope_tables(t):
    inv = ROPE_THETA ** (-jnp.arange(0, HEAD_DIM, 2, dtype=F32) / HEAD_DIM)
    ang = jnp.arange(t, dtype=F32)[:, None] * inv[None, :]
    c, s = jnp.cos(ang), jnp.sin(ang)
    return jnp.concatenate([c, c, c, c], axis=1), jnp.concatenate([-s, s, -s, s], axis=1)


def _overlap_matrix():
    n = np.arange(LANES)[:, None] * CMP_STRIDE
    j = np.arange(LANES)[None, :] * SEL_BLOCK
    m = (n < j + SEL_BLOCK) & (n + CMP_BLOCK > j) & (np.arange(LANES)[None, :] < N_SEL)
    return jnp.asarray(m.astype(np.float32), dtype=BF16)


def _gate_expansion():
    m = np.zeros((3, LANES, NSA_HEADS * HEAD_DIM), np.float32)
    for br in range(3):
        for h in range(NSA_HEADS):
            m[br, br * NSA_HEADS + h, h * HEAD_DIM:(h + 1) * HEAD_DIM] = 1.0
    return jnp.asarray(m, dtype=BF16)


def _reorder_w_in(w):
    qa, kv, ga, za, qb, kb, vb, fb, zb, mg = jnp.split(
        w, np.cumsum([1024, 768, 48, 1024, 1024, 1024, 1024, 16, 1024])[:9].tolist(), axis=1)
    main = jnp.concatenate([qa, za, qb * (SCALE * LOG2E), kb, vb, zb, mg, kv], axis=1).astype(BF16)
    d = w.shape[0]
    small = jnp.concatenate([ga, jnp.zeros((d, LANES - 48), F32), fb, jnp.zeros((d, LANES - 16), F32)],
                            axis=1).astype(BF16)
    return main, small


def _compress_weights(w1, w2):
    eye = jnp.eye(NSA_GROUPS, dtype=F32)
    w1r = w1.reshape(CMP_BLOCK, HEAD_DIM, CMP_HIDDEN)
    half = CMP_BLOCK // 2

    def blockdiag(wl):
        return jnp.einsum('ldh,gk->lgdkh', wl, eye).reshape(half * NSA_GROUPS * HEAD_DIM,
                                                            NSA_GROUPS * CMP_HIDDEN).astype(BF16)

    wa, wb = blockdiag(w1r[:half]), blockdiag(w1r[half:])
    w2d = jnp.einsum('hd,gk,c->gkhcd', w2, eye, jnp.ones((2,), F32)).reshape(
        NSA_GROUPS, NSA_GROUPS * CMP_HIDDEN, LANES).astype(BF16)
    return wa, wb, w2d


def kernel(x, norm_g, w_in, b_forget, cmp_pe_k, cmp_w1_k, cmp_w2_k, cmp_pe_v, cmp_w1_v, cmp_w2_v,
           w_proj_nsa, w_proj_fox, w_out, final_g):
    b, t, d = x.shape
    depth = norm_g.shape[0]
    n = b * t
    nc = t // CMP_STRIDE
    assert d == D_MODEL and nc == LANES and t // SEL_BLOCK == N_SEL
    cos, sin = _rope_tables(t)
    ovl = _overlap_matrix()
    gmat = _gate_expansion()
    fg = final_g.reshape(1, d)
    x2 = x.reshape(n, d)
    for l in range(depth):
        w_main, w_small = _reorder_w_in(w_in[l])
        main2, small2, kc2, vc2 = _inproj(x2, norm_g[l].reshape(1, d), w_main, w_small)
        main3 = main2.reshape(b, t, MAIN_COLS)
        small3 = small2.reshape(b, t, SMALL_COLS)
        xk = kc2.reshape(b, nc, CMP_STRIDE * LANES)
        xv = vc2.reshape(b, nc, CMP_STRIDE * LANES)
        wak, wbk, w2k = _compress_weights(cmp_w1_k[l], cmp_w2_k[l])
        wav, wbv, w2v = _compress_weights(cmp_w1_v[l], cmp_w2_v[l])
        pek = jnp.broadcast_to(cmp_pe_k[l].reshape(1, -1), (8, CMP_BLOCK * HEAD_DIM))
        pev = jnp.broadcast_to(cmp_pe_v[l].reshape(1, -1), (8, CMP_BLOCK * HEAD_DIM))
        kc, vc = _compress(xk, xv, pek, pev, cmp_w1_k[l].astype(BF16), cmp_w1_v[l].astype(BF16),
                           wak, wbk, wav, wbv, w2k, w2v)
        qh, ks, kw, vs, vw, ge, o_cmp, nsel = _nsa_front(main3, small3, cos, sin, gmat, kc, vc, ovl)
        o_cs = _slc_attn(qh, ks, vs, nsel, ge, o_cmp)
        y_a = _win_attn(qh, kw, vw, ge, o_cs, main3)
        b_pad = jnp.concatenate([b_forget[l], jnp.zeros((LANES - FOX_HEADS,), F32)]).reshape(1, LANES)
        y_b = _fox_attn(main3, _decay(small3, b_pad))
        x2 = _out_proj(x2, y_a.reshape(n, 1024), y_b.reshape(n, 1024), main2,
                       w_proj_nsa[l].astype(BF16), w_proj_fox[l].astype(BF16), w_out[l].astype(BF16),
                       fg, final=(l == depth - 1))
    return x2.reshape(b, t, d)
```

```python
import functools

import numpy as np
import jax
import jax.numpy as jnp
from jax import lax
from jax.experimental import pallas as pl
from jax.experimental.pallas import tpu as pltpu

F32 = jnp.float32
BF16 = jnp.bfloat16

D_MODEL = 1024
HEAD_DIM = 64
LANES = 128
NSA_HEADS = 16
NSA_GROUPS = 2
NSA_HPG = NSA_HEADS // NSA_GROUPS
CMP_BLOCK = 32
CMP_STRIDE = 16
CMP_HIDDEN = 2 * HEAD_DIM
SEL_BLOCK = 64
SEL_TOPK = 8
N_SEL = 32
WINDOW = 512
FOX_HEADS = 16
ROPE_THETA = 10000.0
RMS_EPS = 1e-6
NEG_INF = -1e30
FORCED_SCORE = 1e4
SCALE = HEAD_DIM ** -0.5
LOG2E = 1.4426950408889634
EXT = HEAD_DIM

COL_QA, COL_ZA, COL_QB, COL_KB, COL_VB, COL_ZB, COL_RA, COL_RB, COL_KV = (
    0, 1024, 2048, 3072, 4096, 5120, 6144, 7168, 8192)
MAIN_COLS = 8960
SMALL_COLS = 256

TQ_NSA = 128
TK_SLC = 256
TQ_FOX = 256
VMEM_LIMIT = 56 * 1024 * 1024


def _cparams(sem):
    return pltpu.CompilerParams(dimension_semantics=sem, vmem_limit_bytes=VMEM_LIMIT)


def _lane_iota(shape):
    return lax.broadcasted_iota(jnp.int32, shape, len(shape) - 1)


def _dot_nt(a, b):
    return lax.dot_general(a, b, (((1,), (1,)), ((), ())), preferred_element_type=F32)


def _dot(a, b):
    return jnp.dot(a, b, preferred_element_type=F32)


def _split3(x):
    hi = x.astype(BF16).astype(F32)
    r = x - hi
    mid = r.astype(BF16).astype(F32)
    lo = (r - mid).astype(BF16).astype(F32)
    return hi, mid, lo


def _split3_dot(a, m):
    hi, mid, lo = _split3(a)
    return _dot(hi.astype(BF16), m) + _dot(mid.astype(BF16), m) + _dot(lo.astype(BF16), m)


def _sigmoid(x):
    return 1.0 / (1.0 + jnp.exp(-x))


def _rope_slab(x, cos, sin):
    lane = _lane_iota(x.shape)
    swapped = jnp.where((lane % HEAD_DIM) < HEAD_DIM // 2,
                        pltpu.roll(x, LANES - HEAD_DIM // 2, 1), pltpu.roll(x, HEAD_DIM // 2, 1))
    return x * cos + swapped * sin


def _stack_heads(slabs):
    lane = _lane_iota(slabs[0].shape)
    parts = []
    for s in slabs:
        parts.append(jnp.where(lane < HEAD_DIM, s, jnp.zeros_like(s)))
        parts.append(jnp.where(lane >= HEAD_DIM, s, jnp.zeros_like(s)))
    return jnp.concatenate(parts, axis=0)


def _unstack_heads(o, nslab, rows):
    lane = _lane_iota((rows, LANES))
    out = []
    for j in range(nslab):
        a = o[(2 * j) * rows:(2 * j + 1) * rows]
        b = o[(2 * j + 1) * rows:(2 * j + 2) * rows]
        out.append(jnp.where(lane < HEAD_DIM, a, b))
    return out


def _inproj_kernel(x_ref, g_ref, w_ref, ws_ref, main_ref, small_ref, kc_ref, vc_ref, xn_ref, *, kv_off):
    @pl.when(pl.program_id(1) == 0)
    def _():
        x = x_ref[...]
        y = x * lax.rsqrt(jnp.mean(x * x, axis=-1, keepdims=True) + RMS_EPS) * g_ref[...]
        xn = y.astype(BF16)
        xn_ref[...] = xn
        small_ref[...] = _dot(xn, ws_ref[...])

    res = _dot(xn_ref[...], w_ref[...]).astype(BF16)
    main_ref[...] = res

    @pl.when(pl.program_id(1) == pl.num_programs(1) - 1)
    def _():
        kc_ref[...] = res[:, kv_off:kv_off + LANES]
        vc_ref[...] = res[:, kv_off + LANES:kv_off + 2 * LANES]


def _inproj(x2, g, w_main, w_small, tm=2048, tn=1280):
    n = x2.shape[0]
    tm = min(tm, n)
    kv_off = COL_KV - (MAIN_COLS - tn)
    assert 0 <= kv_off and kv_off + 2 * LANES <= tn
    return pl.pallas_call(
        functools.partial(_inproj_kernel, kv_off=kv_off),
        grid=(n // tm, MAIN_COLS // tn),
        in_specs=[
            pl.BlockSpec((tm, D_MODEL), lambda i, j: (i, 0)),
            pl.BlockSpec((1, D_MODEL), lambda i, j: (0, 0)),
            pl.BlockSpec((D_MODEL, tn), lambda i, j: (0, j)),
            pl.BlockSpec((D_MODEL, SMALL_COLS), lambda i, j: (0, 0)),
        ],
        out_specs=[
            pl.BlockSpec((tm, tn), lambda i, j: (i, j)),
            pl.BlockSpec((tm, SMALL_COLS), lambda i, j: (i, 0)),
            pl.BlockSpec((tm, LANES), lambda i, j: (i, 0)),
            pl.BlockSpec((tm, LANES), lambda i, j: (i, 0)),
        ],
        out_shape=[jax.ShapeDtypeStruct((n, MAIN_COLS), BF16),
                   jax.ShapeDtypeStruct((n, SMALL_COLS), F32),
                   jax.ShapeDtypeStruct((n, LANES), BF16),
                   jax.ShapeDtypeStruct((n, LANES), BF16)],
        scratch_shapes=[pltpu.VMEM((tm, D_MODEL), BF16)],
        compiler_params=_cparams(("parallel", "arbitrary")),
        name="inproj",
    )(x2, g, w_main, w_small)


def _compress_kernel(xk_ref, xv_ref, pek_ref, pev_ref, w1k_ref, w1v_ref,
                     wak_ref, wbk_ref, wav_ref, wbv_ref, w2k_ref, w2v_ref, kc_ref, vc_ref):
    def one(x_ref, pe_ref, w1_ref, wa_ref, wb_ref, w2_ref, out_ref, out_scale):
        x = x_ref[...]
        a = _dot(x, wa_ref[...])
        b = _dot(x, wb_ref[...])
        nc = a.shape[0]
        b_up = pltpu.roll(b, nc - 1, 0)
        pe_c = _dot(pe_ref[...].astype(BF16), w1_ref[...])[0:1]
        pe_c = jnp.concatenate([pe_c, pe_c], axis=1)
        hid = a + b_up + pe_c
        hid = (hid * _sigmoid(hid)).astype(BF16)
        for g in range(NSA_GROUPS):
            out_ref[g] = (_dot(hid, w2_ref[g]) * out_scale).astype(BF16)

    one(xk_ref, pek_ref, w1k_ref, wak_ref, wbk_ref, w2k_ref, kc_ref, SCALE * LOG2E)
    one(xv_ref, pev_ref, w1v_ref, wav_ref, wbv_ref, w2v_ref, vc_ref, 1.0)


def _compress(xk, xv, pek, pev, w1k, w1v, wak, wbk, wav, wbv, w2k, w2v):
    b, nc, kw = xk.shape
    full = lambda a: pl.BlockSpec(a.shape, lambda i: (0,) * a.ndim)
    xs = pl.BlockSpec((None, nc, kw), lambda i: (i, 0, 0))
    os_ = pl.BlockSpec((None, NSA_GROUPS, nc, LANES), lambda i: (i, 0, 0, 0))
    return pl.pallas_call(
        _compress_kernel,
        grid=(b,),
        in_specs=[xs, xs] + [full(a) for a in (pek, pev, w1k, w1v, wak, wbk, wav, wbv, w2k, w2v)],
        out_specs=[os_, os_],
        out_shape=[jax.ShapeDtypeStruct((b, NSA_GROUPS, nc, LANES), BF16)] * 2,
        compiler_params=_cparams(("parallel",)),
        name="compress",
    )(xk, xv, pek, pev, w1k, w1v, wak, wbk, wav, wbv, w2k, w2v)


def _nsaprep_kernel(q_ref, s_ref, w_ref, sm_ref, cos_ref, sin_ref, gmat_ref,
                    qh_ref, ks_ref, kw_ref, vs_ref, vw_ref, ge_ref, *, tt):
    cos = cos_ref[...]
    sin = sin_ref[...]
    lane = _lane_iota((tt, LANES))
    t = pl.program_id(1) * tt + lax.broadcasted_iota(jnp.int32, (tt, LANES), 0)
    low = lane < HEAD_DIM
    for j in range(NSA_HEADS // 2):
        x = _rope_slab(q_ref[:, j * LANES:(j + 1) * LANES].astype(F32), cos, sin) * (SCALE * LOG2E)
        qh_ref[2 * j] = jnp.where(low, x, 0.0).astype(BF16)
        qh_ref[2 * j + 1] = jnp.where(low, pltpu.roll(x, HEAD_DIM, 1), 0.0).astype(BF16)

    onehot = jnp.where((lane >= EXT) & (lane - EXT == t // SEL_BLOCK) & (lane < EXT + N_SEL), 1.0, 0.0)
    ks = _rope_slab(s_ref[:, 0:LANES].astype(F32), cos, sin)
    kw = _rope_slab(w_ref[:, 0:LANES].astype(F32), cos, sin)
    for g in range(NSA_GROUPS):
        ks_ref[g] = jnp.where(low, ks if g == 0 else pltpu.roll(ks, HEAD_DIM, 1), onehot).astype(BF16)
        kw_ref[g] = jnp.where(low, kw if g == 0 else pltpu.roll(kw, HEAD_DIM, 1), 0.0).astype(BF16)

    for src_ref, dst_ref in ((s_ref, vs_ref), (w_ref, vw_ref)):
        v = src_ref[:, LANES:2 * LANES].astype(F32)
        v_sw = pltpu.roll(v, HEAD_DIM, 1)
        dst_ref[0, 0] = jnp.where(low, v, 1.0).astype(BF16)
        dst_ref[0, 1] = jnp.where(low, 1.0, v_sw).astype(BF16)
        dst_ref[1, 0] = jnp.where(low, v_sw, 1.0).astype(BF16)
        dst_ref[1, 1] = jnp.where(low, 1.0, v).astype(BF16)
    gates = _sigmoid(sm_ref[:, 0:LANES]).astype(BF16)
    for br in range(3):
        ge_ref[br] = _dot(gates, gmat_ref[br]).astype(BF16)


def _cmp_kernel(q_ref, kc_ref, vc_ref, ovl_ref, ge_ref, o_ref, nsel_ref, *, tq):
    q0 = pl.program_id(1) * tq
    nc = kc_ref.shape[1]
    lane = _lane_iota((tq, LANES))
    t = q0 + lax.broadcasted_iota(jnp.int32, (tq, LANES), 0)
    cmp_valid = (lane * CMP_STRIDE + CMP_BLOCK - 1 <= t) & (lane < nc)
    row_valid = (t >= CMP_BLOCK - 1)[None]
    cur = t // SEL_BLOCK
    forced = (lane == 0) | (lane == cur) | (lane == cur - 1)
    blk_valid = lane * SEL_BLOCK <= t
    row8 = lax.broadcasted_iota(jnp.int32, (8, tq), 0)
    for g in range(NSA_GROUPS):
        qs = _stack_heads([q_ref[:, (g * 4 + j) * LANES:(g * 4 + j + 1) * LANES] for j in range(4)])
        s3 = _dot_nt(qs, kc_ref[g]).reshape(NSA_HPG, tq, nc)
        s3 = jnp.where(cmp_valid[None], s3, NEG_INF)
        m = jnp.max(s3, axis=-1, keepdims=True)
        e = jnp.exp2(s3 - m)
        l = _dot(e.reshape(NSA_HPG * tq, nc).astype(BF16), jnp.ones((nc, LANES), BF16))
        inv = jnp.where(row_valid, 1.0 / l.reshape(NSA_HPG, tq, LANES), 0.0)
        p = (e * inv).astype(BF16)
        o = _dot(p.reshape(NSA_HPG * tq, nc), vc_ref[g])
        for j, slab in enumerate(_unstack_heads(o, 4, tq)):
            cols = slice((g * 4 + j) * LANES, (g * 4 + j + 1) * LANES)
            o_ref[:, cols] = slab * ge_ref[:, cols].astype(F32)
        imp = _split3_dot(jnp.sum(p.astype(F32), axis=0), ovl_ref[...])
        score = jnp.where(forced, FORCED_SCORE, jnp.where(blk_valid, imp, -1.0))
        score = jnp.where(lane < N_SEL, score, -2.0)
        st = score.T[0:N_SEL, :]
        cnt = [jnp.zeros((8, tq), F32) for _ in range(N_SEL // 8)]
        for j in range(N_SEL):
            rj = st[j:j + 1, :]
            for r in range(N_SEL // 8):
                blk = st[8 * r:8 * r + 8, :]
                if j < 8 * r:
                    beats = rj >= blk
                elif j >= 8 * r + 8:
                    beats = rj > blk
                else:
                    beats = (rj > blk) | ((rj == blk) & (row8 + 8 * r > j))
                cnt[r] = cnt[r] + jnp.where(beats, 1.0, 0.0)
        nsel_t = jnp.where(jnp.concatenate(cnt, axis=0) < float(SEL_TOPK), 0.0, NEG_INF)
        nsel_t = jnp.concatenate([jnp.zeros((EXT, tq), F32), nsel_t,
                                  jnp.zeros((LANES - EXT - N_SEL, tq), F32)], axis=0)
        nsel_ref[g] = nsel_t.T.astype(BF16)


def _nsafront_kernel(q_ref, s_ref, w_ref, sm_ref, cos_ref, sin_ref, gmat_ref, kc_ref, vc_ref, ovl_ref,
                     qh_ref, ks_ref, kw_ref, vs_ref, vw_ref, ge_ref, o_ref, nsel_ref, *, tq):
    _nsaprep_kernel(q_ref, s_ref, w_ref, sm_ref, cos_ref, sin_ref, gmat_ref,
                    qh_ref, ks_ref, kw_ref, vs_ref, vw_ref, ge_ref, tt=tq)
    _cmp_kernel(q_ref, kc_ref, vc_ref, ovl_ref, ge_ref.at[0], o_ref, nsel_ref, tq=tq)


def _nsa_front(main3, small3, cos, sin, gmat, kc, vc, ovl, tq=256):
    b, t, _ = main3.shape
    nc = kc.shape[2]
    blk = COL_KV // (2 * LANES)
    kspec = pl.BlockSpec((None, NSA_GROUPS, tq, LANES), lambda i, j: (i, 0, j, 0))
    vspec = pl.BlockSpec((None, NSA_GROUPS, 2, tq, LANES), lambda i, j: (i, 0, 0, j, 0))
    cspec = pl.BlockSpec((None, NSA_GROUPS, nc, LANES), lambda i, j: (i, 0, 0, 0))
    return pl.pallas_call(
        functools.partial(_nsafront_kernel, tq=tq),
        grid=(b, t // tq),
        in_specs=[
            pl.BlockSpec((None, tq, 1024), lambda i, j: (i, j, COL_QA // 1024)),
            pl.BlockSpec((None, tq, 2 * LANES), lambda i, j: (i, j, blk + 1)),
            pl.BlockSpec((None, tq, 2 * LANES), lambda i, j: (i, j, blk + 2)),
            pl.BlockSpec((None, tq, SMALL_COLS), lambda i, j: (i, j, 0)),
            pl.BlockSpec((tq, LANES), lambda i, j: (j, 0)),
            pl.BlockSpec((tq, LANES), lambda i, j: (j, 0)),
            pl.BlockSpec(gmat.shape, lambda i, j: (0, 0, 0)),
            cspec, cspec,
            pl.BlockSpec(ovl.shape, lambda i, j: (0, 0)),
        ],
        out_specs=[
            pl.BlockSpec((None, NSA_HEADS, tq, LANES), lambda i, j: (i, 0, j, 0)),
            kspec, kspec, vspec, vspec,
            pl.BlockSpec((None, 3, tq, 1024), lambda i, j: (i, 0, j, 0)),
            pl.BlockSpec((None, tq, 1024), lambda i, j: (i, j, 0)),
            kspec,
        ],
        out_shape=[jax.ShapeDtypeStruct((b, NSA_HEADS, t, LANES), BF16)]
        + [jax.ShapeDtypeStruct((b, NSA_GROUPS, t, LANES), BF16)] * 2
        + [jax.ShapeDtypeStruct((b, NSA_GROUPS, 2, t, LANES), BF16)] * 2
        + [jax.ShapeDtypeStruct((b, 3, t, 1024), BF16),
           jax.ShapeDtypeStruct((b, t, 1024), F32),
           jax.ShapeDtypeStruct((b, NSA_GROUPS, t, LANES), BF16)],
        compiler_params=_cparams(("parallel", "parallel")),
        name="nsa_front",
    )(main3, main3, main3, small3, cos, sin, gmat, kc, vc, ovl)


def _tile_lanes(x, n):
    return jnp.concatenate([x] * n, axis=1)


def _lane_block_max(mrun, s):
    for w in range(s.shape[1] // LANES):
        mrun = jnp.maximum(mrun, s[:, w * LANES:(w + 1) * LANES])
    return mrun


def _row_max_tiled(mrun, n):
    return _tile_lanes(jnp.broadcast_to(jnp.max(mrun, axis=1, keepdims=True), mrun.shape), n)


def _emit_pipelined(units, score_stage, value_stage):
    def drain(gens):
        gens = [g for g in gens if g is not None]
        while gens:
            for g in list(gens):
                try:
                    next(g)
                except StopIteration:
                    gens.remove(g)

    drain([score_stage(units[0])])
    for n, u in enumerate(units):
        drain([score_stage(units[n + 1]) if n + 1 < len(units) else None, value_stage(u)])


EVEN_ODD = tuple(range(0, NSA_HPG, 2)) + tuple(range(1, NSA_HPG, 2))


def _group_slabs(acc_even, acc_odd, gates, tq):
    low = _lane_iota((tq, LANES)) < HEAD_DIM
    slabs = []
    for j in range(NSA_HPG // 2):
        e = acc_even[j * tq:(j + 1) * tq]
        o = acc_odd[j * tq:(j + 1) * tq]
        den = pltpu.roll(jnp.where(low, o, e), HEAD_DIM, 1)
        slabs.append(jnp.where(low, e, o) / den * gates[:, j * LANES:(j + 1) * LANES].astype(F32))
    return slabs


def _slc_kernel(qh_ref, k_ref, v_ref, nsel_ref, ge_ref, prev_ref, o_ref, s_ref):
    tq, tk = TQ_NSA, TK_SLC
    rows = NSA_HPG * tq
    t_total = k_ref.shape[0]
    nq = t_total // tq
    order = [j // 2 if j % 2 else nq - 1 - j // 2 for j in range(nq)]
    row_max = {}

    def key_chunks(i):
        c_last = i * tq // tk
        return [(c * tk, tk) for c in range(c_last)] + [(c_last * tk, (i + 1) * tq - c_last * tk)]

    def score_stage(i):
        q0 = i * tq
        sbuf = s_ref.at[order.index(i) % 2]
        nsel = nsel_ref[q0:q0 + tq, :]
        qs = jnp.concatenate([qh_ref[h, q0:q0 + tq, :] + nsel for h in EVEN_ODD], axis=0)
        mrun = jnp.full((rows, LANES), NEG_INF, F32)
        pieces = key_chunks(i)
        for n, (k0, w) in enumerate(pieces):
            s = _dot_nt(qs, k_ref[k0:k0 + w, :])
            if n == len(pieces) - 1:
                key = k0 + lax.broadcasted_iota(jnp.int32, (tq, w), 1)
                qry = q0 + lax.broadcasted_iota(jnp.int32, (tq, w), 0)
                causal = jnp.where(key <= qry, 0.0, NEG_INF)
                s = (s.reshape(NSA_HPG, tq, w) + causal[None]).reshape(rows, w)
            sbuf[n, :, 0:w] = s
            mrun = _lane_block_max(mrun, s)
            yield
        row_max[i] = _row_max_tiled(mrun, tk // LANES)

    def value_stage(i):
        q0 = i * tq
        sbuf = s_ref.at[order.index(i) % 2]
        m = row_max.pop(i)
        acc = [jnp.zeros((rows // 2, LANES), F32)] * 2
        for n, (k0, w) in enumerate(key_chunks(i)):
            p = jnp.exp2(sbuf[n, :, 0:w] - m[:, 0:w]).astype(BF16)
            for e in range(2):
                acc[e] = acc[e] + _dot(p[e * rows // 2:(e + 1) * rows // 2], v_ref[e, k0:k0 + w, :])
            yield
        for j, slab in enumerate(_group_slabs(acc[0], acc[1], ge_ref[q0:q0 + tq, :], tq)):
            cols = slice(j * LANES, (j + 1) * LANES)
            o_ref[q0:q0 + tq, cols] = prev_ref[q0:q0 + tq, cols] + slab

    _emit_pipelined(order, score_stage, value_stage)


def _slc_attn(qh, ks, vs, nsel, ge, prev):
    b, _, t, _ = qh.shape
    gw = NSA_HPG * HEAD_DIM
    kv = pl.BlockSpec((None, None, t, LANES), lambda i, g: (i, g, 0, 0))
    return pl.pallas_call(
        _slc_kernel,
        grid=(b, NSA_GROUPS),
        in_specs=[
            pl.BlockSpec((None, NSA_HPG, t, LANES), lambda i, g: (i, g, 0, 0), pipeline_mode=pl.Buffered(1)),
            kv,
            pl.BlockSpec((None, None, 2, t, LANES), lambda i, g: (i, g, 0, 0, 0)),
            kv,
            pl.BlockSpec((None, None, t, gw), lambda i, g: (i, 1, 0, g)),
            pl.BlockSpec((None, t, gw), lambda i, g: (i, 0, g), pipeline_mode=pl.Buffered(1)),
        ],
        out_specs=pl.BlockSpec((None, t, gw), lambda i, g: (i, 0, g)),
        out_shape=jax.ShapeDtypeStruct((b, t, 1024), F32),
        scratch_shapes=[pltpu.VMEM((2, t // TK_SLC, NSA_HPG * TQ_NSA, TK_SLC), F32)],
        compiler_params=_cparams(("parallel", "parallel")),
        name="slc_attn",
    )(qh, ks, vs, nsel, ge, prev)


def _win_kernel(qh_ref, k_ref, v_ref, ge_ref, prev_ref, z_ref, y_ref, s_ref):
    tq = TQ_NSA
    rows = NSA_HPG * tq
    t_total = k_ref.shape[0]
    qry_l = lax.broadcasted_iota(jnp.int32, (tq, LANES), 0)
    key_l = lax.broadcasted_iota(jnp.int32, (tq, LANES), 1)

    step = 2 * LANES
    row_max = {}

    def window(i):
        k_lo = max(i * tq - WINDOW, 0)
        return k_lo, [(c, min(step, i * tq + tq - k_lo - c)) for c in range(0, i * tq + tq - k_lo, step)]

    def score_stage(i):
        q0 = i * tq
        k_lo, pieces = window(i)
        sbuf = s_ref.at[i % 2]
        qs = jnp.concatenate([qh_ref[h, q0:q0 + tq, :] for h in EVEN_ODD], axis=0)
        mrun = jnp.full((rows, LANES), NEG_INF, F32)
        for c0, cw in pieces:
            s = _dot_nt(qs, k_ref[k_lo + c0:k_lo + c0 + cw, :])
            for w in range(cw // LANES):
                k0 = k_lo + c0 + w * LANES
                blk = s[:, w * LANES:(w + 1) * LANES]
                if k0 == q0 or k0 == q0 - WINDOW:
                    dist = (k0 - q0) + key_l - qry_l
                    band = jnp.where((dist <= 0) & (dist > -WINDOW), 0.0, NEG_INF)
                    blk = (blk.reshape(NSA_HPG, tq, LANES) + band[None]).reshape(rows, LANES)
                sbuf[:, c0 + w * LANES:c0 + (w + 1) * LANES] = blk
                mrun = jnp.maximum(mrun, blk)
            yield
        row_max[i] = jnp.broadcast_to(jnp.max(mrun, axis=1, keepdims=True), mrun.shape)

    def value_stage(i):
        q0 = i * tq
        k_lo, pieces = window(i)
        sbuf = s_ref.at[i % 2]
        m = row_max.pop(i)
        acc = [jnp.zeros((rows // 2, LANES), F32)] * 2
        for c0, cw in pieces:
            p = jnp.exp2(sbuf[:, c0:c0 + cw] - _tile_lanes(m, cw // LANES)).astype(BF16)
            for e in range(2):
                acc[e] = acc[e] + _dot(p[e * rows // 2:(e + 1) * rows // 2], v_ref[e, k_lo + c0:k_lo + c0 + cw, :])
            yield
        for j, slab in enumerate(_group_slabs(acc[0], acc[1], ge_ref[q0:q0 + tq, :], tq)):
            cols = slice(j * LANES, (j + 1) * LANES)
            z = z_ref[q0:q0 + tq, cols].astype(F32)
            y = (prev_ref[q0:q0 + tq, cols] + slab) * (z * _sigmoid(z))
            y_ref[q0:q0 + tq, cols] = y.astype(BF16)

    _emit_pipelined(list(range(t_total // tq)), score_stage, value_stage)


def _win_attn(qh, kw, vw, ge, prev, main3):
    b, _, t, _ = qh.shape
    gw = NSA_HPG * HEAD_DIM
    kv = pl.BlockSpec((None, None, t, LANES), lambda i, g: (i, g, 0, 0))
    return pl.pallas_call(
        _win_kernel,
        grid=(b, NSA_GROUPS),
        in_specs=[
            pl.BlockSpec((None, NSA_HPG, t, LANES), lambda i, g: (i, g, 0, 0)),
            kv,
            pl.BlockSpec((None, None, 2, t, LANES), lambda i, g: (i, g, 0, 0, 0)),
            pl.BlockSpec((None, None, t, gw), lambda i, g: (i, 2, 0, g)),
            pl.BlockSpec((None, t, gw), lambda i, g: (i, 0, g)),
            pl.BlockSpec((None, t, gw), lambda i, g: (i, 0, COL_ZA // gw + g)),
        ],
        out_specs=pl.BlockSpec((None, t, gw), lambda i, g: (i, 0, g)),
        out_shape=jax.ShapeDtypeStruct((b, t, 1024), BF16),
        scratch_shapes=[pltpu.VMEM((2, NSA_HPG * TQ_NSA, WINDOW + TQ_NSA), F32)],
        compiler_params=_cparams(("parallel", "parallel")),
        name="win_attn",
    )(qh, kw, vw, ge, prev, main3)


def _decay_kernel(sm_ref, b_ref, f_ref):
    t_total = sm_ref.shape[0]
    x = sm_ref[:, LANES:2 * LANES] + b_ref[...]
    f = jnp.minimum(x, 0.0) - jnp.log1p(jnp.exp(-jnp.abs(x)))
    row = lax.broadcasted_iota(jnp.int32, f.shape, 0)
    sh = 1
    while sh < t_total:
        f = f + jnp.where(row >= sh, pltpu.roll(f, sh, 0), 0.0)
        sh *= 2
    hi, mid, lo = _split3(f * (-LOG2E))
    lane = _lane_iota(f.shape)
    pieces = jnp.where(lane < FOX_HEADS, hi,
                       jnp.where(lane < 2 * FOX_HEADS, pltpu.roll(mid, FOX_HEADS, 1),
                                 jnp.where(lane < 3 * FOX_HEADS, pltpu.roll(lo, 2 * FOX_HEADS, 1), 0.0)))
    f_ref[...] = pieces.astype(BF16)


def _decay(small3, b_pad):
    b, t, _ = small3.shape
    return pl.pallas_call(
        _decay_kernel,
        grid=(b,),
        in_specs=[pl.BlockSpec((None, t, SMALL_COLS), lambda i: (i, 0, 0)),
                  pl.BlockSpec((1, LANES), lambda i: (0, 0))],
        out_specs=pl.BlockSpec((None, t, LANES), lambda i: (i, 0, 0)),
        out_shape=jax.ShapeDtypeStruct((b, t, LANES), BF16),
        compiler_params=_cparams(("parallel",)),
        name="fox_decay",
    )(small3, b_pad)


def _fox_kernel(q_ref, k_ref, v_ref, f_ref, z_ref, y_ref, qf_ref, kf_ref, vf_ref, s_ref):
    tq = TQ_FOX
    t_total = z_ref.shape[0]
    slab_idx = pl.program_id(1)
    causal = (lax.broadcasted_iota(jnp.int32, (tq, tq), 0) >= lax.broadcasted_iota(jnp.int32, (tq, tq), 1))
    lane = _lane_iota((tq, LANES))

    src = lax.broadcasted_iota(jnp.int32, (LANES, LANES), 0)
    dst = lax.broadcasted_iota(jnp.int32, (LANES, LANES), 1)
    for e in range(2):
        own = (lane < HEAD_DIM) if e == 0 else (lane >= HEAD_DIM)
        ext0 = (1 - e) * HEAD_DIM
        head = 2 * slab_idx + e
        place = sum(jnp.where((src == j * FOX_HEADS + head) & (dst == ext0 + j), 1.0, 0.0)
                    for j in range(3)).astype(BF16)
        q_ext = jnp.where((lane >= ext0) & (lane < ext0 + 3), 1.0, 0.0).astype(BF16)
        for r0 in range(0, t_total, tq):
            rs = slice(r0, r0 + tq)
            k_ext = _dot(f_ref[rs, :], place).astype(BF16)
            qf_ref[e, rs, :] = jnp.where(own, q_ref[rs, :], q_ext)
            kf_ref[e, rs, :] = jnp.where(own, k_ref[rs, :], k_ext)
            vf_ref[e, rs, :] = jnp.where(own, v_ref[rs, :], jnp.ones((tq, LANES), BF16))

    nq = t_total // tq
    units = [j // 2 if j % 2 else nq - 1 - j // 2 for j in range(nq)]
    row_max = {}

    def score_stage(i):
        sbuf = s_ref.at[units.index(i) % 2]
        qs = [qf_ref[h, i * tq:(i + 1) * tq, :] for h in range(2)]
        mrun = [jnp.full((tq, LANES), NEG_INF, F32)] * 2
        for c in range(i + 1):
            for h in range(2):
                s = _dot_nt(qs[h], kf_ref[h, c * tq:(c + 1) * tq, :])
                if c == i:
                    s = jnp.where(causal, s, NEG_INF)
                sbuf[h, c] = s
                mrun[h] = _lane_block_max(mrun[h], s)
            yield
        row_max[i] = [_row_max_tiled(mrun[h], tq // LANES) for h in range(2)]

    def value_stage(i):
        sbuf = s_ref.at[units.index(i) % 2]
        m = row_max.pop(i)
        acc = [jnp.zeros((tq, LANES), F32)] * 2
        for c in range(i + 1):
            for h in range(2):
                p = jnp.exp2(sbuf[h, c] - m[h]).astype(BF16)
                acc[h] = acc[h] + _dot(p, vf_ref[h, c * tq:(c + 1) * tq, :])
            yield
        low = lane < HEAD_DIM
        slab = jnp.where(low, acc[0], acc[1]) / pltpu.roll(jnp.where(low, acc[1], acc[0]), HEAD_DIM, 1)
        z = z_ref[i * tq:(i + 1) * tq, :].astype(F32)
        y_ref[i * tq:(i + 1) * tq, :] = (slab * (z * _sigmoid(z))).astype(BF16)

    _emit_pipelined(units, score_stage, value_stage)


def _fox_attn(main3, f3):
    b, t, _ = main3.shape
    nslab = FOX_HEADS // 2

    def col(base):
        return pl.BlockSpec((None, t, LANES), lambda i, p: (i, 0, base // LANES + p))

    return pl.pallas_call(
        _fox_kernel,
        grid=(b, nslab),
        in_specs=[col(COL_QB), col(COL_KB), col(COL_VB),
                  pl.BlockSpec((None, t, LANES), lambda i, p: (i, 0, 0)),
                  col(COL_ZB)],
        out_specs=pl.BlockSpec((None, t, LANES), lambda i, p: (i, 0, p)),
        out_shape=jax.ShapeDtypeStruct((b, t, 1024), BF16),
        scratch_shapes=[pltpu.VMEM((2, t, LANES), BF16), pltpu.VMEM((2, t, LANES), BF16),
                        pltpu.VMEM((2, t, LANES), BF16),
                        pltpu.VMEM((2, 2, t // TQ_FOX, TQ_FOX, TQ_FOX), F32)],
        compiler_params=_cparams(("parallel", "parallel")),
        name="fox_attn",
    )(main3, main3, main3, f3, main3)


def _out_kernel(x_ref, ya_ref, yb_ref, ra_ref, rb_ref, wn_ref, wf_ref, wo_ref, fg_ref, o_ref, *, final):
    ta = _dot(ya_ref[...], wn_ref[...])
    tb = _dot(yb_ref[...], wf_ref[...])
    merged = _sigmoid(ra_ref[...].astype(F32)) * ta + _sigmoid(rb_ref[...].astype(F32)) * tb
    out = x_ref[...] + _dot(merged.astype(BF16), wo_ref[...])
    if final:
        out = out * lax.rsqrt(jnp.mean(out * out, axis=-1, keepdims=True) + RMS_EPS) * fg_ref[...]
    o_ref[...] = out


def _out_proj(x2, ya2, yb2, main2, wn, wf, wo, fg, final, tm=1024):
    n = x2.shape[0]
    tm = min(tm, n)
    row = lambda c: pl.BlockSpec((tm, 1024), lambda i: (i, c))
    w = pl.BlockSpec((1024, 1024), lambda i: (0, 0))
    return pl.pallas_call(
        functools.partial(_out_kernel, final=final),
        grid=(n // tm,),
        in_specs=[row(0), row(0), row(0), row(COL_RA // 1024), row(COL_RB // 1024), w, w, w,
                  pl.BlockSpec((1, 1024), lambda i: (0, 0))],
        out_specs=row(0),
        out_shape=jax.ShapeDtypeStruct((n, 1024), F32),
        compiler_params=_cparams(("parallel",)),
        name="out_proj",
    )(x2, ya2, yb2, main2, main2, wn, wf, wo, fg)


def _rope_tables(t):
    inv = ROPE_THETA ** (-jnp.arange(0, HEAD_DIM, 2, dtype=F32) / HEAD_DIM)
    ang = jnp.arange(t, dtype=F32)[:, None] * inv[None, :]
    c, s = jnp.cos(ang), jnp.sin(ang)
    return jnp.concatenate([c, c, c, c], axis=1), jnp.concatenate([-s, s, -s, s], axis=1)


def _overlap_matrix():
    n = np.arange(LANES)[:, None] * CMP_STRIDE
    j = np.arange(LANES)[None, :] * SEL_BLOCK
    m = (n < j + SEL_BLOCK) & (n + CMP_BLOCK > j) & (np.arange(LANES)[None, :] < N_SEL)
    return jnp.asarray(m.astype(np.float32), dtype=BF16)


def _gate_expansion():
    m = np.zeros((3, LANES, NSA_HEADS * HEAD_DIM), np.float32)
    for br in range(3):
        for h in range(NSA_HEADS):
            m[br, br * NSA_HEADS + h, h * HEAD_DIM:(h + 1) * HEAD_DIM] = 1.0
    return jnp.asarray(m, dtype=BF16)


def _reorder_w_in(w):
    qa, kv, ga, za, qb, kb, vb, fb, zb, mg = jnp.split(
        w, np.cumsum([1024, 768, 48, 1024, 1024, 1024, 1024, 16, 1024])[:9].tolist(), axis=1)
    main = jnp.concatenate([qa, za, qb * (SCALE * LOG2E), kb, vb, zb, mg, kv], axis=1).astype(BF16)
    d = w.shape[0]
    small = jnp.concatenate([ga, jnp.zeros((d, LANES - 48), F32), fb, jnp.zeros((d, LANES - 16), F32)],
                            axis=1).astype(BF16)
    return main, small


def _compress_weights(w1, w2):
    eye = jnp.eye(NSA_GROUPS, dtype=F32)
    w1r = w1.reshape(CMP_BLOCK, HEAD_DIM, CMP_HIDDEN)
    half = CMP_BLOCK // 2

    def blockdiag(wl):
        return jnp.einsum('ldh,gk->lgdkh', wl, eye).reshape(half * NSA_GROUPS * HEAD_DIM,
                                                            NSA_GROUPS * CMP_HIDDEN).astype(BF16)

    wa, wb = blockdiag(w1r[:half]), blockdiag(w1r[half:])
    w2d = jnp.einsum('hd,gk,c->gkhcd', w2, eye, jnp.ones((2,), F32)).reshape(
        NSA_GROUPS, NSA_GROUPS * CMP_HIDDEN, LANES).astype(BF16)
    return wa, wb, w2d


def kernel(x, norm_g, w_in, b_forget, cmp_pe_k, cmp_w1_k, cmp_w2_k, cmp_pe_v, cmp_w1_v, cmp_w2_v,
           w_proj_nsa, w_proj_fox, w_out, final_g):
    b, t, d = x.shape
    depth = norm_g.shape[0]
    n = b * t
    nc = t // CMP_STRIDE
    assert d == D_MODEL and nc == LANES and t // SEL_BLOCK == N_SEL
    cos, sin = _rope_tables(t)
    ovl = _overlap_matrix()
    gmat = _gate_expansion()
    fg = final_g.reshape(1, d)
    x2 = x.reshape(n, d)
    for l in range(depth):
        w_main, w_small = _reorder_w_in(w_in[l])
        main2, small2, kc2, vc2 = _inproj(x2, norm_g[l].reshape(1, d), w_main, w_small)
        main3 = main2.reshape(b, t, MAIN_COLS)
        small3 = small2.reshape(b, t, SMALL_COLS)
        xk = kc2.reshape(b, nc, CMP_STRIDE * LANES)
        xv = vc2.reshape(b, nc, CMP_STRIDE * LANES)
        wak, wbk, w2k = _compress_weights(cmp_w1_k[l], cmp_w2_k[l])
        wav, wbv, w2v = _compress_weights(cmp_w1_v[l], cmp_w2_v[l])
        pek = jnp.broadcast_to(cmp_pe_k[l].reshape(1, -1), (8, CMP_BLOCK * HEAD_DIM))
        pev = jnp.broadcast_to(cmp_pe_v[l].reshape(1, -1), (8, CMP_BLOCK * HEAD_DIM))
        kc, vc = _compress(xk, xv, pek, pev, cmp_w1_k[l].astype(BF16), cmp_w1_v[l].astype(BF16),
                           wak, wbk, wav, wbv, w2k, w2v)
        qh, ks, kw, vs, vw, ge, o_cmp, nsel = _nsa_front(main3, small3, cos, sin, gmat, kc, vc, ovl)
        o_cs = _slc_attn(qh, ks, vs, nsel, ge, o_cmp)
        y_a = _win_attn(qh, kw, vw, ge, o_cs, main3)
        b_pad = jnp.concatenate([b_forget[l], jnp.zeros((LANES - FOX_HEADS,), F32)]).reshape(1, LANES)
        y_b = _fox_attn(main3, _decay(small3, b_pad))
        x2 = _out_proj(x2, y_a.reshape(n, 1024), y_b.reshape(n, 1024), main2,
                       w_proj_nsa[l].astype(BF16), w_proj_fox[l].astype(BF16), w_out[l].astype(BF16),
                       fg, final=(l == depth - 1))
    return x2.reshape(b, t, d)
```

```python
import functools

import numpy as np
import jax
import jax.numpy as jnp
from jax import lax
from jax.experimental import pallas as pl
from jax.experimental.pallas import tpu as pltpu

F32 = jnp.float32
BF16 = jnp.bfloat16

D_MODEL = 1024
HEAD_DIM = 64
LANES = 128
NSA_HEADS = 16
NSA_GROUPS = 2
NSA_HPG = NSA_HEADS // NSA_GROUPS
CMP_BLOCK = 32
CMP_STRIDE = 16
CMP_HIDDEN = 2 * HEAD_DIM
SEL_BLOCK = 64
SEL_TOPK = 8
N_SEL = 32
WINDOW = 512
FOX_HEADS = 16
ROPE_THETA = 10000.0
RMS_EPS = 1e-6
NEG_INF = -1e30
FORCED_SCORE = 1e4
SCALE = HEAD_DIM ** -0.5
LOG2E = 1.4426950408889634
EXT = HEAD_DIM

COL_QA, COL_ZA, COL_QB, COL_KB, COL_VB, COL_ZB, COL_RA, COL_RB, COL_KV = (
    0, 1024, 2048, 3072, 4096, 5120, 6144, 7168, 8192)
MAIN_COLS = 8960
SMALL_COLS = 256

TQ_NSA = 128
TK_SLC = 256
TQ_FOX = 256
VMEM_LIMIT = 56 * 1024 * 1024


def _cparams(sem):
    return pltpu.CompilerParams(dimension_semantics=sem, vmem_limit_bytes=VMEM_LIMIT)


def _lane_iota(shape):
    return lax.broadcasted_iota(jnp.int32, shape, len(shape) - 1)


def _dot_nt(a, b):
    return lax.dot_general(a, b, (((1,), (1,)), ((), ())), preferred_element_type=F32)


def _dot(a, b):
    return jnp.dot(a, b, preferred_element_type=F32)


def _split3(x):
    hi = x.astype(BF16).astype(F32)
    r = x - hi
    mid = r.astype(BF16).astype(F32)
    lo = (r - mid).astype(BF16).astype(F32)
    return hi, mid, lo


def _split3_dot(a, m):
    hi, mid, lo = _split3(a)
    return _dot(hi.astype(BF16), m) + _dot(mid.astype(BF16), m) + _dot(lo.astype(BF16), m)


def _sigmoid(x):
    return 1.0 / (1.0 + jnp.exp(-x))


def _rope_slab(x, cos, sin):
    lane = _lane_iota(x.shape)
    swapped = jnp.where((lane % HEAD_DIM) < HEAD_DIM // 2,
                        pltpu.roll(x, LANES - HEAD_DIM // 2, 1), pltpu.roll(x, HEAD_DIM // 2, 1))
    return x * cos + swapped * sin


def _stack_heads(slabs):
    lane = _lane_iota(slabs[0].shape)
    parts = []
    for s in slabs:
        parts.append(jnp.where(lane < HEAD_DIM, s, jnp.zeros_like(s)))
        parts.append(jnp.where(lane >= HEAD_DIM, s, jnp.zeros_like(s)))
    return jnp.concatenate(parts, axis=0)


def _unstack_heads(o, nslab, rows):
    lane = _lane_iota((rows, LANES))
    out = []
    for j in range(nslab):
        a = o[(2 * j) * rows:(2 * j + 1) * rows]
        b = o[(2 * j + 1) * rows:(2 * j + 2) * rows]
        out.append(jnp.where(lane < HEAD_DIM, a, b))
    return out


def _inproj_kernel(x_ref, g_ref, w_ref, ws_ref, main_ref, small_ref, kc_ref, vc_ref, xn_ref, *, kv_off):
    @pl.when(pl.program_id(1) == 0)
    def _():
        x = x_ref[...]
        y = x * lax.rsqrt(jnp.mean(x * x, axis=-1, keepdims=True) + RMS_EPS) * g_ref[...]
        xn = y.astype(BF16)
        xn_ref[...] = xn
        small_ref[...] = _dot(xn, ws_ref[...])

    res = _dot(xn_ref[...], w_ref[...]).astype(BF16)
    main_ref[...] = res

    @pl.when(pl.program_id(1) == pl.num_programs(1) - 1)
    def _():
        kc_ref[...] = res[:, kv_off:kv_off + LANES]
        vc_ref[...] = res[:, kv_off + LANES:kv_off + 2 * LANES]


def _inproj(x2, g, w_main, w_small, tm=2048, tn=1280):
    n = x2.shape[0]
    tm = min(tm, n)
    kv_off = COL_KV - (MAIN_COLS - tn)
    assert 0 <= kv_off and kv_off + 2 * LANES <= tn
    return pl.pallas_call(
        functools.partial(_inproj_kernel, kv_off=kv_off),
        grid=(n // tm, MAIN_COLS // tn),
        in_specs=[
            pl.BlockSpec((tm, D_MODEL), lambda i, j: (i, 0)),
            pl.BlockSpec((1, D_MODEL), lambda i, j: (0, 0)),
            pl.BlockSpec((D_MODEL, tn), lambda i, j: (0, j)),
            pl.BlockSpec((D_MODEL, SMALL_COLS), lambda i, j: (0, 0)),
        ],
        out_specs=[
            pl.BlockSpec((tm, tn), lambda i, j: (i, j)),
            pl.BlockSpec((tm, SMALL_COLS), lambda i, j: (i, 0)),
            pl.BlockSpec((tm, LANES), lambda i, j: (i, 0)),
            pl.BlockSpec((tm, LANES), lambda i, j: (i, 0)),
        ],
        out_shape=[jax.ShapeDtypeStruct((n, MAIN_COLS), BF16),
                   jax.ShapeDtypeStruct((n, SMALL_COLS), F32),
                   jax.ShapeDtypeStruct((n, LANES), BF16),
                   jax.ShapeDtypeStruct((n, LANES), BF16)],
        scratch_shapes=[pltpu.VMEM((tm, D_MODEL), BF16)],
        compiler_params=_cparams(("parallel", "arbitrary")),
        name="inproj",
    )(x2, g, w_main, w_small)


def _compress_kernel(xk_ref, xv_ref, pek_ref, pev_ref, w1k_ref, w1v_ref,
                     wak_ref, wbk_ref, wav_ref, wbv_ref, w2k_ref, w2v_ref, kc_ref, vc_ref):
    def one(x_ref, pe_ref, w1_ref, wa_ref, wb_ref, w2_ref, out_ref, out_scale):
        x = x_ref[...]
        a = _dot(x, wa_ref[...])
        b = _dot(x, wb_ref[...])
        nc = a.shape[0]
        b_up = pltpu.roll(b, nc - 1, 0)
        pe_c = _dot(pe_ref[...].astype(BF16), w1_ref[...])[0:1]
        pe_c = jnp.concatenate([pe_c, pe_c], axis=1)
        hid = a + b_up + pe_c
        hid = (hid * _sigmoid(hid)).astype(BF16)
        for g in range(NSA_GROUPS):
            out_ref[g] = (_dot(hid, w2_ref[g]) * out_scale).astype(BF16)

    one(xk_ref, pek_ref, w1k_ref, wak_ref, wbk_ref, w2k_ref, kc_ref, SCALE * LOG2E)
    one(xv_ref, pev_ref, w1v_ref, wav_ref, wbv_ref, w2v_ref, vc_ref, 1.0)


def _compress(xk, xv, pek, pev, w1k, w1v, wak, wbk, wav, wbv, w2k, w2v):
    b, nc, kw = xk.shape
    full = lambda a: pl.BlockSpec(a.shape, lambda i: (0,) * a.ndim)
    xs = pl.BlockSpec((None, nc, kw), lambda i: (i, 0, 0))
    os_ = pl.BlockSpec((None, NSA_GROUPS, nc, LANES), lambda i: (i, 0, 0, 0))
    return pl.pallas_call(
        _compress_kernel,
        grid=(b,),
        in_specs=[xs, xs] + [full(a) for a in (pek, pev, w1k, w1v, wak, wbk, wav, wbv, w2k, w2v)],
        out_specs=[os_, os_],
        out_shape=[jax.ShapeDtypeStruct((b, NSA_GROUPS, nc, LANES), BF16)] * 2,
        compiler_params=_cparams(("parallel",)),
        name="compress",
    )(xk, xv, pek, pev, w1k, w1v, wak, wbk, wav, wbv, w2k, w2v)


def _nsaprep_kernel(q_ref, s_ref, w_ref, sm_ref, cos_ref, sin_ref, gmat_ref,
                    qh_ref, ks_ref, kw_ref, vs_ref, vw_ref, ge_ref, *, tt):
    cos = cos_ref[...]
    sin = sin_ref[...]
    lane = _lane_iota((tt, LANES))
    t = pl.program_id(1) * tt + lax.broadcasted_iota(jnp.int32, (tt, LANES), 0)
    low = lane < HEAD_DIM
    for j in range(NSA_HEADS // 2):
        x = _rope_slab(q_ref[:, j * LANES:(j + 1) * LANES].astype(F32), cos, sin) * (SCALE * LOG2E)
        qh_ref[2 * j] = jnp.where(low, x, 0.0).astype(BF16)
        qh_ref[2 * j + 1] = jnp.where(low, pltpu.roll(x, HEAD_DIM, 1), 0.0).astype(BF16)

    onehot = jnp.where((lane >= EXT) & (lane - EXT == t // SEL_BLOCK) & (lane < EXT + N_SEL), 1.0, 0.0)
    ks = _rope_slab(s_ref[:, 0:LANES].astype(F32), cos, sin)
    kw = _rope_slab(w_ref[:, 0:LANES].astype(F32), cos, sin)
    for g in range(NSA_GROUPS):
        ks_ref[g] = jnp.where(low, ks if g == 0 else pltpu.roll(ks, HEAD_DIM, 1), onehot).astype(BF16)
        kw_ref[g] = jnp.where(low, kw if g == 0 else pltpu.roll(kw, HEAD_DIM, 1), 0.0).astype(BF16)

    for src_ref, dst_ref in ((s_ref, vs_ref), (w_ref, vw_ref)):
        v = src_ref[:, LANES:2 * LANES].astype(F32)
        v_sw = pltpu.roll(v, HEAD_DIM, 1)
        dst_ref[0, 0] = jnp.where(low, v, 1.0).astype(BF16)
        dst_ref[0, 1] = jnp.where(low, 1.0, v_sw).astype(BF16)
        dst_ref[1, 0] = jnp.where(low, v_sw, 1.0).astype(BF16)
        dst_ref[1, 1] = jnp.where(low, 1.0, v).astype(BF16)
    gates = _sigmoid(sm_ref[:, 0:LANES]).astype(BF16)
    for br in range(3):
        ge_ref[br] = _dot(gates, gmat_ref[br]).astype(BF16)


def _cmp_kernel(q_ref, kc_ref, vc_ref, ovl_ref, ge_ref, o_ref, nsel_ref, *, tq):
    q0 = pl.program_id(1) * tq
    nc = kc_ref.shape[1]
    lane = _lane_iota((tq, LANES))
    t = q0 + lax.broadcasted_iota(jnp.int32, (tq, LANES), 0)
    cmp_valid = (lane * CMP_STRIDE + CMP_BLOCK - 1 <= t) & (lane < nc)
    row_valid = (t >= CMP_BLOCK - 1)[None]
    cur = t // SEL_BLOCK
    forced = (lane == 0) | (lane == cur) | (lane == cur - 1)
    blk_valid = lane * SEL_BLOCK <= t
    row8 = lax.broadcasted_iota(jnp.int32, (8, tq), 0)
    for g in range(NSA_GROUPS):
        qs = _stack_heads([q_ref[:, (g * 4 + j) * LANES:(g * 4 + j + 1) * LANES] for j in range(4)])
        s3 = _dot_nt(qs, kc_ref[g]).reshape(NSA_HPG, tq, nc)
        s3 = jnp.where(cmp_valid[None], s3, NEG_INF)
        m = jnp.max(s3, axis=-1, keepdims=True)
        e = jnp.exp2(s3 - m)
        l = _dot(e.reshape(NSA_HPG * tq, nc).astype(BF16), jnp.ones((nc, LANES), BF16))
        inv = jnp.where(row_valid, 1.0 / l.reshape(NSA_HPG, tq, LANES), 0.0)
        p = (e * inv).astype(BF16)
        o = _dot(p.reshape(NSA_HPG * tq, nc), vc_ref[g])
        for j, slab in enumerate(_unstack_heads(o, 4, tq)):
            cols = slice((g * 4 + j) * LANES, (g * 4 + j + 1) * LANES)
            o_ref[:, cols] = slab * ge_ref[:, cols].astype(F32)
        imp = _split3_dot(jnp.sum(p.astype(F32), axis=0), ovl_ref[...])
        score = jnp.where(forced, FORCED_SCORE, jnp.where(blk_valid, imp, -1.0))
        score = jnp.where(lane < N_SEL, score, -2.0)
        st = score.T[0:N_SEL, :]
        cnt = [jnp.zeros((8, tq), F32) for _ in range(N_SEL // 8)]
        for j in range(N_SEL):
            rj = st[j:j + 1, :]
            for r in range(N_SEL // 8):
                blk = st[8 * r:8 * r + 8, :]
                if j < 8 * r:
                    beats = rj >= blk
                elif j >= 8 * r + 8:
                    beats = rj > blk
                else:
                    beats = (rj > blk) | ((rj == blk) & (row8 + 8 * r > j))
                cnt[r] = cnt[r] + jnp.where(beats, 1.0, 0.0)
        nsel_t = jnp.where(jnp.concatenate(cnt, axis=0) < float(SEL_TOPK), 0.0, NEG_INF)
        nsel_t = jnp.concatenate([jnp.zeros((EXT, tq), F32), nsel_t,
                                  jnp.zeros((LANES - EXT - N_SEL, tq), F32)], axis=0)
        nsel_ref[g] = nsel_t.T.astype(BF16)


def _nsafront_kernel(q_ref, s_ref, w_ref, sm_ref, cos_ref, sin_ref, gmat_ref, kc_ref, vc_ref, ovl_ref,
                     qh_ref, ks_ref, kw_ref, vs_ref, vw_ref, ge_ref, o_ref, nsel_ref, *, tq):
    _nsaprep_kernel(q_ref, s_ref, w_ref, sm_ref, cos_ref, sin_ref, gmat_ref,
                    qh_ref, ks_ref, kw_ref, vs_ref, vw_ref, ge_ref, tt=tq)
    _cmp_kernel(q_ref, kc_ref, vc_ref, ovl_ref, ge_ref.at[0], o_ref, nsel_ref, tq=tq)


def _nsa_front(main3, small3, cos, sin, gmat, kc, vc, ovl, tq=256):
    b, t, _ = main3.shape
    nc = kc.shape[2]
    blk = COL_KV // (2 * LANES)
    kspec = pl.BlockSpec((None, NSA_GROUPS, tq, LANES), lambda i, j: (i, 0, j, 0))
    vspec = pl.BlockSpec((None, NSA_GROUPS, 2, tq, LANES), lambda i, j: (i, 0, 0, j, 0))
    cspec = pl.BlockSpec((None, NSA_GROUPS, nc, LANES), lambda i, j: (i, 0, 0, 0))
    return pl.pallas_call(
        functools.partial(_nsafront_kernel, tq=tq),
        grid=(b, t // tq),
        in_specs=[
            pl.BlockSpec((None, tq, 1024), lambda i, j: (i, j, COL_QA // 1024)),
            pl.BlockSpec((None, tq, 2 * LANES), lambda i, j: (i, j, blk + 1)),
            pl.BlockSpec((None, tq, 2 * LANES), lambda i, j: (i, j, blk + 2)),
            pl.BlockSpec((None, tq, SMALL_COLS), lambda i, j: (i, j, 0)),
            pl.BlockSpec((tq, LANES), lambda i, j: (j, 0)),
            pl.BlockSpec((tq, LANES), lambda i, j: (j, 0)),
            pl.BlockSpec(gmat.shape, lambda i, j: (0, 0, 0)),
            cspec, cspec,
            pl.BlockSpec(ovl.shape, lambda i, j: (0, 0)),
        ],
        out_specs=[
            pl.BlockSpec((None, NSA_HEADS, tq, LANES), lambda i, j: (i, 0, j, 0)),
            kspec, kspec, vspec, vspec,
            pl.BlockSpec((None, 3, tq, 1024), lambda i, j: (i, 0, j, 0)),
            pl.BlockSpec((None, tq, 1024), lambda i, j: (i, j, 0)),
            kspec,
        ],
        out_shape=[jax.ShapeDtypeStruct((b, NSA_HEADS, t, LANES), BF16)]
        + [jax.ShapeDtypeStruct((b, NSA_GROUPS, t, LANES), BF16)] * 2
        + [jax.ShapeDtypeStruct((b, NSA_GROUPS, 2, t, LANES), BF16)] * 2
        + [jax.ShapeDtypeStruct((b, 3, t, 1024), BF16),
           jax.ShapeDtypeStruct((b, t, 1024), F32),
           jax.ShapeDtypeStruct((b, NSA_GROUPS, t, LANES), BF16)],
        compiler_params=_cparams(("parallel", "parallel")),
        name="nsa_front",
    )(main3, main3, main3, small3, cos, sin, gmat, kc, vc, ovl)


def _tile_lanes(x, n):
    return jnp.concatenate([x] * n, axis=1)


def _lane_block_max(mrun, s):
    for w in range(s.shape[1] // LANES):
        mrun = jnp.maximum(mrun, s[:, w * LANES:(w + 1) * LANES])
    return mrun


def _row_max_tiled(mrun, n):
    return _tile_lanes(jnp.broadcast_to(jnp.max(mrun, axis=1, keepdims=True), mrun.shape), n)


def _emit_pipelined(units, score_stage, value_stage):
    def drain(gens):
        gens = [g for g in gens if g is not None]
        while gens:
            for g in list(gens):
                try:
                    next(g)
                except StopIteration:
                    gens.remove(g)

    drain([score_stage(units[0])])
    for n, u in enumerate(units):
        drain([score_stage(units[n + 1]) if n + 1 < len(units) else None, value_stage(u)])


EVEN_ODD = tuple(range(0, NSA_HPG, 2)) + tuple(range(1, NSA_HPG, 2))


def _group_slabs(acc_even, acc_odd, gates, tq):
    low = _lane_iota((tq, LANES)) < HEAD_DIM
    slabs = []
    for j in range(NSA_HPG // 2):
        e = acc_even[j * tq:(j + 1) * tq]
        o = acc_odd[j * tq:(j + 1) * tq]
        den = pltpu.roll(jnp.where(low, o, e), HEAD_DIM, 1)
        slabs.append(jnp.where(low, e, o) / den * gates[:, j * LANES:(j + 1) * LANES].astype(F32))
    return slabs


def _slc_kernel(qh_ref, k_ref, v_ref, nsel_ref, ge_ref, prev_ref, o_ref, s_ref):
    tq, tk = TQ_NSA, TK_SLC
    rows = NSA_HPG * tq
    t_total = k_ref.shape[0]
    nq = t_total // tq
    order = list(range(nq - 1, -1, -1))
    row_max = {}

    def key_chunks(i):
        c_last = i * tq // tk
        return [(c * tk, tk) for c in range(c_last)] + [(c_last * tk, (i + 1) * tq - c_last * tk)]

    def score_stage(i):
        q0 = i * tq
        sbuf = s_ref.at[order.index(i) % 2]
        nsel = nsel_ref[q0:q0 + tq, :]
        qs = jnp.concatenate([qh_ref[h, q0:q0 + tq, :] + nsel for h in EVEN_ODD], axis=0)
        mrun = jnp.full((rows, LANES), NEG_INF, F32)
        pieces = key_chunks(i)
        for n, (k0, w) in enumerate(pieces):
            s = _dot_nt(qs, k_ref[k0:k0 + w, :])
            if n == len(pieces) - 1:
                key = k0 + lax.broadcasted_iota(jnp.int32, (tq, w), 1)
                qry = q0 + lax.broadcasted_iota(jnp.int32, (tq, w), 0)
                causal = jnp.where(key <= qry, 0.0, NEG_INF)
                s = (s.reshape(NSA_HPG, tq, w) + causal[None]).reshape(rows, w)
            sbuf[n, :, 0:w] = s
            mrun = _lane_block_max(mrun, s)
            yield
        row_max[i] = _row_max_tiled(mrun, tk // LANES)

    def value_stage(i):
        q0 = i * tq
        sbuf = s_ref.at[order.index(i) % 2]
        m = row_max.pop(i)
        acc = [jnp.zeros((rows // 2, LANES), F32)] * 2
        for n, (k0, w) in enumerate(key_chunks(i)):
            p = jnp.exp2(sbuf[n, :, 0:w] - m[:, 0:w]).astype(BF16)
            for e in range(2):
                acc[e] = acc[e] + _dot(p[e * rows // 2:(e + 1) * rows // 2], v_ref[e, k0:k0 + w, :])
            yield
        for j, slab in enumerate(_group_slabs(acc[0], acc[1], ge_ref[q0:q0 + tq, :], tq)):
            cols = slice(j * LANES, (j + 1) * LANES)
            o_ref[q0:q0 + tq, cols] = prev_ref[q0:q0 + tq, cols] + slab

    _emit_pipelined(order, score_stage, value_stage)


def _slc_attn(qh, ks, vs, nsel, ge, prev):
    b, _, t, _ = qh.shape
    gw = NSA_HPG * HEAD_DIM
    kv = pl.BlockSpec((None, None, t, LANES), lambda i, g: (i, g, 0, 0))
    return pl.pallas_call(
        _slc_kernel,
        grid=(b, NSA_GROUPS),
        in_specs=[
            pl.BlockSpec((None, NSA_HPG, t, LANES), lambda i, g: (i, g, 0, 0), pipeline_mode=pl.Buffered(1)),
            kv,
            pl.BlockSpec((None, None, 2, t, LANES), lambda i, g: (i, g, 0, 0, 0)),
            kv,
            pl.BlockSpec((None, None, t, gw), lambda i, g: (i, 1, 0, g)),
            pl.BlockSpec((None, t, gw), lambda i, g: (i, 0, g), pipeline_mode=pl.Buffered(1)),
        ],
        out_specs=pl.BlockSpec((None, t, gw), lambda i, g: (i, 0, g)),
        out_shape=jax.ShapeDtypeStruct((b, t, 1024), F32),
        scratch_shapes=[pltpu.VMEM((2, t // TK_SLC, NSA_HPG * TQ_NSA, TK_SLC), F32)],
        compiler_params=_cparams(("parallel", "parallel")),
        name="slc_attn",
    )(qh, ks, vs, nsel, ge, prev)


def _win_kernel(qh_ref, k_ref, v_ref, ge_ref, prev_ref, z_ref, y_ref, s_ref):
    tq = TQ_NSA
    rows = NSA_HPG * tq
    t_total = k_ref.shape[0]
    qry_l = lax.broadcasted_iota(jnp.int32, (tq, LANES), 0)
    key_l = lax.broadcasted_iota(jnp.int32, (tq, LANES), 1)

    step = 2 * LANES
    row_max = {}

    def window(i):
        k_lo = max(i * tq - WINDOW, 0)
        return k_lo, [(c, min(step, i * tq + tq - k_lo - c)) for c in range(0, i * tq + tq - k_lo, step)]

    def score_stage(i):
        q0 = i * tq
        k_lo, pieces = window(i)
        sbuf = s_ref.at[i % 2]
        qs = jnp.concatenate([qh_ref[h, q0:q0 + tq, :] for h in EVEN_ODD], axis=0)
        mrun = jnp.full((rows, LANES), NEG_INF, F32)
        for c0, cw in pieces:
            s = _dot_nt(qs, k_ref[k_lo + c0:k_lo + c0 + cw, :])
            for w in range(cw // LANES):
                k0 = k_lo + c0 + w * LANES
                blk = s[:, w * LANES:(w + 1) * LANES]
                if k0 == q0 or k0 == q0 - WINDOW:
                    dist = (k0 - q0) + key_l - qry_l
                    band = jnp.where((dist <= 0) & (dist > -WINDOW), 0.0, NEG_INF)
                    blk = (blk.reshape(NSA_HPG, tq, LANES) + band[None]).reshape(rows, LANES)
                sbuf[:, c0 + w * LANES:c0 + (w + 1) * LANES] = blk
                mrun = jnp.maximum(mrun, blk)
            yield
        row_max[i] = jnp.broadcast_to(jnp.max(mrun, axis=1, keepdims=True), mrun.shape)

    def value_stage(i):
        q0 = i * tq
        k_lo, pieces = window(i)
        sbuf = s_ref.at[i % 2]
        m = row_max.pop(i)
        acc = [jnp.zeros((rows // 2, LANES), F32)] * 2
        for c0, cw in pieces:
            p = jnp.exp2(sbuf[:, c0:c0 + cw] - _tile_lanes(m, cw // LANES)).astype(BF16)
            for e in range(2):
                acc[e] = acc[e] + _dot(p[e * rows // 2:(e + 1) * rows // 2], v_ref[e, k_lo + c0:k_lo + c0 + cw, :])
            yield
        for j, slab in enumerate(_group_slabs(acc[0], acc[1], ge_ref[q0:q0 + tq, :], tq)):
            cols = slice(j * LANES, (j + 1) * LANES)
            z = z_ref[q0:q0 + tq, cols].astype(F32)
            y = (prev_ref[q0:q0 + tq, cols] + slab) * (z * _sigmoid(z))
            y_ref[q0:q0 + tq, cols] = y.astype(BF16)

    _emit_pipelined(list(range(t_total // tq)), score_stage, value_stage)


def _win_attn(qh, kw, vw, ge, prev, main3):
    b, _, t, _ = qh.shape
    gw = NSA_HPG * HEAD_DIM
    kv = pl.BlockSpec((None, None, t, LANES), lambda i, g: (i, g, 0, 0))
    return pl.pallas_call(
        _win_kernel,
        grid=(b, NSA_GROUPS),
        in_specs=[
            pl.BlockSpec((None, NSA_HPG, t, LANES), lambda i, g: (i, g, 0, 0)),
            kv,
            pl.BlockSpec((None, None, 2, t, LANES), lambda i, g: (i, g, 0, 0, 0)),
            pl.BlockSpec((None, None, t, gw), lambda i, g: (i, 2, 0, g)),
            pl.BlockSpec((None, t, gw), lambda i, g: (i, 0, g)),
            pl.BlockSpec((None, t, gw), lambda i, g: (i, 0, COL_ZA // gw + g)),
        ],
        out_specs=pl.BlockSpec((None, t, gw), lambda i, g: (i, 0, g)),
        out_shape=jax.ShapeDtypeStruct((b, t, 1024), BF16),
        scratch_shapes=[pltpu.VMEM((2, NSA_HPG * TQ_NSA, WINDOW + TQ_NSA), F32)],
        compiler_params=_cparams(("parallel", "parallel")),
        name="win_attn",
    )(qh, kw, vw, ge, prev, main3)


def _decay_kernel(sm_ref, b_ref, f_ref):
    t_total = sm_ref.shape[0]
    x = sm_ref[:, LANES:2 * LANES] + b_ref[...]
    f = jnp.minimum(x, 0.0) - jnp.log1p(jnp.exp(-jnp.abs(x)))
    row = lax.broadcasted_iota(jnp.int32, f.shape, 0)
    sh = 1
    while sh < t_total:
        f = f + jnp.where(row >= sh, pltpu.roll(f, sh, 0), 0.0)
        sh *= 2
    hi, mid, lo = _split3(f * (-LOG2E))
    lane = _lane_iota(f.shape)
    pieces = jnp.where(lane < FOX_HEADS, hi,
                       jnp.where(lane < 2 * FOX_HEADS, pltpu.roll(mid, FOX_HEADS, 1),
                                 jnp.where(lane < 3 * FOX_HEADS, pltpu.roll(lo, 2 * FOX_HEADS, 1), 0.0)))
    f_ref[...] = pieces.astype(BF16)


def _decay(small3, b_pad):
    b, t, _ = small3.shape
    return pl.pallas_call(
        _decay_kernel,
        grid=(b,),
        in_specs=[pl.BlockSpec((None, t, SMALL_COLS), lambda i: (i, 0, 0)),
                  pl.BlockSpec((1, LANES), lambda i: (0, 0))],
        out_specs=pl.BlockSpec((None, t, LANES), lambda i: (i, 0, 0)),
        out_shape=jax.ShapeDtypeStruct((b, t, LANES), BF16),
        compiler_params=_cparams(("parallel",)),
        name="fox_decay",
    )(small3, b_pad)


def _fox_kernel(q_ref, k_ref, v_ref, f_ref, z_ref, y_ref, qf_ref, kf_ref, vf_ref, s_ref):
    tq = TQ_FOX
    t_total = z_ref.shape[0]
    slab_idx = pl.program_id(1)
    causal = (lax.broadcasted_iota(jnp.int32, (tq, tq), 0) >= lax.broadcasted_iota(jnp.int32, (tq, tq), 1))
    lane = _lane_iota((tq, LANES))

    src = lax.broadcasted_iota(jnp.int32, (LANES, LANES), 0)
    dst = lax.broadcasted_iota(jnp.int32, (LANES, LANES), 1)
    for e in range(2):
        own = (lane < HEAD_DIM) if e == 0 else (lane >= HEAD_DIM)
        ext0 = (1 - e) * HEAD_DIM
        head = 2 * slab_idx + e
        place = sum(jnp.where((src == j * FOX_HEADS + head) & (dst == ext0 + j), 1.0, 0.0)
                    for j in range(3)).astype(BF16)
        q_ext = jnp.where((lane >= ext0) & (lane < ext0 + 3), 1.0, 0.0).astype(BF16)
        for r0 in range(0, t_total, tq):
            rs = slice(r0, r0 + tq)
            k_ext = _dot(f_ref[rs, :], place).astype(BF16)
            qf_ref[e, rs, :] = jnp.where(own, q_ref[rs, :], q_ext)
            kf_ref[e, rs, :] = jnp.where(own, k_ref[rs, :], k_ext)
            vf_ref[e, rs, :] = jnp.where(own, v_ref[rs, :], jnp.ones((tq, LANES), BF16))

    nq = t_total // tq
    units = list(range(nq - 1, -1, -1))
    row_max = {}

    def score_stage(i):
        sbuf = s_ref.at[units.index(i) % 2]
        qs = [qf_ref[h, i * tq:(i + 1) * tq, :] for h in range(2)]
        mrun = [jnp.full((tq, LANES), NEG_INF, F32)] * 2
        for c in range(i + 1):
            for h in range(2):
                s = _dot_nt(qs[h], kf_ref[h, c * tq:(c + 1) * tq, :])
                if c == i:
                    s = jnp.where(causal, s, NEG_INF)
                sbuf[h, c] = s
                mrun[h] = _lane_block_max(mrun[h], s)
            yield
        row_max[i] = [_row_max_tiled(mrun[h], tq // LANES) for h in range(2)]

    def value_stage(i):
        sbuf = s_ref.at[units.index(i) % 2]
        m = row_max.pop(i)
        acc = [jnp.zeros((tq, LANES), F32)] * 2
        for c in range(i + 1):
            for h in range(2):
                p = jnp.exp2(sbuf[h, c] - m[h]).astype(BF16)
                acc[h] = acc[h] + _dot(p, vf_ref[h, c * tq:(c + 1) * tq, :])
            yield
        low = lane < HEAD_DIM
        slab = jnp.where(low, acc[0], acc[1]) / pltpu.roll(jnp.where(low, acc[1], acc[0]), HEAD_DIM, 1)
        z = z_ref[i * tq:(i + 1) * tq, :].astype(F32)
        y_ref[i * tq:(i + 1) * tq, :] = (slab * (z * _sigmoid(z))).astype(BF16)

    _emit_pipelined(units, score_stage, value_stage)


def _fox_attn(main3, f3):
    b, t, _ = main3.shape
    nslab = FOX_HEADS // 2

    def col(base):
        return pl.BlockSpec((None, t, LANES), lambda i, p: (i, 0, base // LANES + p))

    return pl.pallas_call(
        _fox_kernel,
        grid=(b, nslab),
        in_specs=[col(COL_QB), col(COL_KB), col(COL_VB),
                  pl.BlockSpec((None, t, LANES), lambda i, p: (i, 0, 0)),
                  col(COL_ZB)],
        out_specs=pl.BlockSpec((None, t, LANES), lambda i, p: (i, 0, p)),
        out_shape=jax.ShapeDtypeStruct((b, t, 1024), BF16),
        scratch_shapes=[pltpu.VMEM((2, t, LANES), BF16), pltpu.VMEM((2, t, LANES), BF16),
                        pltpu.VMEM((2, t, LANES), BF16),
                        pltpu.VMEM((2, 2, t // TQ_FOX, TQ_FOX, TQ_FOX), F32)],
        compiler_params=_cparams(("parallel", "parallel")),
        name="fox_attn",
    )(main3, main3, main3, f3, main3)


def _out_kernel(x_ref, ya_ref, yb_ref, ra_ref, rb_ref, wn_ref, wf_ref, wo_ref, fg_ref, o_ref, *, final):
    ta = _dot(ya_ref[...], wn_ref[...])
    tb = _dot(yb_ref[...], wf_ref[...])
    merged = _sigmoid(ra_ref[...].astype(F32)) * ta + _sigmoid(rb_ref[...].astype(F32)) * tb
    out = x_ref[...] + _dot(merged.astype(BF16), wo_ref[...])
    if final:
        out = out * lax.rsqrt(jnp.mean(out * out, axis=-1, keepdims=True) + RMS_EPS) * fg_ref[...]
    o_ref[...] = out


def _out_proj(x2, ya2, yb2, main2, wn, wf, wo, fg, final, tm=1024):
    n = x2.shape[0]
    tm = min(tm, n)
    row = lambda c: pl.BlockSpec((tm, 1024), lambda i: (i, c))
    w = pl.BlockSpec((1024, 1024), lambda i: (0, 0))
    return pl.pallas_call(
        functools.partial(_out_kernel, final=final),
        grid=(n // tm,),
        in_specs=[row(0), row(0), row(0), row(COL_RA // 1024), row(COL_RB // 1024), w, w, w,
                  pl.BlockSpec((1, 1024), lambda i: (0, 0))],
        out_specs=row(0),
        out_shape=jax.ShapeDtypeStruct((n, 1024), F32),
        compiler_params=_cparams(("parallel",)),
        name="out_proj",
    )(x2, ya2, yb2, main2, main2, wn, wf, wo, fg)


def _rope_tables(t):
    inv = ROPE_THETA ** (-jnp.arange(0, HEAD_DIM, 2, dtype=F32) / HEAD_DIM)
    ang = jnp.arange(t, dtype=F32)[:, None] * inv[None, :]
    c, s = jnp.cos(ang), jnp.sin(ang)
    return jnp.concatenate([c, c, c, c], axis=1), jnp.concatenate([-s, s, -s, s], axis=1)


def _overlap_matrix():
    n = np.arange(LANES)[:, None] * CMP_STRIDE
    j = np.arange(LANES)[None, :] * SEL_BLOCK
    m = (n < j + SEL_BLOCK) & (n + CMP_BLOCK > j) & (np.arange(LANES)[None, :] < N_SEL)
    return jnp.asarray(m.astype(np.float32), dtype=BF16)


def _gate_expansion():
    m = np.zeros((3, LANES, NSA_HEADS * HEAD_DIM), np.float32)
    for br in range(3):
        for h in range(NSA_HEADS):
            m[br, br * NSA_HEADS + h, h * HEAD_DIM:(h + 1) * HEAD_DIM] = 1.0
    return jnp.asarray(m, dtype=BF16)


def _reorder_w_in(w):
    qa, kv, ga, za, qb, kb, vb, fb, zb, mg = jnp.split(
        w, np.cumsum([1024, 768, 48, 1024, 1024, 1024, 1024, 16, 1024])[:9].tolist(), axis=1)
    main = jnp.concatenate([qa, za, qb * (SCALE * LOG2E), kb, vb, zb, mg, kv], axis=1).astype(BF16)
    d = w.shape[0]
    small = jnp.concatenate([ga, jnp.zeros((d, LANES - 48), F32), fb, jnp.zeros((d, LANES - 16), F32)],
                            axis=1).astype(BF16)
    return main, small


def _compress_weights(w1, w2):
    eye = jnp.eye(NSA_GROUPS, dtype=F32)
    w1r = w1.reshape(CMP_BLOCK, HEAD_DIM, CMP_HIDDEN)
    half = CMP_BLOCK // 2

    def blockdiag(wl):
        return jnp.einsum('ldh,gk->lgdkh', wl, eye).reshape(half * NSA_GROUPS * HEAD_DIM,
                                                            NSA_GROUPS * CMP_HIDDEN).astype(BF16)

    wa, wb = blockdiag(w1r[:half]), blockdiag(w1r[half:])
    w2d = jnp.einsum('hd,gk,c->gkhcd', w2, eye, jnp.ones((2,), F32)).reshape(
        NSA_GROUPS, NSA_GROUPS * CMP_HIDDEN, LANES).astype(BF16)
    return wa, wb, w2d


def kernel(x, norm_g, w_in, b_forget, cmp_pe_k, cmp_w1_k, cmp_w2_k, cmp_pe_v, cmp_w1_v, cmp_w2_v,
           w_proj_nsa, w_proj_fox, w_out, final_g):
    b, t, d = x.shape
    depth = norm_g.shape[0]
    n = b * t
    nc = t // CMP_STRIDE
    assert d == D_MODEL and nc == LANES and t // SEL_BLOCK == N_SEL
    cos, sin = _rope_tables(t)
    ovl = _overlap_matrix()
    gmat = _gate_expansion()
    fg = final_g.reshape(1, d)
    x2 = x.reshape(n, d)
    for l in range(depth):
        w_main, w_small = _reorder_w_in(w_in[l])
        main2, small2, kc2, vc2 = _inproj(x2, norm_g[l].reshape(1, d), w_main, w_small)
        main3 = main2.reshape(b, t, MAIN_COLS)
        small3 = small2.reshape(b, t, SMALL_COLS)
        xk = kc2.reshape(b, nc, CMP_STRIDE * LANES)
        xv = vc2.reshape(b, nc, CMP_STRIDE * LANES)
        wak, wbk, w2k = _compress_weights(cmp_w1_k[l], cmp_w2_k[l])
        wav, wbv, w2v = _compress_weights(cmp_w1_v[l], cmp_w2_v[l])
        pek = jnp.broadcast_to(cmp_pe_k[l].reshape(1, -1), (8, CMP_BLOCK * HEAD_DIM))
        pev = jnp.broadcast_to(cmp_pe_v[l].reshape(1, -1), (8, CMP_BLOCK * HEAD_DIM))
        kc, vc = _compress(xk, xv, pek, pev, cmp_w1_k[l].astype(BF16), cmp_w1_v[l].astype(BF16),
                           wak, wbk, wav, wbv, w2k, w2v)
        qh, ks, kw, vs, vw, ge, o_cmp, nsel = _nsa_front(main3, small3, cos, sin, gmat, kc, vc, ovl)
        o_cs = _slc_attn(qh, ks, vs, nsel, ge, o_cmp)
        y_a = _win_attn(qh, kw, vw, ge, o_cs, main3)
        b_pad = jnp.concatenate([b_forget[l], jnp.zeros((LANES - FOX_HEADS,), F32)]).reshape(1, LANES)
        y_b = _fox_attn(main3, _decay(small3, b_pad))
        x2 = _out_proj(x2, y_a.reshape(n, 1024), y_b.reshape(n, 1024), main2,
                       w_proj_nsa[l].astype(BF16), w_proj_fox[l].astype(BF16), w_out[l].astype(BF16),
                       fg, final=(l == depth - 1))
    return x2.reshape(b, t, d)
```

```python
import functools

import numpy as np
import jax
import jax.numpy as jnp
from jax import lax
from jax.experimental import pallas as pl
from jax.experimental.pallas import tpu as pltpu

F32 = jnp.float32
BF16 = jnp.bfloat16

D_MODEL = 1024
HEAD_DIM = 64
LANES = 128
NSA_HEADS = 16
NSA_GROUPS = 2
NSA_HPG = NSA_HEADS // NSA_GROUPS
CMP_BLOCK = 32
CMP_STRIDE = 16
CMP_HIDDEN = 2 * HEAD_DIM
SEL_BLOCK = 64
SEL_TOPK = 8
N_SEL = 32
WINDOW = 512
FOX_HEADS = 16
ROPE_THETA = 10000.0
RMS_EPS = 1e-6
NEG_INF = -1e30
FORCED_SCORE = 1e4
SCALE = HEAD_DIM ** -0.5
LOG2E = 1.4426950408889634
EXT = HEAD_DIM

COL_QA, COL_ZA, COL_QB, COL_KB, COL_VB, COL_ZB, COL_RA, COL_RB, COL_KV = (
    0, 1024, 2048, 3072, 4096, 5120, 6144, 7168, 8192)
MAIN_COLS = 8960
SMALL_COLS = 256

TQ_NSA = 128
TK_SLC = 256
TQ_FOX = 256
VMEM_LIMIT = 56 * 1024 * 1024


def _cparams(sem):
    return pltpu.CompilerParams(dimension_semantics=sem, vmem_limit_bytes=VMEM_LIMIT)


def _lane_iota(shape):
    return lax.broadcasted_iota(jnp.int32, shape, len(shape) - 1)


def _dot_nt(a, b):
    return lax.dot_general(a, b, (((1,), (1,)), ((), ())), preferred_element_type=F32)


def _dot(a, b):
    return jnp.dot(a, b, preferred_element_type=F32)


def _split3(x):
    hi = x.astype(BF16).astype(F32)
    r = x - hi
    mid = r.astype(BF16).astype(F32)
    lo = (r - mid).astype(BF16).astype(F32)
    return hi, mid, lo


def _split3_dot(a, m):
    hi, mid, lo = _split3(a)
    return _dot(hi.astype(BF16), m) + _dot(mid.astype(BF16), m) + _dot(lo.astype(BF16), m)


def _sigmoid(x):
    return 1.0 / (1.0 + jnp.exp(-x))


def _rope_slab(x, cos, sin):
    lane = _lane_iota(x.shape)
    swapped = jnp.where((lane % HEAD_DIM) < HEAD_DIM // 2,
                        pltpu.roll(x, LANES - HEAD_DIM // 2, 1), pltpu.roll(x, HEAD_DIM // 2, 1))
    return x * cos + swapped * sin


def _stack_heads(slabs):
    lane = _lane_iota(slabs[0].shape)
    parts = []
    for s in slabs:
        parts.append(jnp.where(lane < HEAD_DIM, s, jnp.zeros_like(s)))
        parts.append(jnp.where(lane >= HEAD_DIM, s, jnp.zeros_like(s)))
    return jnp.concatenate(parts, axis=0)


def _unstack_heads(o, nslab, rows):
    lane = _lane_iota((rows, LANES))
    out = []
    for j in range(nslab):
        a = o[(2 * j) * rows:(2 * j + 1) * rows]
        b = o[(2 * j + 1) * rows:(2 * j + 2) * rows]
        out.append(jnp.where(lane < HEAD_DIM, a, b))
    return out


def _inproj_kernel(x_ref, g_ref, w_ref, ws_ref, main_ref, small_ref, kc_ref, vc_ref, xn_ref, *, kv_off):
    @pl.when(pl.program_id(1) == 0)
    def _():
        x = x_ref[...]
        y = x * lax.rsqrt(jnp.mean(x * x, axis=-1, keepdims=True) + RMS_EPS) * g_ref[...]
        xn = y.astype(BF16)
        xn_ref[...] = xn
        small_ref[...] = _dot(xn, ws_ref[...])

    res = _dot(xn_ref[...], w_ref[...]).astype(BF16)
    main_ref[...] = res

    @pl.when(pl.program_id(1) == pl.num_programs(1) - 1)
    def _():
        kc_ref[...] = res[:, kv_off:kv_off + LANES]
        vc_ref[...] = res[:, kv_off + LANES:kv_off + 2 * LANES]


def _inproj(x2, g, w_main, w_small, tm=2048, tn=1280):
    n = x2.shape[0]
    tm = min(tm, n)
    kv_off = COL_KV - (MAIN_COLS - tn)
    assert 0 <= kv_off and kv_off + 2 * LANES <= tn
    return pl.pallas_call(
        functools.partial(_inproj_kernel, kv_off=kv_off),
        grid=(n // tm, MAIN_COLS // tn),
        in_specs=[
            pl.BlockSpec((tm, D_MODEL), lambda i, j: (i, 0)),
            pl.BlockSpec((1, D_MODEL), lambda i, j: (0, 0)),
            pl.BlockSpec((D_MODEL, tn), lambda i, j: (0, j)),
            pl.BlockSpec((D_MODEL, SMALL_COLS), lambda i, j: (0, 0)),
        ],
        out_specs=[
            pl.BlockSpec((tm, tn), lambda i, j: (i, j)),
            pl.BlockSpec((tm, SMALL_COLS), lambda i, j: (i, 0)),
            pl.BlockSpec((tm, LANES), lambda i, j: (i, 0)),
            pl.BlockSpec((tm, LANES), lambda i, j: (i, 0)),
        ],
        out_shape=[jax.ShapeDtypeStruct((n, MAIN_COLS), BF16),
                   jax.ShapeDtypeStruct((n, SMALL_COLS), F32),
                   jax.ShapeDtypeStruct((n, LANES), BF16),
                   jax.ShapeDtypeStruct((n, LANES), BF16)],
        scratch_shapes=[pltpu.VMEM((tm, D_MODEL), BF16)],
        compiler_params=_cparams(("parallel", "arbitrary")),
        name="inproj",
    )(x2, g, w_main, w_small)


def _compress_kernel(xk_ref, xv_ref, pek_ref, pev_ref, w1k_ref, w1v_ref,
                     wak_ref, wbk_ref, wav_ref, wbv_ref, w2k_ref, w2v_ref, kc_ref, vc_ref):
    def one(x_ref, pe_ref, w1_ref, wa_ref, wb_ref, w2_ref, out_ref, out_scale):
        x = x_ref[...]
        a = _dot(x, wa_ref[...])
        b = _dot(x, wb_ref[...])
        nc = a.shape[0]
        b_up = pltpu.roll(b, nc - 1, 0)
        pe_c = _dot(pe_ref[...].astype(BF16), w1_ref[...])[0:1]
        pe_c = jnp.concatenate([pe_c, pe_c], axis=1)
        hid = a + b_up + pe_c
        hid = (hid * _sigmoid(hid)).astype(BF16)
        for g in range(NSA_GROUPS):
            out_ref[g] = (_dot(hid, w2_ref[g]) * out_scale).astype(BF16)

    one(xk_ref, pek_ref, w1k_ref, wak_ref, wbk_ref, w2k_ref, kc_ref, SCALE * LOG2E)
    one(xv_ref, pev_ref, w1v_ref, wav_ref, wbv_ref, w2v_ref, vc_ref, 1.0)


def _compress(xk, xv, pek, pev, w1k, w1v, wak, wbk, wav, wbv, w2k, w2v):
    b, nc, kw = xk.shape
    full = lambda a: pl.BlockSpec(a.shape, lambda i: (0,) * a.ndim)
    xs = pl.BlockSpec((None, nc, kw), lambda i: (i, 0, 0))
    os_ = pl.BlockSpec((None, NSA_GROUPS, nc, LANES), lambda i: (i, 0, 0, 0))
    return pl.pallas_call(
        _compress_kernel,
        grid=(b,),
        in_specs=[xs, xs] + [full(a) for a in (pek, pev, w1k, w1v, wak, wbk, wav, wbv, w2k, w2v)],
        out_specs=[os_, os_],
        out_shape=[jax.ShapeDtypeStruct((b, NSA_GROUPS, nc, LANES), BF16)] * 2,
        compiler_params=_cparams(("parallel",)),
        name="compress",
    )(xk, xv, pek, pev, w1k, w1v, wak, wbk, wav, wbv, w2k, w2v)


def _nsaprep_kernel(q_ref, s_ref, w_ref, sm_ref, cos_ref, sin_ref, gmat_ref,
                    qh_ref, ks_ref, kw_ref, vs_ref, vw_ref, ge_ref, *, tt):
    cos = cos_ref[...]
    sin = sin_ref[...]
    lane = _lane_iota((tt, LANES))
    t = pl.program_id(1) * tt + lax.broadcasted_iota(jnp.int32, (tt, LANES), 0)
    low = lane < HEAD_DIM
    for j in range(NSA_HEADS // 2):
        x = _rope_slab(q_ref[:, j * LANES:(j + 1) * LANES].astype(F32), cos, sin) * (SCALE * LOG2E)
        qh_ref[2 * j] = jnp.where(low, x, 0.0).astype(BF16)
        qh_ref[2 * j + 1] = jnp.where(low, pltpu.roll(x, HEAD_DIM, 1), 0.0).astype(BF16)

    onehot = jnp.where((lane >= EXT) & (lane - EXT == t // SEL_BLOCK) & (lane < EXT + N_SEL), 1.0, 0.0)
    ks = _rope_slab(s_ref[:, 0:LANES].astype(F32), cos, sin)
    kw = _rope_slab(w_ref[:, 0:LANES].astype(F32), cos, sin)
    for g in range(NSA_GROUPS):
        ks_ref[g] = jnp.where(low, ks if g == 0 else pltpu.roll(ks, HEAD_DIM, 1), onehot).astype(BF16)
        kw_ref[g] = jnp.where(low, kw if g == 0 else pltpu.roll(kw, HEAD_DIM, 1), 0.0).astype(BF16)

    for src_ref, dst_ref in ((s_ref, vs_ref), (w_ref, vw_ref)):
        v = src_ref[:, LANES:2 * LANES].astype(F32)
        v_sw = pltpu.roll(v, HEAD_DIM, 1)
        dst_ref[0, 0] = jnp.where(low, v, 1.0).astype(BF16)
        dst_ref[0, 1] = jnp.where(low, 1.0, v_sw).astype(BF16)
        dst_ref[1, 0] = jnp.where(low, v_sw, 1.0).astype(BF16)
        dst_ref[1, 1] = jnp.where(low, 1.0, v).astype(BF16)
    gates = _sigmoid(sm_ref[:, 0:LANES]).astype(BF16)
    for br in range(3):
        ge_ref[br] = _dot(gates, gmat_ref[br]).astype(BF16)


def _cmp_kernel(q_ref, kc_ref, vc_ref, ovl_ref, ge_ref, o_ref, nsel_ref, *, tq):
    q0 = pl.program_id(1) * tq
    nc = kc_ref.shape[1]
    lane = _lane_iota((tq, LANES))
    t = q0 + lax.broadcasted_iota(jnp.int32, (tq, LANES), 0)
    cmp_valid = (lane * CMP_STRIDE + CMP_BLOCK - 1 <= t) & (lane < nc)
    row_valid = (t >= CMP_BLOCK - 1)[None]
    cur = t // SEL_BLOCK
    forced = (lane == 0) | (lane == cur) | (lane == cur - 1)
    blk_valid = lane * SEL_BLOCK <= t
    row8 = lax.broadcasted_iota(jnp.int32, (8, tq), 0)
    for g in range(NSA_GROUPS):
        qs = _stack_heads([q_ref[:, (g * 4 + j) * LANES:(g * 4 + j + 1) * LANES] for j in range(4)])
        s3 = _dot_nt(qs, kc_ref[g]).reshape(NSA_HPG, tq, nc)
        s3 = jnp.where(cmp_valid[None], s3, NEG_INF)
        m = jnp.max(s3, axis=-1, keepdims=True)
        e = jnp.exp2(s3 - m)
        l = _dot(e.reshape(NSA_HPG * tq, nc).astype(BF16), jnp.ones((nc, LANES), BF16))
        inv = jnp.where(row_valid, 1.0 / l.reshape(NSA_HPG, tq, LANES), 0.0)
        p = (e * inv).astype(BF16)
        o = _dot(p.reshape(NSA_HPG * tq, nc), vc_ref[g])
        for j, slab in enumerate(_unstack_heads(o, 4, tq)):
            cols = slice((g * 4 + j) * LANES, (g * 4 + j + 1) * LANES)
            o_ref[:, cols] = slab * ge_ref[:, cols].astype(F32)
        imp = _split3_dot(jnp.sum(p.astype(F32), axis=0), ovl_ref[...])
        score = jnp.where(forced, FORCED_SCORE, jnp.where(blk_valid, imp, -1.0))
        score = jnp.where(lane < N_SEL, score, -2.0)
        st = score.T[0:N_SEL, :]
        cnt = [jnp.zeros((8, tq), F32) for _ in range(N_SEL // 8)]
        for j in range(N_SEL):
            rj = st[j:j + 1, :]
            for r in range(N_SEL // 8):
                blk = st[8 * r:8 * r + 8, :]
                if j < 8 * r:
                    beats = rj >= blk
                elif j >= 8 * r + 8:
                    beats = rj > blk
                else:
                    beats = (rj > blk) | ((rj == blk) & (row8 + 8 * r > j))
                cnt[r] = cnt[r] + jnp.where(beats, 1.0, 0.0)
        nsel_t = jnp.where(jnp.concatenate(cnt, axis=0) < float(SEL_TOPK), 0.0, NEG_INF)
        nsel_t = jnp.concatenate([jnp.zeros((EXT, tq), F32), nsel_t,
                                  jnp.zeros((LANES - EXT - N_SEL, tq), F32)], axis=0)
        nsel_ref[g] = nsel_t.T.astype(BF16)


def _nsafront_kernel(q_ref, s_ref, w_ref, sm_ref, cos_ref, sin_ref, gmat_ref, kc_ref, vc_ref, ovl_ref,
                     qh_ref, ks_ref, kw_ref, vs_ref, vw_ref, ge_ref, o_ref, nsel_ref, *, tq):
    _nsaprep_kernel(q_ref, s_ref, w_ref, sm_ref, cos_ref, sin_ref, gmat_ref,
                    qh_ref, ks_ref, kw_ref, vs_ref, vw_ref, ge_ref, tt=tq)
    _cmp_kernel(q_ref, kc_ref, vc_ref, ovl_ref, ge_ref.at[0], o_ref, nsel_ref, tq=tq)


def _nsa_front(main3, small3, cos, sin, gmat, kc, vc, ovl, tq=256):
    b, t, _ = main3.shape
    nc = kc.shape[2]
    blk = COL_KV // (2 * LANES)
    kspec = pl.BlockSpec((None, NSA_GROUPS, tq, LANES), lambda i, j: (i, 0, j, 0))
    vspec = pl.BlockSpec((None, NSA_GROUPS, 2, tq, LANES), lambda i, j: (i, 0, 0, j, 0))
    cspec = pl.BlockSpec((None, NSA_GROUPS, nc, LANES), lambda i, j: (i, 0, 0, 0))
    return pl.pallas_call(
        functools.partial(_nsafront_kernel, tq=tq),
        grid=(b, t // tq),
        in_specs=[
            pl.BlockSpec((None, tq, 1024), lambda i, j: (i, j, COL_QA // 1024)),
            pl.BlockSpec((None, tq, 2 * LANES), lambda i, j: (i, j, blk + 1)),
            pl.BlockSpec((None, tq, 2 * LANES), lambda i, j: (i, j, blk + 2)),
            pl.BlockSpec((None, tq, SMALL_COLS), lambda i, j: (i, j, 0)),
            pl.BlockSpec((tq, LANES), lambda i, j: (j, 0)),
            pl.BlockSpec((tq, LANES), lambda i, j: (j, 0)),
            pl.BlockSpec(gmat.shape, lambda i, j: (0, 0, 0)),
            cspec, cspec,
            pl.BlockSpec(ovl.shape, lambda i, j: (0, 0)),
        ],
        out_specs=[
            pl.BlockSpec((None, NSA_HEADS, tq, LANES), lambda i, j: (i, 0, j, 0)),
            kspec, kspec, vspec, vspec,
            pl.BlockSpec((None, 3, tq, 1024), lambda i, j: (i, 0, j, 0)),
            pl.BlockSpec((None, tq, 1024), lambda i, j: (i, j, 0)),
            kspec,
        ],
        out_shape=[jax.ShapeDtypeStruct((b, NSA_HEADS, t, LANES), BF16)]
        + [jax.ShapeDtypeStruct((b, NSA_GROUPS, t, LANES), BF16)] * 2
        + [jax.ShapeDtypeStruct((b, NSA_GROUPS, 2, t, LANES), BF16)] * 2
        + [jax.ShapeDtypeStruct((b, 3, t, 1024), BF16),
           jax.ShapeDtypeStruct((b, t, 1024), F32),
           jax.ShapeDtypeStruct((b, NSA_GROUPS, t, LANES), BF16)],
        compiler_params=_cparams(("parallel", "parallel")),
        name="nsa_front",
    )(main3, main3, main3, small3, cos, sin, gmat, kc, vc, ovl)


def _tile_lanes(x, n):
    return jnp.concatenate([x] * n, axis=1)


def _lane_block_max(mrun, s):
    for w in range(s.shape[1] // LANES):
        mrun = jnp.maximum(mrun, s[:, w * LANES:(w + 1) * LANES])
    return mrun


def _row_max_tiled(mrun, n):
    return _tile_lanes(jnp.broadcast_to(jnp.max(mrun, axis=1, keepdims=True), mrun.shape), n)


def _emit_pipelined(units, score_stage, value_stage):
    def drain(gens):
        gens = [g for g in gens if g is not None]
        while gens:
            for g in list(gens):
                try:
                    next(g)
                except StopIteration:
                    gens.remove(g)

    drain([score_stage(units[0])])
    for n, u in enumerate(units):
        drain([score_stage(units[n + 1]) if n + 1 < len(units) else None, value_stage(u)])


EVEN_ODD = tuple(range(0, NSA_HPG, 2)) + tuple(range(1, NSA_HPG, 2))


def _group_slabs(acc_even, acc_odd, gates, tq):
    low = _lane_iota((tq, LANES)) < HEAD_DIM
    slabs = []
    for j in range(NSA_HPG // 2):
        e = acc_even[j * tq:(j + 1) * tq]
        o = acc_odd[j * tq:(j + 1) * tq]
        den = pltpu.roll(jnp.where(low, o, e), HEAD_DIM, 1)
        slabs.append(jnp.where(low, e, o) / den * gates[:, j * LANES:(j + 1) * LANES].astype(F32))
    return slabs


def _slc_kernel(qh_ref, k_ref, v_ref, nsel_ref, ge_ref, prev_ref, o_ref, s_ref):
    tq, tk = TQ_NSA, TK_SLC
    rows = NSA_HPG * tq
    t_total = k_ref.shape[0]
    nq = t_total // tq
    order = [j // 2 if j % 2 else nq - 1 - j // 2 for j in range(nq)]
    row_max = {}

    def key_chunks(i):
        c_last = i * tq // tk
        return [(c * tk, tk) for c in range(c_last)] + [(c_last * tk, (i + 1) * tq - c_last * tk)]

    def score_stage(i):
        q0 = i * tq
        sbuf = s_ref.at[order.index(i) % 2]
        nsel = nsel_ref[q0:q0 + tq, :]
        qs = jnp.concatenate([qh_ref[h, q0:q0 + tq, :] + nsel for h in EVEN_ODD], axis=0)
        mrun = jnp.full((rows, LANES), NEG_INF, F32)
        pieces = key_chunks(i)
        for n, (k0, w) in enumerate(pieces):
            s = _dot_nt(qs, k_ref[k0:k0 + w, :])
            if n == len(pieces) - 1:
                key = k0 + lax.broadcasted_iota(jnp.int32, (tq, w), 1)
                qry = q0 + lax.broadcasted_iota(jnp.int32, (tq, w), 0)
                causal = jnp.where(key <= qry, 0.0, NEG_INF)
                s = (s.reshape(NSA_HPG, tq, w) + causal[None]).reshape(rows, w)
            sbuf[n, :, 0:w] = s
            mrun = _lane_block_max(mrun, s)
            yield
        row_max[i] = _row_max_tiled(mrun, tk // LANES)

    def value_stage(i):
        q0 = i * tq
        sbuf = s_ref.at[order.index(i) % 2]
        m = row_max.pop(i)
        acc = [jnp.zeros((rows // 2, LANES), F32)] * 2
        for n, (k0, w) in enumerate(key_chunks(i)):
            p = jnp.exp2(sbuf[n, :, 0:w] - m[:, 0:w]).astype(BF16)
            for e in range(2):
                acc[e] = acc[e] + _dot(p[e * rows // 2:(e + 1) * rows // 2], v_ref[e, k0:k0 + w, :])
            yield
        for j, slab in enumerate(_group_slabs(acc[0], acc[1], ge_ref[q0:q0 + tq, :], tq)):
            cols = slice(j * LANES, (j + 1) * LANES)
            o_ref[q0:q0 + tq, cols] = prev_ref[q0:q0 + tq, cols] + slab

    _emit_pipelined(order, score_stage, value_stage)


def _slc_attn(qh, ks, vs, nsel, ge, prev):
    b, _, t, _ = qh.shape
    gw = NSA_HPG * HEAD_DIM
    kv = pl.BlockSpec((None, None, t, LANES), lambda i, g: (i, g, 0, 0))
    return pl.pallas_call(
        _slc_kernel,
        grid=(b, NSA_GROUPS),
        in_specs=[
            pl.BlockSpec((None, NSA_HPG, t, LANES), lambda i, g: (i, g, 0, 0), pipeline_mode=pl.Buffered(1)),
            kv,
            pl.BlockSpec((None, None, 2, t, LANES), lambda i, g: (i, g, 0, 0, 0)),
            kv,
            pl.BlockSpec((None, None, t, gw), lambda i, g: (i, 1, 0, g)),
            pl.BlockSpec((None, t, gw), lambda i, g: (i, 0, g), pipeline_mode=pl.Buffered(1)),
        ],
        out_specs=pl.BlockSpec((None, t, gw), lambda i, g: (i, 0, g)),
        out_shape=jax.ShapeDtypeStruct((b, t, 1024), F32),
        scratch_shapes=[pltpu.VMEM((2, t // TK_SLC, NSA_HPG * TQ_NSA, TK_SLC), F32)],
        compiler_params=_cparams(("parallel", "parallel")),
        name="slc_attn",
    )(qh, ks, vs, nsel, ge, prev)


def _win_kernel(qh_ref, k_ref, v_ref, ge_ref, prev_ref, z_ref, y_ref, s_ref):
    tq = TQ_NSA
    rows = NSA_HPG * tq
    t_total = k_ref.shape[0]
    qry_l = lax.broadcasted_iota(jnp.int32, (tq, LANES), 0)
    key_l = lax.broadcasted_iota(jnp.int32, (tq, LANES), 1)

    step = 2 * LANES
    row_max = {}

    def window(i):
        k_lo = max(i * tq - WINDOW, 0)
        return k_lo, [(c, min(step, i * tq + tq - k_lo - c)) for c in range(0, i * tq + tq - k_lo, step)]

    def score_stage(i):
        q0 = i * tq
        k_lo, pieces = window(i)
        sbuf = s_ref.at[i % 2]
        qs = jnp.concatenate([qh_ref[h, q0:q0 + tq, :] for h in EVEN_ODD], axis=0)
        mrun = jnp.full((rows, LANES), NEG_INF, F32)
        for c0, cw in pieces:
            s = _dot_nt(qs, k_ref[k_lo + c0:k_lo + c0 + cw, :])
            for w in range(cw // LANES):
                k0 = k_lo + c0 + w * LANES
                blk = s[:, w * LANES:(w + 1) * LANES]
                if k0 == q0 or k0 == q0 - WINDOW:
                    dist = (k0 - q0) + key_l - qry_l
                    band = jnp.where((dist <= 0) & (dist > -WINDOW), 0.0, NEG_INF)
                    blk = (blk.reshape(NSA_HPG, tq, LANES) + band[None]).reshape(rows, LANES)
                sbuf[:, c0 + w * LANES:c0 + (w + 1) * LANES] = blk
                mrun = jnp.maximum(mrun, blk)
            yield
        row_max[i] = jnp.broadcast_to(jnp.max(mrun, axis=1, keepdims=True), mrun.shape)

    def value_stage(i):
        q0 = i * tq
        k_lo, pieces = window(i)
        sbuf = s_ref.at[i % 2]
        m = row_max.pop(i)
        acc = [jnp.zeros((rows // 2, LANES), F32)] * 2
        for c0, cw in pieces:
            p = jnp.exp2(sbuf[:, c0:c0 + cw] - _tile_lanes(m, cw // LANES)).astype(BF16)
            for e in range(2):
                acc[e] = acc[e] + _dot(p[e * rows // 2:(e + 1) * rows // 2], v_ref[e, k_lo + c0:k_lo + c0 + cw, :])
            yield
        for j, slab in enumerate(_group_slabs(acc[0], acc[1], ge_ref[q0:q0 + tq, :], tq)):
            cols = slice(j * LANES, (j + 1) * LANES)
            z = z_ref[q0:q0 + tq, cols].astype(F32)
            y = (prev_ref[q0:q0 + tq, cols] + slab) * (z * _sigmoid(z))
            y_ref[q0:q0 + tq, cols] = y.astype(BF16)

    _emit_pipelined(list(range(t_total // tq)), score_stage, value_stage)


def _win_attn(qh, kw, vw, ge, prev, main3):
    b, _, t, _ = qh.shape
    gw = NSA_HPG * HEAD_DIM
    kv = pl.BlockSpec((None, None, t, LANES), lambda i, g: (i, g, 0, 0))
    return pl.pallas_call(
        _win_kernel,
        grid=(b, NSA_GROUPS),
        in_specs=[
            pl.BlockSpec((None, NSA_HPG, t, LANES), lambda i, g: (i, g, 0, 0)),
            kv,
            pl.BlockSpec((None, None, 2, t, LANES), lambda i, g: (i, g, 0, 0, 0)),
            pl.BlockSpec((None, None, t, gw), lambda i, g: (i, 2, 0, g)),
            pl.BlockSpec((None, t, gw), lambda i, g: (i, 0, g)),
            pl.BlockSpec((None, t, gw), lambda i, g: (i, 0, COL_ZA // gw + g)),
        ],
        out_specs=pl.BlockSpec((None, t, gw), lambda i, g: (i, 0, g)),
        out_shape=jax.ShapeDtypeStruct((b, t, 1024), BF16),
        scratch_shapes=[pltpu.VMEM((2, NSA_HPG * TQ_NSA, WINDOW + TQ_NSA), F32)],
        compiler_params=_cparams(("parallel", "parallel")),
        name="win_attn",
    )(qh, kw, vw, ge, prev, main3)


def _decay_kernel(sm_ref, b_ref, f_ref):
    t_total = sm_ref.shape[0]
    x = sm_ref[:, LANES:2 * LANES] + b_ref[...]
    f = jnp.minimum(x, 0.0) - jnp.log1p(jnp.exp(-jnp.abs(x)))
    row = lax.broadcasted_iota(jnp.int32, f.shape, 0)
    sh = 1
    while sh < t_total:
        f = f + jnp.where(row >= sh, pltpu.roll(f, sh, 0), 0.0)
        sh *= 2
    hi, mid, lo = _split3(f * (-LOG2E))
    lane = _lane_iota(f.shape)
    pieces = jnp.where(lane < FOX_HEADS, hi,
                       jnp.where(lane < 2 * FOX_HEADS, pltpu.roll(mid, FOX_HEADS, 1),
                                 jnp.where(lane < 3 * FOX_HEADS, pltpu.roll(lo, 2 * FOX_HEADS, 1), 0.0)))
    f_ref[...] = pieces.astype(BF16)


def _decay(small3, b_pad):
    b, t, _ = small3.shape
    return pl.pallas_call(
        _decay_kernel,
        grid=(b,),
        in_specs=[pl.BlockSpec((None, t, SMALL_COLS), lambda i: (i, 0, 0)),
                  pl.BlockSpec((1, LANES), lambda i: (0, 0))],
        out_specs=pl.BlockSpec((None, t, LANES), lambda i: (i, 0, 0)),
        out_shape=jax.ShapeDtypeStruct((b, t, LANES), BF16),
        compiler_params=_cparams(("parallel",)),
        name="fox_decay",
    )(small3, b_pad)


def _fox_kernel(q_ref, k_ref, v_ref, f_ref, z_ref, y_ref, qf_ref, kf_ref, vf_ref, s_ref):
    tq = TQ_FOX
    t_total = z_ref.shape[0]
    slab_idx = pl.program_id(1)
    causal = (lax.broadcasted_iota(jnp.int32, (tq, tq), 0) >= lax.broadcasted_iota(jnp.int32, (tq, tq), 1))
    lane = _lane_iota((tq, LANES))

    src = lax.broadcasted_iota(jnp.int32, (LANES, LANES), 0)
    dst = lax.broadcasted_iota(jnp.int32, (LANES, LANES), 1)
    for e in range(2):
        own = (lane < HEAD_DIM) if e == 0 else (lane >= HEAD_DIM)
        ext0 = (1 - e) * HEAD_DIM
        head = 2 * slab_idx + e
        place = sum(jnp.where((src == j * FOX_HEADS + head) & (dst == ext0 + j), 1.0, 0.0)
                    for j in range(3)).astype(BF16)
        q_ext = jnp.where((lane >= ext0) & (lane < ext0 + 3), 1.0, 0.0).astype(BF16)
        for r0 in range(0, t_total, tq):
            rs = slice(r0, r0 + tq)
            k_ext = _dot(f_ref[rs, :], place).astype(BF16)
            qf_ref[e, rs, :] = jnp.where(own, q_ref[rs, :], q_ext)
            kf_ref[e, rs, :] = jnp.where(own, k_ref[rs, :], k_ext)
            vf_ref[e, rs, :] = jnp.where(own, v_ref[rs, :], jnp.ones((tq, LANES), BF16))

    nq = t_total // tq
    units = list(range(nq - 1, -1, -1))
    row_max = {}

    def score_stage(i):
        sbuf = s_ref.at[units.index(i) % 2]
        qs = [qf_ref[h, i * tq:(i + 1) * tq, :] for h in range(2)]
        mrun = [jnp.full((tq, LANES), NEG_INF, F32)] * 2
        for c in range(i + 1):
            for h in range(2):
                s = _dot_nt(qs[h], kf_ref[h, c * tq:(c + 1) * tq, :])
                if c == i:
                    s = jnp.where(causal, s, NEG_INF)
                sbuf[h, c] = s
                mrun[h] = _lane_block_max(mrun[h], s)
            yield
        row_max[i] = [_row_max_tiled(mrun[h], tq // LANES) for h in range(2)]

    def value_stage(i):
        sbuf = s_ref.at[units.index(i) % 2]
        m = row_max.pop(i)
        acc = [jnp.zeros((tq, LANES), F32)] * 2
        for c in range(i + 1):
            for h in range(2):
                p = jnp.exp2(sbuf[h, c] - m[h]).astype(BF16)
                acc[h] = acc[h] + _dot(p, vf_ref[h, c * tq:(c + 1) * tq, :])
            yield
        low = lane < HEAD_DIM
        slab = jnp.where(low, acc[0], acc[1]) / pltpu.roll(jnp.where(low, acc[1], acc[0]), HEAD_DIM, 1)
        z = z_ref[i * tq:(i + 1) * tq, :].astype(F32)
        y_ref[i * tq:(i + 1) * tq, :] = (slab * (z * _sigmoid(z))).astype(BF16)

    _emit_pipelined(units, score_stage, value_stage)


def _fox_attn(main3, f3):
    b, t, _ = main3.shape
    nslab = FOX_HEADS // 2

    def col(base):
        return pl.BlockSpec((None, t, LANES), lambda i, p: (i, 0, base // LANES + p))

    return pl.pallas_call(
        _fox_kernel,
        grid=(b, nslab),
        in_specs=[col(COL_QB), col(COL_KB), col(COL_VB),
                  pl.BlockSpec((None, t, LANES), lambda i, p: (i, 0, 0)),
                  col(COL_ZB)],
        out_specs=pl.BlockSpec((None, t, LANES), lambda i, p: (i, 0, p)),
        out_shape=jax.ShapeDtypeStruct((b, t, 1024), BF16),
        scratch_shapes=[pltpu.VMEM((2, t, LANES), BF16), pltpu.VMEM((2, t, LANES), BF16),
                        pltpu.VMEM((2, t, LANES), BF16),
                        pltpu.VMEM((2, 2, t // TQ_FOX, TQ_FOX, TQ_FOX), F32)],
        compiler_params=_cparams(("parallel", "parallel")),
        name="fox_attn",
    )(main3, main3, main3, f3, main3)


def _out_kernel(x_ref, ya_ref, yb_ref, ra_ref, rb_ref, wn_ref, wf_ref, wo_ref, fg_ref, o_ref, *, final):
    ta = _dot(ya_ref[...], wn_ref[...])
    tb = _dot(yb_ref[...], wf_ref[...])
    merged = _sigmoid(ra_ref[...].astype(F32)) * ta + _sigmoid(rb_ref[...].astype(F32)) * tb
    out = x_ref[...] + _dot(merged.astype(BF16), wo_ref[...])
    if final:
        out = out * lax.rsqrt(jnp.mean(out * out, axis=-1, keepdims=True) + RMS_EPS) * fg_ref[...]
    o_ref[...] = out


def _out_proj(x2, ya2, yb2, main2, wn, wf, wo, fg, final, tm=1024):
    n = x2.shape[0]
    tm = min(tm, n)
    row = lambda c: pl.BlockSpec((tm, 1024), lambda i: (i, c))
    w = pl.BlockSpec((1024, 1024), lambda i: (0, 0))
    return pl.pallas_call(
        functools.partial(_out_kernel, final=final),
        grid=(n // tm,),
        in_specs=[row(0), row(0), row(0), row(COL_RA // 1024), row(COL_RB // 1024), w, w, w,
                  pl.BlockSpec((1, 1024), lambda i: (0, 0))],
        out_specs=row(0),
        out_shape=jax.ShapeDtypeStruct((n, 1024), F32),
        compiler_params=_cparams(("parallel",)),
        name="out_proj",
    )(x2, ya2, yb2, main2, main2, wn, wf, wo, fg)


def _rope_tables(t):
    inv = ROPE_THETA ** (-jnp.arange(0, HEAD_DIM, 2, dtype=F32) / HEAD_DIM)
    ang = jnp.arange(t, dtype=F32)[:, None] * inv[None, :]
    c, s = jnp.cos(ang), jnp.sin(ang)
    return jnp.concatenate([c, c, c, c], axis=1), jnp.concatenate([-s, s, -s, s], axis=1)


def _overlap_matrix():
    n = np.arange(LANES)[:, None] * CMP_STRIDE
    j = np.arange(LANES)[None, :] * SEL_BLOCK
    m = (n < j + SEL_BLOCK) & (n + CMP_BLOCK > j) & (np.arange(LANES)[None, :] < N_SEL)
    return jnp.asarray(m.astype(np.float32), dtype=BF16)


def _gate_expansion():
    m = np.zeros((3, LANES, NSA_HEADS * HEAD_DIM), np.float32)
    for br in range(3):
        for h in range(NSA_HEADS):
            m[br, br * NSA_HEADS + h, h * HEAD_DIM:(h + 1) * HEAD_DIM] = 1.0
    return jnp.asarray(m, dtype=BF16)


def _reorder_w_in(w):
    qa, kv, ga, za, qb, kb, vb, fb, zb, mg = jnp.split(
        w, np.cumsum([1024, 768, 48, 1024, 1024, 1024, 1024, 16, 1024])[:9].tolist(), axis=1)
    main = jnp.concatenate([qa, za, qb * (SCALE * LOG2E), kb, vb, zb, mg, kv], axis=1).astype(BF16)
    d = w.shape[0]
    small = jnp.concatenate([ga, jnp.zeros((d, LANES - 48), F32), fb, jnp.zeros((d, LANES - 16), F32)],
                            axis=1).astype(BF16)
    return main, small


def _compress_weights(w1, w2):
    eye = jnp.eye(NSA_GROUPS, dtype=F32)
    w1r = w1.reshape(CMP_BLOCK, HEAD_DIM, CMP_HIDDEN)
    half = CMP_BLOCK // 2

    def blockdiag(wl):
        return jnp.einsum('ldh,gk->lgdkh', wl, eye).reshape(half * NSA_GROUPS * HEAD_DIM,
                                                            NSA_GROUPS * CMP_HIDDEN).astype(BF16)

    wa, wb = blockdiag(w1r[:half]), blockdiag(w1r[half:])
    w2d = jnp.einsum('hd,gk,c->gkhcd', w2, eye, jnp.ones((2,), F32)).reshape(
        NSA_GROUPS, NSA_GROUPS * CMP_HIDDEN, LANES).astype(BF16)
    return wa, wb, w2d


def kernel(x, norm_g, w_in, b_forget, cmp_pe_k, cmp_w1_k, cmp_w2_k, cmp_pe_v, cmp_w1_v, cmp_w2_v,
           w_proj_nsa, w_proj_fox, w_out, final_g):
    b, t, d = x.shape
    depth = norm_g.shape[0]
    n = b * t
    nc = t // CMP_STRIDE
    assert d == D_MODEL and nc == LANES and t // SEL_BLOCK == N_SEL
    cos, sin = _rope_tables(t)
    ovl = _overlap_matrix()
    gmat = _gate_expansion()
    fg = final_g.reshape(1, d)
    x2 = x.reshape(n, d)
    for l in range(depth):
        w_main, w_small = _reorder_w_in(w_in[l])
        main2, small2, kc2, vc2 = _inproj(x2, norm_g[l].reshape(1, d), w_main, w_small)
        main3 = main2.reshape(b, t, MAIN_COLS)
        small3 = small2.reshape(b, t, SMALL_COLS)
        xk = kc2.reshape(b, nc, CMP_STRIDE * LANES)
        xv = vc2.reshape(b, nc, CMP_STRIDE * LANES)
        wak, wbk, w2k = _compress_weights(cmp_w1_k[l], cmp_w2_k[l])
        wav, wbv, w2v = _compress_weights(cmp_w1_v[l], cmp_w2_v[l])
        pek = jnp.broadcast_to(cmp_pe_k[l].reshape(1, -1), (8, CMP_BLOCK * HEAD_DIM))
        pev = jnp.broadcast_to(cmp_pe_v[l].reshape(1, -1), (8, CMP_BLOCK * HEAD_DIM))
        kc, vc = _compress(xk, xv, pek, pev, cmp_w1_k[l].astype(BF16), cmp_w1_v[l].astype(BF16),
                           wak, wbk, wav, wbv, w2k, w2v)
        qh, ks, kw, vs, vw, ge, o_cmp, nsel = _nsa_front(main3, small3, cos, sin, gmat, kc, vc, ovl)
        o_cs = _slc_attn(qh, ks, vs, nsel, ge, o_cmp)
        y_a = _win_attn(qh, kw, vw, ge, o_cs, main3)
        b_pad = jnp.concatenate([b_forget[l], jnp.zeros((LANES - FOX_HEADS,), F32)]).reshape(1, LANES)
        y_b = _fox_attn(main3, _decay(small3, b_pad))
        x2 = _out_proj(x2, y_a.reshape(n, 1024), y_b.reshape(n, 1024), main2,
                       w_proj_nsa[l].astype(BF16), w_proj_fox[l].astype(BF16), w_out[l].astype(BF16),
                       fg, final=(l == depth - 1))
    return x2.reshape(b, t, d)
```

```python
import functools

import numpy as np
import jax
import jax.numpy as jnp
from jax import lax
from jax.experimental import pallas as pl
from jax.experimental.pallas import tpu as pltpu

F32 = jnp.float32
BF16 = jnp.bfloat16

D_MODEL = 1024
HEAD_DIM = 64
LANES = 128
NSA_HEADS = 16
NSA_GROUPS = 2
NSA_HPG = NSA_HEADS // NSA_GROUPS
CMP_BLOCK = 32
CMP_STRIDE = 16
CMP_HIDDEN = 2 * HEAD_DIM
SEL_BLOCK = 64
SEL_TOPK = 8
N_SEL = 32
WINDOW = 512
FOX_HEADS = 16
ROPE_THETA = 10000.0
RMS_EPS = 1e-6
NEG_INF = -1e30
FORCED_SCORE = 1e4
SCALE = HEAD_DIM ** -0.5
LOG2E = 1.4426950408889634
EXT = HEAD_DIM

COL_QA, COL_ZA, COL_QB, COL_KB, COL_VB, COL_ZB, COL_RA, COL_RB, COL_KV = (
    0, 1024, 2048, 3072, 4096, 5120, 6144, 7168, 8192)
MAIN_COLS = 8960
SMALL_COLS = 256

TQ_NSA = 128
TK_SLC = 256
TQ_FOX = 256
VMEM_LIMIT = 56 * 1024 * 1024


def _cparams(sem):
    return pltpu.CompilerParams(dimension_semantics=sem, vmem_limit_bytes=VMEM_LIMIT)


def _lane_iota(shape):
    return lax.broadcasted_iota(jnp.int32, shape, len(shape) - 1)


def _dot_nt(a, b):
    return lax.dot_general(a, b, (((1,), (1,)), ((), ())), preferred_element_type=F32)


def _dot(a, b):
    return jnp.dot(a, b, preferred_element_type=F32)


def _split3(x):
    hi = x.astype(BF16).astype(F32)
    r = x - hi
    mid = r.astype(BF16).astype(F32)
    lo = (r - mid).astype(BF16).astype(F32)
    return hi, mid, lo


def _split3_dot(a, m):
    hi, mid, lo = _split3(a)
    return _dot(hi.astype(BF16), m) + _dot(mid.astype(BF16), m) + _dot(lo.astype(BF16), m)


def _sigmoid(x):
    return 1.0 / (1.0 + jnp.exp(-x))


def _rope_slab(x, cos, sin):
    lane = _lane_iota(x.shape)
    swapped = jnp.where((lane % HEAD_DIM) < HEAD_DIM // 2,
                        pltpu.roll(x, LANES - HEAD_DIM // 2, 1), pltpu.roll(x, HEAD_DIM // 2, 1))
    return x * cos + swapped * sin


def _stack_heads(slabs):
    lane = _lane_iota(slabs[0].shape)
    parts = []
    for s in slabs:
        parts.append(jnp.where(lane < HEAD_DIM, s, jnp.zeros_like(s)))
        parts.append(jnp.where(lane >= HEAD_DIM, s, jnp.zeros_like(s)))
    return jnp.concatenate(parts, axis=0)


def _unstack_heads(o, nslab, rows):
    lane = _lane_iota((rows, LANES))
    out = []
    for j in range(nslab):
        a = o[(2 * j) * rows:(2 * j + 1) * rows]
        b = o[(2 * j + 1) * rows:(2 * j + 2) * rows]
        out.append(jnp.where(lane < HEAD_DIM, a, b))
    return out


def _inproj_kernel(x_ref, g_ref, w_ref, ws_ref, main_ref, small_ref, kc_ref, vc_ref, xn_ref, *, kv_off):
    @pl.when(pl.program_id(1) == 0)
    def _():
        x = x_ref[...]
        y = x * lax.rsqrt(jnp.mean(x * x, axis=-1, keepdims=True) + RMS_EPS) * g_ref[...]
        xn = y.astype(BF16)
        xn_ref[...] = xn
        small_ref[...] = _dot(xn, ws_ref[...])

    res = _dot(xn_ref[...], w_ref[...]).astype(BF16)
    main_ref[...] = res

    @pl.when(pl.program_id(1) == pl.num_programs(1) - 1)
    def _():
        kc_ref[...] = res[:, kv_off:kv_off + LANES]
        vc_ref[...] = res[:, kv_off + LANES:kv_off + 2 * LANES]


def _inproj(x2, g, w_main, w_small, tm=2048, tn=1280):
    n = x2.shape[0]
    tm = min(tm, n)
    kv_off = COL_KV - (MAIN_COLS - tn)
    assert 0 <= kv_off and kv_off + 2 * LANES <= tn
    return pl.pallas_call(
        functools.partial(_inproj_kernel, kv_off=kv_off),
        grid=(n // tm, MAIN_COLS // tn),
        in_specs=[
            pl.BlockSpec((tm, D_MODEL), lambda i, j: (i, 0)),
            pl.BlockSpec((1, D_MODEL), lambda i, j: (0, 0)),
            pl.BlockSpec((D_MODEL, tn), lambda i, j: (0, j)),
            pl.BlockSpec((D_MODEL, SMALL_COLS), lambda i, j: (0, 0)),
        ],
        out_specs=[
            pl.BlockSpec((tm, tn), lambda i, j: (i, j)),
            pl.BlockSpec((tm, SMALL_COLS), lambda i, j: (i, 0)),
            pl.BlockSpec((tm, LANES), lambda i, j: (i, 0)),
            pl.BlockSpec((tm, LANES), lambda i, j: (i, 0)),
        ],
        out_shape=[jax.ShapeDtypeStruct((n, MAIN_COLS), BF16),
                   jax.ShapeDtypeStruct((n, SMALL_COLS), F32),
                   jax.ShapeDtypeStruct((n, LANES), BF16),
                   jax.ShapeDtypeStruct((n, LANES), BF16)],
        scratch_shapes=[pltpu.VMEM((tm, D_MODEL), BF16)],
        compiler_params=_cparams(("parallel", "arbitrary")),
        name="inproj",
    )(x2, g, w_main, w_small)


def _compress_kernel(xk_ref, xv_ref, pek_ref, pev_ref, w1k_ref, w1v_ref,
                     wak_ref, wbk_ref, wav_ref, wbv_ref, w2k_ref, w2v_ref, kc_ref, vc_ref):
    def one(x_ref, pe_ref, w1_ref, wa_ref, wb_ref, w2_ref, out_ref, out_scale):
        x = x_ref[...]
        a = _dot(x, wa_ref[...])
        b = _dot(x, wb_ref[...])
        nc = a.shape[0]
        b_up = pltpu.roll(b, nc - 1, 0)
        pe_c = _dot(pe_ref[...].astype(BF16), w1_ref[...])[0:1]
        pe_c = jnp.concatenate([pe_c, pe_c], axis=1)
        hid = a + b_up + pe_c
        hid = (hid * _sigmoid(hid)).astype(BF16)
        for g in range(NSA_GROUPS):
            out_ref[g] = (_dot(hid, w2_ref[g]) * out_scale).astype(BF16)

    one(xk_ref, pek_ref, w1k_ref, wak_ref, wbk_ref, w2k_ref, kc_ref, SCALE * LOG2E)
    one(xv_ref, pev_ref, w1v_ref, wav_ref, wbv_ref, w2v_ref, vc_ref, 1.0)


def _compress(xk, xv, pek, pev, w1k, w1v, wak, wbk, wav, wbv, w2k, w2v):
    b, nc, kw = xk.shape
    full = lambda a: pl.BlockSpec(a.shape, lambda i: (0,) * a.ndim)
    xs = pl.BlockSpec((None, nc, kw), lambda i: (i, 0, 0))
    os_ = pl.BlockSpec((None, NSA_GROUPS, nc, LANES), lambda i: (i, 0, 0, 0))
    return pl.pallas_call(
        _compress_kernel,
        grid=(b,),
        in_specs=[xs, xs] + [full(a) for a in (pek, pev, w1k, w1v, wak, wbk, wav, wbv, w2k, w2v)],
        out_specs=[os_, os_],
        out_shape=[jax.ShapeDtypeStruct((b, NSA_GROUPS, nc, LANES), BF16)] * 2,
        compiler_params=_cparams(("parallel",)),
        name="compress",
    )(xk, xv, pek, pev, w1k, w1v, wak, wbk, wav, wbv, w2k, w2v)


def _nsaprep_kernel(q_ref, s_ref, w_ref, sm_ref, cos_ref, sin_ref, gmat_ref,
                    qh_ref, ks_ref, kw_ref, vs_ref, vw_ref, ge_ref, *, tt):
    cos = cos_ref[...]
    sin = sin_ref[...]
    lane = _lane_iota((tt, LANES))
    t = pl.program_id(1) * tt + lax.broadcasted_iota(jnp.int32, (tt, LANES), 0)
    low = lane < HEAD_DIM
    for j in range(NSA_HEADS // 2):
        x = _rope_slab(q_ref[:, j * LANES:(j + 1) * LANES].astype(F32), cos, sin) * (SCALE * LOG2E)
        qh_ref[2 * j] = jnp.where(low, x, 0.0).astype(BF16)
        qh_ref[2 * j + 1] = jnp.where(low, pltpu.roll(x, HEAD_DIM, 1), 0.0).astype(BF16)

    onehot = jnp.where((lane >= EXT) & (lane - EXT == t // SEL_BLOCK) & (lane < EXT + N_SEL), 1.0, 0.0)
    ks = _rope_slab(s_ref[:, 0:LANES].astype(F32), cos, sin)
    kw = _rope_slab(w_ref[:, 0:LANES].astype(F32), cos, sin)
    for g in range(NSA_GROUPS):
        ks_ref[g] = jnp.where(low, ks if g == 0 else pltpu.roll(ks, HEAD_DIM, 1), onehot).astype(BF16)
        kw_ref[g] = jnp.where(low, kw if g == 0 else pltpu.roll(kw, HEAD_DIM, 1), 0.0).astype(BF16)

    for src_ref, dst_ref in ((s_ref, vs_ref), (w_ref, vw_ref)):
        v = src_ref[:, LANES:2 * LANES].astype(F32)
        v_sw = pltpu.roll(v, HEAD_DIM, 1)
        dst_ref[0, 0] = jnp.where(low, v, 1.0).astype(BF16)
        dst_ref[0, 1] = jnp.where(low, 1.0, v_sw).astype(BF16)
        dst_ref[1, 0] = jnp.where(low, v_sw, 1.0).astype(BF16)
        dst_ref[1, 1] = jnp.where(low, 1.0, v).astype(BF16)
    gates = _sigmoid(sm_ref[:, 0:LANES]).astype(BF16)
    for br in range(3):
        ge_ref[br] = _dot(gates, gmat_ref[br]).astype(BF16)


def _cmp_kernel(q_ref, kc_ref, vc_ref, ovl_ref, ge_ref, o_ref, nsel_ref, *, tq):
    q0 = pl.program_id(1) * tq
    nc = kc_ref.shape[1]
    lane = _lane_iota((tq, LANES))
    t = q0 + lax.broadcasted_iota(jnp.int32, (tq, LANES), 0)
    cmp_valid = (lane * CMP_STRIDE + CMP_BLOCK - 1 <= t) & (lane < nc)
    row_valid = (t >= CMP_BLOCK - 1)[None]
    cur = t // SEL_BLOCK
    forced = (lane == 0) | (lane == cur) | (lane == cur - 1)
    blk_valid = lane * SEL_BLOCK <= t
    row8 = lax.broadcasted_iota(jnp.int32, (8, tq), 0)
    for g in range(NSA_GROUPS):
        qs = _stack_heads([q_ref[:, (g * 4 + j) * LANES:(g * 4 + j + 1) * LANES] for j in range(4)])
        s3 = _dot_nt(qs, kc_ref[g]).reshape(NSA_HPG, tq, nc)
        s3 = jnp.where(cmp_valid[None], s3, NEG_INF)
        m = jnp.max(s3, axis=-1, keepdims=True)
        e = jnp.exp2(s3 - m)
        l = _dot(e.reshape(NSA_HPG * tq, nc).astype(BF16), jnp.ones((nc, LANES), BF16))
        inv = jnp.where(row_valid, 1.0 / l.reshape(NSA_HPG, tq, LANES), 0.0)
        p = (e * inv).astype(BF16)
        o = _dot(p.reshape(NSA_HPG * tq, nc), vc_ref[g])
        for j, slab in enumerate(_unstack_heads(o, 4, tq)):
            cols = slice((g * 4 + j) * LANES, (g * 4 + j + 1) * LANES)
            o_ref[:, cols] = slab * ge_ref[:, cols].astype(F32)
        imp = _split3_dot(jnp.sum(p.astype(F32), axis=0), ovl_ref[...])
        score = jnp.where(forced, FORCED_SCORE, jnp.where(blk_valid, imp, -1.0))
        score = jnp.where(lane < N_SEL, score, -2.0)
        st = score.T[0:N_SEL, :]
        cnt = [jnp.zeros((8, tq), F32) for _ in range(N_SEL // 8)]
        for j in range(N_SEL):
            rj = st[j:j + 1, :]
            for r in range(N_SEL // 8):
                blk = st[8 * r:8 * r + 8, :]
                if j < 8 * r:
                    beats = rj >= blk
                elif j >= 8 * r + 8:
                    beats = rj > blk
                else:
                    beats = (rj > blk) | ((rj == blk) & (row8 + 8 * r > j))
                cnt[r] = cnt[r] + jnp.where(beats, 1.0, 0.0)
        nsel_t = jnp.where(jnp.concatenate(cnt, axis=0) < float(SEL_TOPK), 0.0, NEG_INF)
        nsel_t = jnp.concatenate([jnp.zeros((EXT, tq), F32), nsel_t,
                                  jnp.zeros((LANES - EXT - N_SEL, tq), F32)], axis=0)
        nsel_ref[g] = nsel_t.T.astype(BF16)


def _nsafront_kernel(q_ref, s_ref, w_ref, sm_ref, cos_ref, sin_ref, gmat_ref, kc_ref, vc_ref, ovl_ref,
                     qh_ref, ks_ref, kw_ref, vs_ref, vw_ref, ge_ref, o_ref, nsel_ref, *, tq):
    _nsaprep_kernel(q_ref, s_ref, w_ref, sm_ref, cos_ref, sin_ref, gmat_ref,
                    qh_ref, ks_ref, kw_ref, vs_ref, vw_ref, ge_ref, tt=tq)
    _cmp_kernel(q_ref, kc_ref, vc_ref, ovl_ref, ge_ref.at[0], o_ref, nsel_ref, tq=tq)


def _nsa_front(main3, small3, cos, sin, gmat, kc, vc, ovl, tq=256):
    b, t, _ = main3.shape
    nc = kc.shape[2]
    blk = COL_KV // (2 * LANES)
    kspec = pl.BlockSpec((None, NSA_GROUPS, tq, LANES), lambda i, j: (i, 0, j, 0))
    vspec = pl.BlockSpec((None, NSA_GROUPS, 2, tq, LANES), lambda i, j: (i, 0, 0, j, 0))
    cspec = pl.BlockSpec((None, NSA_GROUPS, nc, LANES), lambda i, j: (i, 0, 0, 0))
    return pl.pallas_call(
        functools.partial(_nsafront_kernel, tq=tq),
        grid=(b, t // tq),
        in_specs=[
            pl.BlockSpec((None, tq, 1024), lambda i, j: (i, j, COL_QA // 1024)),
            pl.BlockSpec((None, tq, 2 * LANES), lambda i, j: (i, j, blk + 1)),
            pl.BlockSpec((None, tq, 2 * LANES), lambda i, j: (i, j, blk + 2)),
            pl.BlockSpec((None, tq, SMALL_COLS), lambda i, j: (i, j, 0)),
            pl.BlockSpec((tq, LANES), lambda i, j: (j, 0)),
            pl.BlockSpec((tq, LANES), lambda i, j: (j, 0)),
            pl.BlockSpec(gmat.shape, lambda i, j: (0, 0, 0)),
            cspec, cspec,
            pl.BlockSpec(ovl.shape, lambda i, j: (0, 0)),
        ],
        out_specs=[
            pl.BlockSpec((None, NSA_HEADS, tq, LANES), lambda i, j: (i, 0, j, 0)),
            kspec, kspec, vspec, vspec,
            pl.BlockSpec((None, 3, tq, 1024), lambda i, j: (i, 0, j, 0)),
            pl.BlockSpec((None, tq, 1024), lambda i, j: (i, j, 0)),
            kspec,
        ],
        out_shape=[jax.ShapeDtypeStruct((b, NSA_HEADS, t, LANES), BF16)]
        + [jax.ShapeDtypeStruct((b, NSA_GROUPS, t, LANES), BF16)] * 2
        + [jax.ShapeDtypeStruct((b, NSA_GROUPS, 2, t, LANES), BF16)] * 2
        + [jax.ShapeDtypeStruct((b, 3, t, 1024), BF16),
           jax.ShapeDtypeStruct((b, t, 1024), F32),
           jax.ShapeDtypeStruct((b, NSA_GROUPS, t, LANES), BF16)],
        compiler_params=_cparams(("parallel", "parallel")),
        name="nsa_front",
    )(main3, main3, main3, small3, cos, sin, gmat, kc, vc, ovl)


def _tile_lanes(x, n):
    return jnp.concatenate([x] * n, axis=1)


def _lane_block_max(mrun, s):
    for w in range(s.shape[1] // LANES):
        mrun = jnp.maximum(mrun, s[:, w * LANES:(w + 1) * LANES])
    return mrun


def _row_max_tiled(mrun, n):
    return _tile_lanes(jnp.broadcast_to(jnp.max(mrun, axis=1, keepdims=True), mrun.shape), n)


def _emit_pipelined(units, score_stage, value_stage):
    def drain(gens):
        gens = [g for g in gens if g is not None]
        while gens:
            for g in list(gens):
                try:
                    next(g)
                except StopIteration:
                    gens.remove(g)

    drain([score_stage(units[0])])
    for n, u in enumerate(units):
        drain([score_stage(units[n + 1]) if n + 1 < len(units) else None, value_stage(u)])


EVEN_ODD = tuple(range(0, NSA_HPG, 2)) + tuple(range(1, NSA_HPG, 2))


def _group_slabs(acc_even, acc_odd, gates, tq):
    low = _lane_iota((tq, LANES)) < HEAD_DIM
    slabs = []
    for j in range(NSA_HPG // 2):
        e = acc_even[j * tq:(j + 1) * tq]
        o = acc_odd[j * tq:(j + 1) * tq]
        den = pltpu.roll(jnp.where(low, o, e), HEAD_DIM, 1)
        slabs.append(jnp.where(low, e, o) / den * gates[:, j * LANES:(j + 1) * LANES].astype(F32))
    return slabs


def _slc_kernel(qh_ref, k_ref, v_ref, nsel_ref, ge_ref, prev_ref, o_ref, s_ref):
    tq, tk = TQ_NSA, TK_SLC
    rows = NSA_HPG * tq
    t_total = k_ref.shape[0]
    nq = t_total // tq
    order = [j // 2 if j % 2 else nq - 1 - j // 2 for j in range(nq)]
    row_max = {}

    def key_chunks(i):
        c_last = i * tq // tk
        return [(c * tk, tk) for c in range(c_last)] + [(c_last * tk, (i + 1) * tq - c_last * tk)]

    def score_stage(i):
        q0 = i * tq
        sbuf = s_ref.at[order.index(i) % 2]
        nsel = nsel_ref[q0:q0 + tq, :]
        qs = jnp.concatenate([qh_ref[h, q0:q0 + tq, :] + nsel for h in EVEN_ODD], axis=0)
        mrun = jnp.full((rows, LANES), NEG_INF, F32)
        pieces = key_chunks(i)
        for n, (k0, w) in enumerate(pieces):
            s = _dot_nt(qs, k_ref[k0:k0 + w, :])
            if n == len(pieces) - 1:
                key = k0 + lax.broadcasted_iota(jnp.int32, (tq, w), 1)
                qry = q0 + lax.broadcasted_iota(jnp.int32, (tq, w), 0)
                causal = jnp.where(key <= qry, 0.0, NEG_INF)
                s = (s.reshape(NSA_HPG, tq, w) + causal[None]).reshape(rows, w)
            sbuf[n, :, 0:w] = s
            mrun = _lane_block_max(mrun, s)
            yield
        row_max[i] = _row_max_tiled(mrun, tk // LANES)

    def value_stage(i):
        q0 = i * tq
        sbuf = s_ref.at[order.index(i) % 2]
        m = row_max.pop(i)
        acc = [jnp.zeros((rows // 2, LANES), F32)] * 2
        for n, (k0, w) in enumerate(key_chunks(i)):
            p = jnp.exp2(sbuf[n, :, 0:w] - m[:, 0:w]).astype(BF16)
            for e in range(2):
                acc[e] = acc[e] + _dot(p[e * rows // 2:(e + 1) * rows // 2], v_ref[e, k0:k0 + w, :])
            yield
        for j, slab in enumerate(_group_slabs(acc[0], acc[1], ge_ref[q0:q0 + tq, :], tq)):
            cols = slice(j * LANES, (j + 1) * LANES)
            o_ref[q0:q0 + tq, cols] = prev_ref[q0:q0 + tq, cols] + slab

    _emit_pipelined(order, score_stage, value_stage)


def _slc_attn(qh, ks, vs, nsel, ge, prev):
    b, _, t, _ = qh.shape
    gw = NSA_HPG * HEAD_DIM
    kv = pl.BlockSpec((None, None, t, LANES), lambda i, g: (i, g, 0, 0))
    return pl.pallas_call(
        _slc_kernel,
        grid=(b, NSA_GROUPS),
        in_specs=[
            pl.BlockSpec((None, NSA_HPG, t, LANES), lambda i, g: (i, g, 0, 0), pipeline_mode=pl.Buffered(1)),
            kv,
            pl.BlockSpec((None, None, 2, t, LANES), lambda i, g: (i, g, 0, 0, 0)),
            kv,
            pl.BlockSpec((None, None, t, gw), lambda i, g: (i, 1, 0, g)),
            pl.BlockSpec((None, t, gw), lambda i, g: (i, 0, g), pipeline_mode=pl.Buffered(1)),
        ],
        out_specs=pl.BlockSpec((None, t, gw), lambda i, g: (i, 0, g)),
        out_shape=jax.ShapeDtypeStruct((b, t, 1024), F32),
        scratch_shapes=[pltpu.VMEM((2, t // TK_SLC, NSA_HPG * TQ_NSA, TK_SLC), F32)],
        compiler_params=_cparams(("parallel", "parallel")),
        name="slc_attn",
    )(qh, ks, vs, nsel, ge, prev)


def _win_kernel(qh_ref, k_ref, v_ref, ge_ref, prev_ref, z_ref, y_ref, s_ref):
    tq = TQ_NSA
    rows = NSA_HPG * tq
    t_total = k_ref.shape[0]
    qry_l = lax.broadcasted_iota(jnp.int32, (tq, LANES), 0)
    key_l = lax.broadcasted_iota(jnp.int32, (tq, LANES), 1)

    step = 2 * LANES
    row_max = {}

    def window(i):
        k_lo = max(i * tq - WINDOW, 0)
        return k_lo, [(c, min(step, i * tq + tq - k_lo - c)) for c in range(0, i * tq + tq - k_lo, step)]

    def score_stage(i):
        q0 = i * tq
        k_lo, pieces = window(i)
        sbuf = s_ref.at[i % 2]
        qs = jnp.concatenate([qh_ref[h, q0:q0 + tq, :] for h in EVEN_ODD], axis=0)
        mrun = jnp.full((rows, LANES), NEG_INF, F32)
        for c0, cw in pieces:
            s = _dot_nt(qs, k_ref[k_lo + c0:k_lo + c0 + cw, :])
            for w in range(cw // LANES):
                k0 = k_lo + c0 + w * LANES
                blk = s[:, w * LANES:(w + 1) * LANES]
                if k0 == q0 or k0 == q0 - WINDOW:
                    dist = (k0 - q0) + key_l - qry_l
                    band = jnp.where((dist <= 0) & (dist > -WINDOW), 0.0, NEG_INF)
                    blk = (blk.reshape(NSA_HPG, tq, LANES) + band[None]).reshape(rows, LANES)
                sbuf[:, c0 + w * LANES:c0 + (w + 1) * LANES] = blk
                mrun = jnp.maximum(mrun, blk)
            yield
        row_max[i] = jnp.broadcast_to(jnp.max(mrun, axis=1, keepdims=True), mrun.shape)

    def value_stage(i):
        q0 = i * tq
        k_lo, pieces = window(i)
        sbuf = s_ref.at[i % 2]
        m = row_max.pop(i)
        acc = [jnp.zeros((rows // 2, LANES), F32)] * 2
        for c0, cw in pieces:
            p = jnp.exp2(sbuf[:, c0:c0 + cw] - _tile_lanes(m, cw // LANES)).astype(BF16)
            for e in range(2):
                acc[e] = acc[e] + _dot(p[e * rows // 2:(e + 1) * rows // 2], v_ref[e, k_lo + c0:k_lo + c0 + cw, :])
            yield
        for j, slab in enumerate(_group_slabs(acc[0], acc[1], ge_ref[q0:q0 + tq, :], tq)):
            cols = slice(j * LANES, (j + 1) * LANES)
            z = z_ref[q0:q0 + tq, cols].astype(F32)
            y = (prev_ref[q0:q0 + tq, cols] + slab) * (z * _sigmoid(z))
            y_ref[q0:q0 + tq, cols] = y.astype(BF16)

    _emit_pipelined(list(range(t_total // tq)), score_stage, value_stage)


def _win_attn(qh, kw, vw, ge, prev, main3):
    b, _, t, _ = qh.shape
    gw = NSA_HPG * HEAD_DIM
    kv = pl.BlockSpec((None, None, t, LANES), lambda i, g: (i, g, 0, 0))
    return pl.pallas_call(
        _win_kernel,
        grid=(b, NSA_GROUPS),
        in_specs=[
            pl.BlockSpec((None, NSA_HPG, t, LANES), lambda i, g: (i, g, 0, 0)),
            kv,
            pl.BlockSpec((None, None, 2, t, LANES), lambda i, g: (i, g, 0, 0, 0)),
            pl.BlockSpec((None, None, t, gw), lambda i, g: (i, 2, 0, g)),
            pl.BlockSpec((None, t, gw), lambda i, g: (i, 0, g)),
            pl.BlockSpec((None, t, gw), lambda i, g: (i, 0, COL_ZA // gw + g)),
        ],
        out_specs=pl.BlockSpec((None, t, gw), lambda i, g: (i, 0, g)),
        out_shape=jax.ShapeDtypeStruct((b, t, 1024), BF16),
        scratch_shapes=[pltpu.VMEM((2, NSA_HPG * TQ_NSA, WINDOW + TQ_NSA), F32)],
        compiler_params=_cparams(("parallel", "parallel")),
        name="win_attn",
    )(qh, kw, vw, ge, prev, main3)


def _decay_kernel(sm_ref, b_ref, f_ref):
    t_total = sm_ref.shape[0]
    x = sm_ref[:, LANES:2 * LANES] + b_ref[...]
    f = jnp.minimum(x, 0.0) - jnp.log1p(jnp.exp(-jnp.abs(x)))
    row = lax.broadcasted_iota(jnp.int32, f.shape, 0)
    sh = 1
    while sh < t_total:
        f = f + jnp.where(row >= sh, pltpu.roll(f, sh, 0), 0.0)
        sh *= 2
    hi, mid, lo = _split3(f * (-LOG2E))
    lane = _lane_iota(f.shape)
    pieces = jnp.where(lane < FOX_HEADS, hi,
                       jnp.where(lane < 2 * FOX_HEADS, pltpu.roll(mid, FOX_HEADS, 1),
                                 jnp.where(lane < 3 * FOX_HEADS, pltpu.roll(lo, 2 * FOX_HEADS, 1), 0.0)))
    f_ref[...] = pieces.astype(BF16)


def _decay(small3, b_pad):
    b, t, _ = small3.shape
    return pl.pallas_call(
        _decay_kernel,
        grid=(b,),
        in_specs=[pl.BlockSpec((None, t, SMALL_COLS), lambda i: (i, 0, 0)),
                  pl.BlockSpec((1, LANES), lambda i: (0, 0))],
        out_specs=pl.BlockSpec((None, t, LANES), lambda i: (i, 0, 0)),
        out_shape=jax.ShapeDtypeStruct((b, t, LANES), BF16),
        compiler_params=_cparams(("parallel",)),
        name="fox_decay",
    )(small3, b_pad)


def _fox_kernel(q_ref, k_ref, v_ref, f_ref, z_ref, y_ref, qf_ref, kf_ref, vf_ref, s_ref):
    tq = TQ_FOX
    t_total = z_ref.shape[0]
    slab_idx = pl.program_id(1)
    half = tq // 2
    causal_top = (lax.broadcasted_iota(jnp.int32, (half, half), 0) >= lax.broadcasted_iota(jnp.int32, (half, half), 1))
    causal_bot = (lax.broadcasted_iota(jnp.int32, (half, tq), 0) + half
                  >= lax.broadcasted_iota(jnp.int32, (half, tq), 1))
    lane = _lane_iota((tq, LANES))

    src = lax.broadcasted_iota(jnp.int32, (LANES, LANES), 0)
    dst = lax.broadcasted_iota(jnp.int32, (LANES, LANES), 1)
    for e in range(2):
        own = (lane < HEAD_DIM) if e == 0 else (lane >= HEAD_DIM)
        ext0 = (1 - e) * HEAD_DIM
        head = 2 * slab_idx + e
        place = sum(jnp.where((src == j * FOX_HEADS + head) & (dst == ext0 + j), 1.0, 0.0)
                    for j in range(3)).astype(BF16)
        q_ext = jnp.where((lane >= ext0) & (lane < ext0 + 3), 1.0, 0.0).astype(BF16)
        for r0 in range(0, t_total, tq):
            rs = slice(r0, r0 + tq)
            k_ext = _dot(f_ref[rs, :], place).astype(BF16)
            qf_ref[e, rs, :] = jnp.where(own, q_ref[rs, :], q_ext)
            kf_ref[e, rs, :] = jnp.where(own, k_ref[rs, :], k_ext)
            vf_ref[e, rs, :] = jnp.where(own, v_ref[rs, :], jnp.ones((tq, LANES), BF16))

    nq = t_total // tq
    units = list(range(nq - 1, -1, -1))
    row_max = {}

    def score_stage(i):
        sbuf = s_ref.at[units.index(i) % 2]
        qs = [qf_ref[h, i * tq:(i + 1) * tq, :] for h in range(2)]
        mrun = [jnp.full((tq, LANES), NEG_INF, F32)] * 2
        for c in range(i):
            for h in range(2):
                s = _dot_nt(qs[h], kf_ref[h, c * tq:(c + 1) * tq, :])
                sbuf[h, c] = s
                mrun[h] = _lane_block_max(mrun[h], s)
            yield
        k0 = i * tq
        for h in range(2):
            s_top = jnp.where(causal_top, _dot_nt(qs[h][:half], kf_ref[h, k0:k0 + half, :]), NEG_INF)
            s_bot = jnp.where(causal_bot, _dot_nt(qs[h][half:], kf_ref[h, k0:k0 + tq, :]), NEG_INF)
            sbuf[h, i, :half, :half] = s_top
            sbuf[h, i, half:, :] = s_bot
            mrun[h] = jnp.concatenate([jnp.maximum(mrun[h][:half], s_top),
                                       _lane_block_max(mrun[h][half:], s_bot)], axis=0)
        yield
        row_max[i] = [_row_max_tiled(mrun[h], tq // LANES) for h in range(2)]

    def value_stage(i):
        sbuf = s_ref.at[units.index(i) % 2]
        m = row_max.pop(i)
        acc = [jnp.zeros((tq, LANES), F32)] * 2
        for c in range(i):
            for h in range(2):
                p = jnp.exp2(sbuf[h, c] - m[h]).astype(BF16)
                acc[h] = acc[h] + _dot(p, vf_ref[h, c * tq:(c + 1) * tq, :])
            yield
        k0 = i * tq
        for h in range(2):
            p_top = jnp.exp2(sbuf[h, i, :half, :half] - m[h][:half, :half]).astype(BF16)
            p_bot = jnp.exp2(sbuf[h, i, half:, :] - m[h][half:]).astype(BF16)
            acc[h] = acc[h] + jnp.concatenate([_dot(p_top, vf_ref[h, k0:k0 + half, :]),
                                               _dot(p_bot, vf_ref[h, k0:k0 + tq, :])], axis=0)
        yield
        low = lane < HEAD_DIM
        slab = jnp.where(low, acc[0], acc[1]) / pltpu.roll(jnp.where(low, acc[1], acc[0]), HEAD_DIM, 1)
        z = z_ref[i * tq:(i + 1) * tq, :].astype(F32)
        y_ref[i * tq:(i + 1) * tq, :] = (slab * (z * _sigmoid(z))).astype(BF16)

    _emit_pipelined(units, score_stage, value_stage)


def _fox_attn(main3, f3):
    b, t, _ = main3.shape
    nslab = FOX_HEADS // 2

    def col(base):
        return pl.BlockSpec((None, t, LANES), lambda i, p: (i, 0, base // LANES + p))

    return pl.pallas_call(
        _fox_kernel,
        grid=(b, nslab),
        in_specs=[col(COL_QB), col(COL_KB), col(COL_VB),
                  pl.BlockSpec((None, t, LANES), lambda i, p: (i, 0, 0)),
                  col(COL_ZB)],
        out_specs=pl.BlockSpec((None, t, LANES), lambda i, p: (i, 0, p)),
        out_shape=jax.ShapeDtypeStruct((b, t, 1024), BF16),
        scratch_shapes=[pltpu.VMEM((2, t, LANES), BF16), pltpu.VMEM((2, t, LANES), BF16),
                        pltpu.VMEM((2, t, LANES), BF16),
                        pltpu.VMEM((2, 2, t // TQ_FOX, TQ_FOX, TQ_FOX), F32)],
        compiler_params=_cparams(("parallel", "parallel")),
        name="fox_attn",
    )(main3, main3, main3, f3, main3)


def _out_kernel(x_ref, ya_ref, yb_ref, ra_ref, rb_ref, wn_ref, wf_ref, wo_ref, fg_ref, o_ref, *, final):
    ta = _dot(ya_ref[...], wn_ref[...])
    tb = _dot(yb_ref[...], wf_ref[...])
    merged = _sigmoid(ra_ref[...].astype(F32)) * ta + _sigmoid(rb_ref[...].astype(F32)) * tb
    out = x_ref[...] + _dot(merged.astype(BF16), wo_ref[...])
    if final:
        out = out * lax.rsqrt(jnp.mean(out * out, axis=-1, keepdims=True) + RMS_EPS) * fg_ref[...]
    o_ref[...] = out


def _out_proj(x2, ya2, yb2, main2, wn, wf, wo, fg, final, tm=1024):
    n = x2.shape[0]
    tm = min(tm, n)
    row = lambda c: pl.BlockSpec((tm, 1024), lambda i: (i, c))
    w = pl.BlockSpec((1024, 1024), lambda i: (0, 0))
    return pl.pallas_call(
        functools.partial(_out_kernel, final=final),
        grid=(n // tm,),
        in_specs=[row(0), row(0), row(0), row(COL_RA // 1024), row(COL_RB // 1024), w, w, w,
                  pl.BlockSpec((1, 1024), lambda i: (0, 0))],
        out_specs=row(0),
        out_shape=jax.ShapeDtypeStruct((n, 1024), F32),
        compiler_params=_cparams(("parallel",)),
        name="out_proj",
    )(x2, ya2, yb2, main2, main2, wn, wf, wo, fg)


def _rope_tables(t):
    inv = ROPE_THETA ** (-jnp.arange(0, HEAD_DIM, 2, dtype=F32) / HEAD_DIM)
    ang = jnp.arange(t, dtype=F32)[:, None] * inv[None, :]
    c, s = jnp.cos(ang), jnp.sin(ang)
    return jnp.concatenate([c, c, c, c], axis=1), jnp.concatenate([-s, s, -s, s], axis=1)


def _overlap_matrix():
    n = np.arange(LANES)[:, None] * CMP_STRIDE
    j = np.arange(LANES)[None, :] * SEL_BLOCK
    m = (n < j + SEL_BLOCK) & (n + CMP_BLOCK > j) & (np.arange(LANES)[None, :] < N_SEL)
    return jnp.asarray(m.astype(np.float32), dtype=BF16)


def _gate_expansion():
    m = np.zeros((3, LANES, NSA_HEADS * HEAD_DIM), np.float32)
    for br in range(3):
        for h in range(NSA_HEADS):
            m[br, br * NSA_HEADS + h, h * HEAD_DIM:(h + 1) * HEAD_DIM] = 1.0
    return jnp.asarray(m, dtype=BF16)


def _reorder_w_in(w):
    qa, kv, ga, za, qb, kb, vb, fb, zb, mg = jnp.split(
        w, np.cumsum([1024, 768, 48, 1024, 1024, 1024, 1024, 16, 1024])[:9].tolist(), axis=1)
    main = jnp.concatenate([qa, za, qb * (SCALE * LOG2E), kb, vb, zb, mg, kv], axis=1).astype(BF16)
    d = w.shape[0]
    small = jnp.concatenate([ga, jnp.zeros((d, LANES - 48), F32), fb, jnp.zeros((d, LANES - 16), F32)],
                            axis=1).astype(BF16)
    return main, small


def _compress_weights(w1, w2):
    eye = jnp.eye(NSA_GROUPS, dtype=F32)
    w1r = w1.reshape(CMP_BLOCK, HEAD_DIM, CMP_HIDDEN)
    half = CMP_BLOCK // 2

    def blockdiag(wl):
        return jnp.einsum('ldh,gk->lgdkh', wl, eye).reshape(half * NSA_GROUPS * HEAD_DIM,
                                                            NSA_GROUPS * CMP_HIDDEN).astype(BF16)

    wa, wb = blockdiag(w1r[:half]), blockdiag(w1r[half:])
    w2d = jnp.einsum('hd,gk,c->gkhcd', w2, eye, jnp.ones((2,), F32)).reshape(
        NSA_GROUPS, NSA_GROUPS * CMP_HIDDEN, LANES).astype(BF16)
    return wa, wb, w2d


def kernel(x, norm_g, w_in, b_forget, cmp_pe_k, cmp_w1_k, cmp_w2_k, cmp_pe_v, cmp_w1_v, cmp_w2_v,
           w_proj_nsa, w_proj_fox, w_out, final_g):
    b, t, d = x.shape
    depth = norm_g.shape[0]
    n = b * t
    nc = t // CMP_STRIDE
    assert d == D_MODEL and nc == LANES and t // SEL_BLOCK == N_SEL
    cos, sin = _rope_tables(t)
    ovl = _overlap_matrix()
    gmat = _gate_expansion()
    fg = final_g.reshape(1, d)
    x2 = x.reshape(n, d)
    for l in range(depth):
        w_main, w_small = _reorder_w_in(w_in[l])
        main2, small2, kc2, vc2 = _inproj(x2, norm_g[l].reshape(1, d), w_main, w_small)
        main3 = main2.reshape(b, t, MAIN_COLS)
        small3 = small2.reshape(b, t, SMALL_COLS)
        xk = kc2.reshape(b, nc, CMP_STRIDE * LANES)
        xv = vc2.reshape(b, nc, CMP_STRIDE * LANES)
        wak, wbk, w2k = _compress_weights(cmp_w1_k[l], cmp_w2_k[l])
        wav, wbv, w2v = _compress_weights(cmp_w1_v[l], cmp_w2_v[l])
        pek = jnp.broadcast_to(cmp_pe_k[l].reshape(1, -1), (8, CMP_BLOCK * HEAD_DIM))
        pev = jnp.broadcast_to(cmp_pe_v[l].reshape(1, -1), (8, CMP_BLOCK * HEAD_DIM))
        kc, vc = _compress(xk, xv, pek, pev, cmp_w1_k[l].astype(BF16), cmp_w1_v[l].astype(BF16),
                           wak, wbk, wav, wbv, w2k, w2v)
        qh, ks, kw, vs, vw, ge, o_cmp, nsel = _nsa_front(main3, small3, cos, sin, gmat, kc, vc, ovl)
        o_cs = _slc_attn(qh, ks, vs, nsel, ge, o_cmp)
        y_a = _win_attn(qh, kw, vw, ge, o_cs, main3)
        b_pad = jnp.concatenate([b_forget[l], jnp.zeros((LANES - FOX_HEADS,), F32)]).reshape(1, LANES)
        y_b = _fox_attn(main3, _decay(small3, b_pad))
        x2 = _out_proj(x2, y_a.reshape(n, 1024), y_b.reshape(n, 1024), main2,
                       w_proj_nsa[l].astype(BF16), w_proj_fox[l].astype(BF16), w_out[l].astype(BF16),
                       fg, final=(l == depth - 1))
    return x2.reshape(b, t, d)
```
